```python
import math
import jax
import jax.numpy as jnp
from jax import lax
import numpy as np

D_MODEL = 1024
BATCH = 8
SEQ = 2048
DEPTH = 4
DEC_BATCH = 32
DEC_SEQ = 1
PAST_LEN = 8192
PAGE_SIZE = 128

N_MIXERS = 3
N_LRU_LAYERS = len(range(0, DEPTH, N_MIXERS))
N_FOX_LAYERS = len(range(1, DEPTH, N_MIXERS))
N_POOL_LAYERS = len(range(2, DEPTH, N_MIXERS))

D_RNN = (4 * D_MODEL // 3) // 128 * 128
LRU_BLOCKS = 16
LRU_BW = D_RNN // LRU_BLOCKS
CONV_WIDTH = 4
LRU_C = 8.0

FOX_HEADS = 16
FOX_HEAD_DIM = D_MODEL // FOX_HEADS
Q_BLOCK = 128
FORGET_BIAS_INIT = 7.0
CACHE_FORGET_NOISE = 0.5

POOL_WINDOWS = (2, 4, 8, 16)
POOL_GROUPS = len(POOL_WINDOWS)
POOL_GW = D_MODEL // POOL_GROUPS
POOL_BUF = max(POOL_WINDOWS) - 1

MEM_LEN = 256
XA_HEADS = 4
XA_HEAD_DIM = D_MODEL // XA_HEADS

D_FF = 4 * D_MODEL
RMS_EPS = 1e-6
NEG_INF = -1e30

kernel_name = 'hybrid_lru_fox_pool_decoder_step'


def rmsnorm(x, g):
    xf = x.astype(jnp.float32)
    xf = xf * lax.rsqrt(jnp.mean(xf * xf, axis=-1, keepdims=True) + RMS_EPS)
    return (xf * g.astype(jnp.float32)).astype(x.dtype)


def sq_relu_mlp(xn, w_up, w_down):
    return jnp.square(jax.nn.relu(xn @ w_up)) @ w_down


def causal_depthwise_conv(u, buf, w, b):
    t = u.shape[1]
    full = jnp.concatenate([buf.astype(u.dtype), u], axis=1)
    out = b
    for k in range(CONV_WIDTH):
        out = out + full[:, k:k + t] * w[k]
    return out, full[:, t:]


def block_diag_linear(u, w, b):
    ub = u.reshape(u.shape[:-1] + (LRU_BLOCKS, LRU_BW))
    return jnp.einsum('btnc,ncd->btnd', ub, w).reshape(u.shape) + b


def rg_lru(u, h0, pos0, lam, w_rg, b_rg, w_ig, b_ig):
    f32 = jnp.float32
    t = u.shape[1]
    r = jax.nn.sigmoid(block_diag_linear(u, w_rg, b_rg).astype(f32))
    i = jax.nn.sigmoid(block_diag_linear(u, w_ig, b_ig).astype(f32))
    log_a = -LRU_C * r * jax.nn.softplus(-lam.astype(f32))
    mult = jnp.sqrt(-jnp.expm1(2.0 * log_a))
    first = (pos0 + jnp.arange(t)) == 0
    mult = jnp.where(first[None, :, None], 1.0, mult)
    xin = mult * i * u.astype(f32)

    def step(h, inp):
        a_t, x_t = inp
        h = a_t * h + x_t
        return h, h

    h_last, hs = lax.scan(step, h0.astype(f32),
                          (jnp.swapaxes(jnp.exp(log_a), 0, 1), jnp.swapaxes(xin, 0, 1)))
    return jnp.swapaxes(hs, 0, 1).astype(u.dtype), h_last.astype(h0.dtype)


def lru_mixer(xn, conv_buf, h0, pos0, w_in, conv_w, conv_b, lam, w_rg, b_rg, w_ig, b_ig, w_out):
    gate, u = jnp.split(xn @ w_in, 2, axis=-1)
    u, new_buf = causal_depthwise_conv(u, conv_buf, conv_w, conv_b)
    h, h_last = rg_lru(u, h0, pos0, lam, w_rg, b_rg, w_ig, b_ig)
    return (jax.nn.gelu(gate) * h) @ w_out, new_buf, h_last


def fox_project(xn, w_qkvf, b_f):
    b, t, _ = xn.shape
    q, k, v, fl = jnp.split(xn @ w_qkvf, [D_MODEL, 2 * D_MODEL, 3 * D_MODEL], axis=-1)
    shp = (b, t, FOX_HEADS, FOX_HEAD_DIM)
    lf = jax.nn.log_sigmoid((fl + b_f).astype(jnp.float32))
    return q.reshape(shp), k.reshape(shp), v.reshape(shp), lf


def forget_logits(q, k, fq, fk):
    s = jnp.einsum('bqhd,bkhd->bhqk', q, k).astype(jnp.float32) * (FOX_HEAD_DIM ** -0.5)
    return s + jnp.swapaxes(fq, 1, 2)[..., :, None] - jnp.swapaxes(fk, 1, 2)[..., None, :]


def fox_prompt(xn, w_qkvf, b_f, w_o):
    b, t, _ = xn.shape
    q, k, v, lf = fox_project(xn, w_qkvf, b_f)
    cum = jnp.cumsum(lf, axis=1)
    kpos = jnp.arange(t)

    def query_block(blk):
        start = blk * Q_BLOCK
        qb = lax.dynamic_slice_in_dim(q, start, Q_BLOCK, axis=1)
        fb = lax.dynamic_slice_in_dim(cum, start, Q_BLOCK, axis=1)
        qpos = start + jnp.arange(Q_BLOCK)
        s = forget_logits(qb, k, fb, cum)
        s = jnp.where(qpos[:, None] >= kpos[None, :], s, NEG_INF)
        pr = jax.nn.softmax(s, axis=-1).astype(v.dtype)
        return jnp.einsum('bhqk,bkhd->bqhd', pr, v)

    o = lax.map(query_block, jnp.arange(t // Q_BLOCK))
    o = jnp.moveaxis(o, 0, 1).reshape(b, t, D_MODEL)
    return o @ w_o, k, v, lf


def fox_sample(xn, k_pool, v_pool, lf_pool, layer_j, page_table, w_qkvf, b_f, w_o):
    b, t, _ = xn.shape
    q, k, v, lf = fox_project(xn, w_qkvf, b_f)
    k_past = k_pool[layer_j][page_table].reshape(b, -1, FOX_HEADS, FOX_HEAD_DIM).astype(k.dtype)
    v_past = v_pool[layer_j][page_table].reshape(b, -1, FOX_HEADS, FOX_HEAD_DIM).astype(v.dtype)
    lf_past = lf_pool[layer_j][page_table].reshape(b, -1, FOX_HEADS).astype(jnp.float32)
    n_past = k_past.shape[1]
    after = lax.cumsum(jnp.concatenate([lf_past[:, 1:], jnp.zeros_like(lf_past[:, :1])], axis=1),
                       axis=1, reverse=True)
    cum_new = jnp.cumsum(lf, axis=1)
    s_past = forget_logits(q, k_past, cum_new, -after)
    s_new = forget_logits(q, k, cum_new, cum_new)
    s_new = jnp.where(jnp.tril(jnp.ones((t, t), bool)), s_new, NEG_INF)
    pr = jax.nn.softmax(jnp.concatenate([s_past, s_new], axis=-1), axis=-1).astype(v.dtype)
    o = (jnp.einsum('bhqk,bkhd->bqhd', pr[..., :n_past], v_past)
         + jnp.einsum('bhqk,bkhd->bqhd', pr[..., n_past:], v))
    return o.reshape(b, t, D_MODEL) @ w_o, k, v, lf


def pool_mixer(xn, buf, pos0, w, b, scale):
    bsz, t, _ = xn.shape
    full = jnp.concatenate([buf.astype(xn.dtype), xn], axis=1)
    ff = full.astype(jnp.float32)
    csum = jnp.concatenate([jnp.zeros_like(ff[:, :1]), jnp.cumsum(ff, axis=1)], axis=1)
    end = csum[:, POOL_BUF + 1:]
    pos = pos0 + jnp.arange(t)
    means = []
    for g, win in enumerate(POOL_WINDOWS):
        ch = slice(g * POOL_GW, (g + 1) * POOL_GW)
        start = csum[:, POOL_BUF + 1 - win:POOL_BUF + 1 - win + t, ch]
        cnt = jnp.minimum(pos + 1, win).astype(jnp.float32)[None, :, None]
        means.append((end[..., ch] - start) / cnt)
    d = jnp.concatenate(means, axis=-1) - ff[:, POOL_BUF:]
    d = d.astype(xn.dtype).reshape(bsz, t, POOL_GROUPS, POOL_GW)
    y = jnp.einsum('btgc,gcd->btgd', d, w).reshape(bsz, t, D_MODEL) + b
    return y * scale, full[:, t:]


def memory_kv(mem, g, w_xkv):
    k, v = jnp.split(rmsnorm(mem, g) @ w_xkv, 2, axis=-1)
    shp = (mem.shape[0], mem.shape[1], XA_HEADS, XA_HEAD_DIM)
    return k.reshape(shp), v.reshape(shp)


def memory_attend(hn, mk, mv, w_xq, w_xo):
    bsz, t, _ = hn.shape
    q = (hn @ w_xq).reshape(bsz, t, XA_HEADS, XA_HEAD_DIM)
    s = jnp.einsum('bqhd,bkhd->bhqk', q, mk.astype(q.dtype)).astype(jnp.float32) * (XA_HEAD_DIM ** -0.5)
    pr = jax.nn.softmax(s, axis=-1).astype(mv.dtype)
    o = jnp.einsum('bhqk,bkhd->bqhd', pr, mv).reshape(bsz, t, D_MODEL)
    return o @ w_xo


def trunk(x, pos0, mem_k, mem_v, lru_h, lru_conv, pool_buf, fox_paged, p):
    hs, convs, pools, ks, vs, lfs = [], [], [], [], [], []
    for layer in range(DEPTH):
        kind, j = layer % N_MIXERS, layer // N_MIXERS
        xn = rmsnorm(x, p['norm_mix_g'][layer])
        if kind == 0:
            y, cb, hl = lru_mixer(xn, lru_conv[j], lru_h[j], pos0, p['w_lru_in'][j], p['lru_conv_w'][j],
                                  p['lru_conv_b'][j], p['lru_lambda'][j], p['lru_w_rg'][j], p['lru_b_rg'][j],
                                  p['lru_w_ig'][j], p['lru_b_ig'][j], p['w_lru_out'][j])
            convs.append(cb)
            hs.append(hl)
        elif kind == 1:
            if fox_paged is None:
                y, k, v, lf = fox_prompt(xn, p['w_fox_qkvf'][j], p['b_fox_f'][j], p['w_fox_o'][j])
            else:
                k_pool, v_pool, lf_pool, page_table = fox_paged
                y, k, v, lf = fox_sample(xn, k_pool, v_pool, lf_pool, j, page_table,
                                         p['w_fox_qkvf'][j], p['b_fox_f'][j], p['w_fox_o'][j])
            ks.append(k)
            vs.append(v)
            lfs.append(lf)
        else:
            y, pb = pool_mixer(xn, pool_buf[j], pos0, p['w_pool'][j], p['b_pool'][j], p['pool_scale'][j])
            pools.append(pb)
        x = x + y
        x = x + memory_attend(rmsnorm(x, p['norm_x_g'][layer]), mem_k[layer], mem_v[layer],
                              p['w_xq'][layer], p['w_xo'][layer])
        x = x + sq_relu_mlp(rmsnorm(x, p['norm_mlp_g'][layer]), p['w_up'][layer], p['w_down'][layer])
    y = rmsnorm(x, p['final_norm_g'])
    return y, jnp.stack(hs), jnp.stack(convs), jnp.stack(pools), jnp.stack(ks), jnp.stack(vs), jnp.stack(lfs)


def setup_inputs(seed: int = 0) -> dict:
    key = jax.random.key(seed)
    ks = iter(list(jax.random.split(key, 64)))
    f32 = jnp.float32

    def nrm(shape, scale):
        return scale * jax.random.normal(next(ks), shape, f32)

    def gain(shape):
        return 1.0 + nrm(shape, 0.05)

    n_pages = PAST_LEN // PAGE_SIZE
    n_pool = (5 * DEC_BATCH * n_pages) // 4
    page_table = jax.random.permutation(next(ks), n_pool)[:DEC_BATCH * n_pages]
    page_table = page_table.reshape(DEC_BATCH, n_pages).astype(jnp.int32)

    u = jax.random.uniform(next(ks), (N_LRU_LAYERS, D_RNN), f32, 0.9, 0.999)
    a0 = u ** (1.0 / LRU_C)
    lru_lambda = jnp.log(a0) - jnp.log1p(-a0)

    return {
        'x_prompt': nrm((BATCH, SEQ, D_MODEL), 1.0),
        'x_sample': nrm((DEC_BATCH, DEC_SEQ, D_MODEL), 1.0),
        'mem_prompt': nrm((BATCH, MEM_LEN, D_MODEL), 1.0),
        'cache_fox_k': nrm((N_FOX_LAYERS, n_pool, PAGE_SIZE, FOX_HEADS, FOX_HEAD_DIM), 1.0),
        'cache_fox_v': nrm((N_FOX_LAYERS, n_pool, PAGE_SIZE, FOX_HEADS, FOX_HEAD_DIM), 1.0),
        'cache_fox_lf': jax.nn.log_sigmoid(
            FORGET_BIAS_INIT + nrm((N_FOX_LAYERS, n_pool, PAGE_SIZE, FOX_HEADS), CACHE_FORGET_NOISE)),
        'cache_mem_k': nrm((DEPTH, DEC_BATCH, MEM_LEN, XA_HEADS, XA_HEAD_DIM), 1.0),
        'cache_mem_v': nrm((DEPTH, DEC_BATCH, MEM_LEN, XA_HEADS, XA_HEAD_DIM), 1.0),
        'state_lru_h': nrm((N_LRU_LAYERS, DEC_BATCH, D_RNN), 0.5),
        'state_lru_conv': nrm((N_LRU_LAYERS, DEC_BATCH, CONV_WIDTH - 1, D_RNN), 1.0),
        'state_pool': nrm((N_POOL_LAYERS, DEC_BATCH, POOL_BUF, D_MODEL), 1.0),
        'page_table': page_table,
        'norm_mix_g': gain((DEPTH, D_MODEL)),
        'norm_mem_g': gain((DEPTH, D_MODEL)),
        'norm_x_g': gain((DEPTH, D_MODEL)),
        'norm_mlp_g': gain((DEPTH, D_MODEL)),
        'final_norm_g': gain((D_MODEL,)),
        'w_lru_in': nrm((N_LRU_LAYERS, D_MODEL, 2 * D_RNN), D_MODEL ** -0.5),
        'lru_conv_w': nrm((N_LRU_LAYERS, CONV_WIDTH, D_RNN), CONV_WIDTH ** -0.5),
        'lru_conv_b': nrm((N_LRU_LAYERS, D_RNN), 0.01),
        'lru_w_rg': nrm((N_LRU_LAYERS, LRU_BLOCKS, LRU_BW, LRU_BW), LRU_BW ** -0.5),
        'lru_b_rg': nrm((N_LRU_LAYERS, D_RNN), 0.01),
        'lru_w_ig': nrm((N_LRU_LAYERS, LRU_BLOCKS, LRU_BW, LRU_BW), LRU_BW ** -0.5),
        'lru_b_ig': nrm((N_LRU_LAYERS, D_RNN), 0.01),
        'lru_lambda': lru_lambda,
        'w_lru_out': nrm((N_LRU_LAYERS, D_RNN, D_MODEL), D_RNN ** -0.5),
        'w_fox_qkvf': nrm((N_FOX_LAYERS, D_MODEL, 3 * D_MODEL + FOX_HEADS), D_MODEL ** -0.5),
        'b_fox_f': FORGET_BIAS_INIT + nrm((N_FOX_LAYERS, FOX_HEADS), 0.1),
        'w_fox_o': nrm((N_FOX_LAYERS, D_MODEL, D_MODEL), D_MODEL ** -0.5),
        'w_pool': nrm((N_POOL_LAYERS, POOL_GROUPS, POOL_GW, POOL_GW), POOL_GW ** -0.5),
        'b_pool': nrm((N_POOL_LAYERS, D_MODEL), 0.01),
        'pool_scale': 1.0 + nrm((N_POOL_LAYERS, D_MODEL), 0.1),
        'w_xq': nrm((DEPTH, D_MODEL, D_MODEL), D_MODEL ** -0.5),
        'w_xkv': nrm((DEPTH, D_MODEL, 2 * D_MODEL), D_MODEL ** -0.5),
        'w_xo': nrm((DEPTH, D_MODEL, D_MODEL), D_MODEL ** -0.5),
        'w_up': nrm((DEPTH, D_MODEL, D_FF), D_MODEL ** -0.5),
        'w_down': nrm((DEPTH, D_FF, D_MODEL), D_FF ** -0.5),
    }


def reference(x_prompt, x_sample, mem_prompt, cache_fox_k, cache_fox_v, cache_fox_lf,
              cache_mem_k, cache_mem_v, state_lru_h, state_lru_conv, state_pool, page_table,
              norm_mix_g, norm_mem_g, norm_x_g, norm_mlp_g, final_norm_g,
              w_lru_in, lru_conv_w, lru_conv_b, lru_w_rg, lru_b_rg, lru_w_ig, lru_b_ig, lru_lambda, w_lru_out,
              w_fox_qkvf, b_fox_f, w_fox_o, w_pool, b_pool, pool_scale,
              w_xq, w_xkv, w_xo, w_up, w_down):
    p = dict(norm_mix_g=norm_mix_g, norm_x_g=norm_x_g, norm_mlp_g=norm_mlp_g, final_norm_g=final_norm_g,
             w_lru_in=w_lru_in, lru_conv_w=lru_conv_w, lru_conv_b=lru_conv_b, lru_w_rg=lru_w_rg,
             lru_b_rg=lru_b_rg, lru_w_ig=lru_w_ig, lru_b_ig=lru_b_ig, lru_lambda=lru_lambda,
             w_lru_out=w_lru_out, w_fox_qkvf=w_fox_qkvf, b_fox_f=b_fox_f, w_fox_o=w_fox_o,
             w_pool=w_pool, b_pool=b_pool, pool_scale=pool_scale, w_xq=w_xq, w_xo=w_xo,
             w_up=w_up, w_down=w_down)

    bsz = x_prompt.shape[0]
    mem_kv_list = [memory_kv(mem_prompt, norm_mem_g[l], w_xkv[l]) for l in range(DEPTH)]
    mem_k_p = jnp.stack([kv[0] for kv in mem_kv_list])
    mem_v_p = jnp.stack([kv[1] for kv in mem_kv_list])
    dt = x_prompt.dtype
    h0 = jnp.zeros((N_LRU_LAYERS, bsz, D_RNN), dt)
    c0 = jnp.zeros((N_LRU_LAYERS, bsz, CONV_WIDTH - 1, D_RNN), dt)
    pb0 = jnp.zeros((N_POOL_LAYERS, bsz, POOL_BUF, D_MODEL), dt)
    y_prompt, lru_h_p, lru_conv_p, pool_p, fox_k_p, fox_v_p, fox_lf_p = trunk(
        x_prompt, 0, mem_k_p, mem_v_p, h0, c0, pb0, None, p)

    pos_s = page_table.shape[1] * PAGE_SIZE
    y_sample, lru_h_s, lru_conv_s, pool_s, fox_k_s, fox_v_s, fox_lf_s = trunk(
        x_sample, pos_s, cache_mem_k, cache_mem_v, state_lru_h, state_lru_conv, state_pool,
        (cache_fox_k, cache_fox_v, cache_fox_lf, page_table), p)

    return (y_prompt, y_sample,
            lru_h_p, lru_conv_p, pool_p, fox_k_p, fox_v_p, fox_lf_p, mem_k_p, mem_v_p,
            lru_h_s, lru_conv_s, pool_s, fox_k_s, fox_v_s, fox_lf_s)
```

```python
import functools

import jax
import jax.numpy as jnp
from jax import lax
from jax.experimental import pallas as pl
from jax.experimental.pallas import tpu as pltpu

F32 = jnp.float32
BF16 = jnp.bfloat16

RMS_EPS = 1e-6
NEG_INF = -1e30
LRU_C = 8.0
CONV_WIDTH = 4
FOX_HEADS = 16
XA_HEADS = 4
POOL_WINDOWS = (2, 4, 8, 16)
N_MIXERS = 3

LANES = 128
SUBLANES = 8
HALO = 16

_NT = (((1,), (1,)), ((), ()))


def _tile(n, target):
    t = 1
    while t * 2 <= min(n, target):
        t *= 2
    while t > 1 and n % t:
        t //= 2
    return t if (n % t == 0 and t >= SUBLANES) else n


def _params(*sem):
    return pltpu.CompilerParams(dimension_semantics=sem)


def _rmsnorm(x, g):
    x = x.astype(F32)
    x = x * lax.rsqrt(jnp.mean(x * x, axis=-1, keepdims=True) + RMS_EPS)
    return x * g


def _softplus(z):
    return jnp.maximum(z, 0.0) + jnp.log1p(jnp.exp(-jnp.abs(z)))


def _gelu_tanh(x):
    c = 0.7978845608028654
    return x * (0.5 * (1.0 + jnp.tanh(c * (x + 0.044715 * (x * x * x)))))


def _norm_matmul_kernel(x_ref, g_ref, w_ref, o_ref, xn_ref, *, scale):
    @pl.when(pl.program_id(1) == 0)
    def _():
        xn_ref[...] = _rmsnorm(x_ref[...], g_ref[...]).astype(BF16)

    acc = jnp.dot(xn_ref[...], w_ref[...], preferred_element_type=F32)
    if scale != 1.0:
        acc = acc * scale
    o_ref[...] = acc.astype(o_ref.dtype)


def norm_matmul(x, g, w, *, tn, out_dtype=F32, scale=1.0, name="norm_matmul"):
    m, d = x.shape
    n = w.shape[1]
    tm = _tile(m, 1024)
    return pl.pallas_call(
        functools.partial(_norm_matmul_kernel, scale=scale),
        grid=(m // tm, n // tn),
        in_specs=[
            pl.BlockSpec((tm, d), lambda i, j: (i, 0)),
            pl.BlockSpec((1, d), lambda i, j: (0, 0)),
            pl.BlockSpec((d, tn), lambda i, j: (0, j)),
        ],
        out_specs=pl.BlockSpec((tm, tn), lambda i, j: (i, j)),
        out_shape=jax.ShapeDtypeStruct((m, n), out_dtype),
        scratch_shapes=[pltpu.VMEM((tm, d), BF16)],
        compiler_params=_params("parallel", "arbitrary"),
        name=name,
    )(x, g.reshape(1, d), w)


def _fox_proj_kernel(x_ref, g_ref, w_ref, wf_ref, q_ref, k_ref, v_ref, f_ref, xn_ref, *, q_scale):
    j = pl.program_id(1)

    @pl.when(j == 0)
    def _():
        xn_ref[...] = _rmsnorm(x_ref[...], g_ref[...]).astype(BF16)

    @pl.when(j == 0)
    def _():
        acc = jnp.dot(xn_ref[...], w_ref[...], preferred_element_type=F32)
        q_ref[...] = (acc * q_scale).astype(q_ref.dtype)

    @pl.when(j == 1)
    def _():
        k_ref[...] = jnp.dot(xn_ref[...], w_ref[...], preferred_element_type=F32)

    @pl.when(j == 2)
    def _():
        v_ref[...] = jnp.dot(xn_ref[...], w_ref[...], preferred_element_type=F32)

    @pl.when(j == 3)
    def _():
        f_ref[...] = jnp.dot(xn_ref[...], wf_ref[...], preferred_element_type=F32)


def fox_proj(x, g, w_qkv, w_f, *, q_dtype, q_scale):
    m, d = x.shape
    tm = _tile(m, 512)
    row = lambda i, j: (i, 0)
    return pl.pallas_call(
        functools.partial(_fox_proj_kernel, q_scale=q_scale),
        grid=(m // tm, 4),
        in_specs=[
            pl.BlockSpec((tm, d), row),
            pl.BlockSpec((1, d), lambda i, j: (0, 0)),
            pl.BlockSpec((d, d), lambda i, j: (0, jnp.minimum(j, 2))),
            pl.BlockSpec((d, LANES), lambda i, j: (0, 0)),
        ],
        out_specs=[
            pl.BlockSpec((tm, d), row),
            pl.BlockSpec((tm, d), row),
            pl.BlockSpec((tm, d), row),
            pl.BlockSpec((tm, LANES), row),
        ],
        out_shape=[
            jax.ShapeDtypeStruct((m, d), q_dtype),
            jax.ShapeDtypeStruct((m, d), F32),
            jax.ShapeDtypeStruct((m, d), F32),
            jax.ShapeDtypeStruct((m, LANES), F32),
        ],
        scratch_shapes=[pltpu.VMEM((tm, d), BF16)],
        compiler_params=_params("parallel", "arbitrary"),
        name="fox_proj",
    )(x, g.reshape(1, d), w_qkv, w_f)


def _mem_kv_kernel(x_ref, g_ref, w_ref, k_ref, v_ref, xn_ref):
    j = pl.program_id(2)

    @pl.when(j == 0)
    def _():
        xn_ref[...] = _rmsnorm(x_ref[...], g_ref[...]).astype(BF16)
        k_ref[...] = jnp.dot(xn_ref[...], w_ref[...], preferred_element_type=F32)

    @pl.when(j == 1)
    def _():
        v_ref[...] = jnp.dot(xn_ref[...], w_ref[...], preferred_element_type=F32)


def mem_kv(mem, g_all, w_all):
    m, d = mem.shape
    depth = g_all.shape[0]
    tm = _tile(m, 512)
    out_spec = pl.BlockSpec((None, tm, d), lambda l, i, j: (l, i, 0))
    return pl.pallas_call(
        _mem_kv_kernel,
        grid=(depth, m // tm, 2),
        in_specs=[
            pl.BlockSpec((tm, d), lambda l, i, j: (i, 0)),
            pl.BlockSpec((None, 1, d), lambda l, i, j: (l, 0, 0)),
            pl.BlockSpec((None, d, d), lambda l, i, j: (l, 0, j)),
        ],
        out_specs=[out_spec, out_spec],
        out_shape=[jax.ShapeDtypeStruct((depth, m, d), F32)] * 2,
        scratch_shapes=[pltpu.VMEM((tm, d), BF16)],
        compiler_params=_params("parallel", "parallel", "arbitrary"),
        name="mem_kv",
    )(mem, g_all.reshape(depth, 1, d), w_all)


def _matmul_res_kernel(h_ref, w_ref, r_ref, o_ref):
    o_ref[...] = r_ref[...] + jnp.dot(h_ref[...].astype(BF16), w_ref[...], preferred_element_type=F32)


def matmul_res(h, w, res, *, name="matmul_res"):
    m, k = h.shape
    n = w.shape[1]
    tm = _tile(m, 512)
    return pl.pallas_call(
        _matmul_res_kernel,
        grid=(m // tm,),
        in_specs=[
            pl.BlockSpec((tm, k), lambda i: (i, 0)),
            pl.BlockSpec((k, n), lambda i: (0, 0)),
            pl.BlockSpec((tm, n), lambda i: (i, 0)),
        ],
        out_specs=pl.BlockSpec((tm, n), lambda i: (i, 0)),
        out_shape=jax.ShapeDtypeStruct((m, n), F32),
        compiler_params=_params("parallel"),
        name=name,
    )(h, w, res)


def _mlp_kernel(*refs, final_norm):
    if final_norm:
        x_ref, g_ref, wu_ref, wd_ref, gf_ref, o_ref, y_ref, xn_ref, acc_ref = refs
    else:
        x_ref, g_ref, wu_ref, wd_ref, o_ref, xn_ref, acc_ref = refs
    j = pl.program_id(1)

    @pl.when(j == 0)
    def _():
        xn_ref[...] = _rmsnorm(x_ref[...], g_ref[...]).astype(BF16)
        acc_ref[...] = jnp.zeros_like(acc_ref)

    h = jnp.dot(xn_ref[...], wu_ref[...], preferred_element_type=F32)
    h = jnp.square(jnp.maximum(h, 0.0)).astype(BF16)
    acc_ref[...] += jnp.dot(h, wd_ref[...], preferred_element_type=F32)

    @pl.when(j == pl.num_programs(1) - 1)
    def _():
        out = x_ref[...] + acc_ref[...]
        o_ref[...] = out
        if final_norm:
            y_ref[...] = _rmsnorm(out, gf_ref[...])


def mlp(x, g, w_up, w_down, final_g=None):
    m, d = x.shape
    f = w_up.shape[1]
    tm = _tile(m, 1024)
    tf = _tile(f, 512)
    row = lambda i, j: (i, 0)
    vec = pl.BlockSpec((1, d), lambda i, j: (0, 0))
    in_specs = [
        pl.BlockSpec((tm, d), row),
        vec,
        pl.BlockSpec((d, tf), lambda i, j: (0, j)),
        pl.BlockSpec((tf, d), lambda i, j: (j, 0)),
    ]
    args = [x, g.reshape(1, d), w_up, w_down]
    out_specs = [pl.BlockSpec((tm, d), row)]
    out_shape = [jax.ShapeDtypeStruct((m, d), F32)]
    if final_g is not None:
        in_specs.append(vec)
        args.append(final_g.reshape(1, d))
        out_specs.append(pl.BlockSpec((tm, d), row))
        out_shape.append(jax.ShapeDtypeStruct((m, d), F32))
    outs = pl.pallas_call(
        functools.partial(_mlp_kernel, final_norm=final_g is not None),
        grid=(m // tm, f // tf),
        in_specs=in_specs,
        out_specs=out_specs,
        out_shape=out_shape,
        scratch_shapes=[pltpu.VMEM((tm, d), BF16), pltpu.VMEM((tm, d), F32)],
        compiler_params=_params("parallel", "arbitrary"),
        name="mlp",
    )(*args)
    return outs if final_g is not None else (outs[0], None)


def _mem_attn_kernel(q_ref, k_ref, v_ref, o_ref):
    d = q_ref.shape[1]
    dh = d // XA_HEADS
    for h in range(XA_HEADS):
        sl = slice(h * dh, (h + 1) * dh)
        s = lax.dot_general(q_ref[:, sl], k_ref[:, sl].astype(BF16), _NT, preferred_element_type=F32)
        e = jnp.exp(s - jnp.max(s, axis=1, keepdims=True))
        p = e * (1.0 / jnp.sum(e, axis=1, keepdims=True))
        o = jnp.dot(p.astype(BF16), v_ref[:, sl].astype(BF16), preferred_element_type=F32)
        o_ref[:, sl] = o.astype(o_ref.dtype)


def mem_attn(q, k_all, v_all, layer):
    bsz, t, d = q.shape
    n_mem = k_all.shape[2]
    tm = _tile(t, 512)
    kv_spec = pl.BlockSpec((None, None, n_mem, d), lambda b, i: (layer, b, 0, 0))
    return pl.pallas_call(
        _mem_attn_kernel,
        grid=(bsz, t // tm),
        in_specs=[pl.BlockSpec((None, tm, d), lambda b, i: (b, i, 0)), kv_spec, kv_spec],
        out_specs=pl.BlockSpec((None, tm, d), lambda b, i: (b, i, 0)),
        out_shape=jax.ShapeDtypeStruct((bsz, t, d), BF16),
        compiler_params=_params("parallel", "parallel"),
        name="mem_attn",
    )(q, k_all, v_all)


def _lru_gates(uc, wg_ref, brg, big, lam):
    half = uc.shape[1] // 2
    ucb = uc.astype(BF16)
    g0 = jnp.dot(ucb[:, :half], wg_ref[0], preferred_element_type=F32)
    g1 = jnp.dot(ucb[:, half:], wg_ref[1], preferred_element_type=F32)
    rg = jnp.concatenate([g0[:, :half], g1[:, :half]], axis=1) + brg
    ig = jnp.concatenate([g0[:, half:], g1[:, half:]], axis=1) + big
    r = jax.nn.sigmoid(rg)
    i = jax.nn.sigmoid(ig)
    log_a = (-LRU_C * r) * _softplus(-lam)
    a = jnp.exp(log_a)
    mult = jnp.sqrt(-jnp.tanh(log_a) * (a * a + 1.0))
    return a, mult, i


def _scan8(a8, x8, row8):
    for s in (1, 2, 4):
        keep = row8 >= s
        xs = jnp.where(keep, pltpu.roll(x8, s, 0), 0.0)
        a_s = jnp.where(keep, pltpu.roll(a8, s, 0), 1.0)
        x8 = x8 + a8 * xs
        a8 = a8 * a_s
    return a8, x8


def _lru_prompt_kernel(gate_ref, u_ref, buf_ref, h0_ref, cw_ref, cb_ref, wg_ref, brg_ref, big_ref, lam_ref,
                       y_ref, hl_ref, cbuf_ref, ufull, a_scr, x_scr, hc, *, first_at_zero):
    i = pl.program_id(1)
    tt, c = u_ref.shape
    pad = SUBLANES

    @pl.when(i == 0)
    def _():
        ufull[0:pad, :] = buf_ref[...]
        hc[...] = h0_ref[...]

    @pl.when(i > 0)
    def _():
        ufull[0:pad, :] = ufull[tt:tt + pad, :]

    u = u_ref[...]
    ufull[pad:pad + tt, :] = u
    uc = cb_ref[...]
    for k in range(CONV_WIDTH - 1):
        off = pad - (CONV_WIDTH - 1) + k
        uc = uc + ufull[off:off + tt, :] * cw_ref[k:k + 1, :]
    uc = uc + u * cw_ref[CONV_WIDTH - 1:CONV_WIDTH, :]

    a, mult, ig = _lru_gates(uc, wg_ref, brg_ref[...], big_ref[...], lam_ref[...])
    if first_at_zero:
        row = lax.broadcasted_iota(jnp.int32, (tt, 1), 0)
        mult = jnp.where(jnp.logical_and(row == 0, i == 0), 1.0, mult)
    x_scr[...] = (mult * ig) * uc
    a_scr[...] = a

    row8 = lax.broadcasted_iota(jnp.int32, (SUBLANES, c), 0)

    def body(r, h):
        off = pl.multiple_of(r * SUBLANES, SUBLANES)
        a8, x8 = _scan8(a_scr[pl.ds(off, SUBLANES), :], x_scr[pl.ds(off, SUBLANES), :], row8)
        h8 = x8 + a8 * h
        x_scr[pl.ds(off, SUBLANES), :] = h8
        return h8[SUBLANES - 1:SUBLANES, :]

    h = lax.fori_loop(0, tt // SUBLANES, body, hc[...])
    hc[...] = h
    y_ref[...] = (_gelu_tanh(gate_ref[...]) * x_scr[...]).astype(y_ref.dtype)

    @pl.when(i == pl.num_programs(1) - 1)
    def _():
        hl_ref[...] = h
        cbuf_ref[...] = ufull[tt:tt + pad, :]


def lru_prompt(gu, conv_buf, h0, conv_w, conv_b, wg, b_rg, b_ig, lam, *, pos0):
    bsz, t, c2 = gu.shape
    c = c2 // 2
    tt = _tile(t, 256)
    pad = SUBLANES
    buf8 = jnp.pad(conv_buf, ((0, 0), (pad - (CONV_WIDTH - 1), 0), (0, 0)))
    vec = pl.BlockSpec((1, c), lambda b, i: (0, 0))
    y, h_last, cbuf = pl.pallas_call(
        functools.partial(_lru_prompt_kernel, first_at_zero=(pos0 == 0)),
        grid=(bsz, t // tt),
        in_specs=[
            pl.BlockSpec((None, tt, c), lambda b, i: (b, i, 0)),
            pl.BlockSpec((None, tt, c), lambda b, i: (b, i, 1)),
            pl.BlockSpec((None, pad, c), lambda b, i: (b, 0, 0)),
            pl.BlockSpec((None, 1, c), lambda b, i: (b, 0, 0)),
            pl.BlockSpec((CONV_WIDTH, c), lambda b, i: (0, 0)),
            vec,
            pl.BlockSpec((2, c // 2, c), lambda b, i: (0, 0, 0)),
            vec, vec, vec,
        ],
        out_specs=[
            pl.BlockSpec((None, tt, c), lambda b, i: (b, i, 0)),
            pl.BlockSpec((None, 1, c), lambda b, i: (b, 0, 0)),
            pl.BlockSpec((None, pad, c), lambda b, i: (b, 0, 0)),
        ],
        out_shape=[
            jax.ShapeDtypeStruct((bsz, t, c), BF16),
            jax.ShapeDtypeStruct((bsz, 1, c), F32),
            jax.ShapeDtypeStruct((bsz, pad, c), F32),
        ],
        scratch_shapes=[
            pltpu.VMEM((tt + pad, c), F32),
            pltpu.VMEM((tt, c), F32),
            pltpu.VMEM((tt, c), F32),
            pltpu.VMEM((1, c), F32),
        ],
        compiler_params=_params("parallel", "arbitrary"),
        name="lru_prompt",
    )(gu, gu, buf8, h0.reshape(bsz, 1, c), conv_w, conv_b.reshape(1, c), wg,
      b_rg.reshape(1, c), b_ig.reshape(1, c), lam.reshape(1, c))
    return y, h_last.reshape(bsz, c), cbuf[:, pad - (CONV_WIDTH - 1):, :]


def _lru_step_kernel(gate_ref, u_ref, buf_ref, h0_ref, cw_ref, cb_ref, wg_ref, brg_ref, big_ref, lam_ref,
                     y_ref, h_ref, *, first_at_zero):
    u = u_ref[...]
    uc = cb_ref[...]
    for k in range(CONV_WIDTH - 1):
        uc = uc + buf_ref[k] * cw_ref[k:k + 1, :]
    uc = uc + u * cw_ref[CONV_WIDTH - 1:CONV_WIDTH, :]
    a, mult, ig = _lru_gates(uc, wg_ref, brg_ref[...], big_ref[...], lam_ref[...])
    if first_at_zero:
        mult = jnp.ones_like(mult)
    h = a * h0_ref[...] + (mult * ig) * uc
    h_ref[...] = h
    y_ref[...] = (_gelu_tanh(gate_ref[...]) * h).astype(y_ref.dtype)


def lru_step(gu, conv_buf, h0, conv_w, conv_b, wg, b_rg, b_ig, lam, *, pos0):
    bsz, c2 = gu.shape
    c = c2 // 2
    buf_t = jnp.swapaxes(conv_buf, 0, 1)
    full = lambda *shape: pl.BlockSpec(shape, lambda i: (0,) * len(shape))
    y, h = pl.pallas_call(
        functools.partial(_lru_step_kernel, first_at_zero=(pos0 == 0)),
        grid=(1,),
        in_specs=[
            pl.BlockSpec((bsz, c), lambda i: (0, 0)),
            pl.BlockSpec((bsz, c), lambda i: (0, 1)),
            full(CONV_WIDTH - 1, bsz, c),
            full(bsz, c),
            full(CONV_WIDTH, c),
            full(1, c),
            full(2, c // 2, c),
            full(1, c), full(1, c), full(1, c),
        ],
        out_specs=[full(bsz, c), full(bsz, c)],
        out_shape=[jax.ShapeDtypeStruct((bsz, c), BF16), jax.ShapeDtypeStruct((bsz, c), F32)],
        compiler_params=_params("arbitrary"),
        name="lru_step",
    )(gu, gu, buf_t, h0, conv_w, conv_b.reshape(1, c), wg,
      b_rg.reshape(1, c), b_ig.reshape(1, c), lam.reshape(1, c))
    new_buf = jnp.concatenate([conv_buf[:, 1:], gu[:, None, c:]], axis=1)
    return y, h, new_buf


def _pool_groups(xn, shifted, cnt, w_ref, b, scale):
    d = xn.shape[1]
    gw = d // len(POOL_WINDOWS)
    ys = []
    for gi, win in enumerate(POOL_WINDOWS):
        ch = slice(gi * gw, (gi + 1) * gw)
        s = xn[:, ch]
        for k in range(1, win):
            s = s + shifted(k, ch)
        dd = (s / cnt(win) - xn[:, ch]).astype(BF16)
        ys.append(jnp.dot(dd, w_ref[gi], preferred_element_type=F32))
    return (jnp.concatenate(ys, axis=1) + b) * scale


def _pool_prompt_kernel(x_ref, buf_ref, g_ref, w_ref, b_ref, sc_ref, o_ref, nb_ref, full, *, pos0):
    i = pl.program_id(1)
    tm, d = x_ref.shape

    @pl.when(i == 0)
    def _():
        full[0:HALO, :] = buf_ref[...]

    @pl.when(i > 0)
    def _():
        full[0:HALO, :] = full[tm:tm + HALO, :]

    x = x_ref[...]
    xn = _rmsnorm(x, g_ref[...])
    full[HALO:HALO + tm, :] = xn
    pos = pos0 + i * tm + lax.broadcasted_iota(jnp.int32, (tm, 1), 0)
    y = _pool_groups(
        xn,
        lambda k, ch: full[HALO - k:HALO - k + tm, ch],
        lambda win: jnp.minimum(pos + 1, win).astype(F32),
        w_ref, b_ref[...], sc_ref[...])
    o_ref[...] = x + y

    @pl.when(i == pl.num_programs(1) - 1)
    def _():
        nb_ref[...] = full[tm:tm + HALO, :]


def pool_prompt(x, buf, g, w, b, scale, *, pos0):
    bsz, t, d = x.shape
    nbuf = buf.shape[1]
    tm = _tile(t, 512)
    buf16 = jnp.pad(buf, ((0, 0), (HALO - nbuf, 0), (0, 0)))
    vec = pl.BlockSpec((1, d), lambda b_, i: (0, 0))
    ng = len(POOL_WINDOWS)
    out, nb = pl.pallas_call(
        functools.partial(_pool_prompt_kernel, pos0=pos0),
        grid=(bsz, t // tm),
        in_specs=[
            pl.BlockSpec((None, tm, d), lambda b_, i: (b_, i, 0)),
            pl.BlockSpec((None, HALO, d), lambda b_, i: (b_, 0, 0)),
            vec,
            pl.BlockSpec((ng, d // ng, d // ng), lambda b_, i: (0, 0, 0)),
            vec, vec,
        ],
        out_specs=[
            pl.BlockSpec((None, tm, d), lambda b_, i: (b_, i, 0)),
            pl.BlockSpec((None, HALO, d), lambda b_, i: (b_, 0, 0)),
        ],
        out_shape=[jax.ShapeDtypeStruct((bsz, t, d), F32), jax.ShapeDtypeStruct((bsz, HALO, d), F32)],
        scratch_shapes=[pltpu.VMEM((tm + HALO, d), F32)],
        compiler_params=_params("parallel", "arbitrary"),
        name="pool_prompt",
    )(x, buf16, g.reshape(1, d), w, b.reshape(1, d), scale.reshape(1, d))
    return out, nb[:, HALO - nbuf:, :]


def _pool_step_kernel(x_ref, buf_ref, g_ref, w_ref, b_ref, sc_ref, o_ref, xn_ref, *, pos0):
    x = x_ref[...]
    xn = _rmsnorm(x, g_ref[...])
    nbuf = buf_ref.shape[0]
    y = _pool_groups(
        xn,
        lambda k, ch: buf_ref[nbuf - k, :, ch],
        lambda win: float(min(pos0 + 1, win)),
        w_ref, b_ref[...], sc_ref[...])
    o_ref[...] = x + y
    xn_ref[...] = xn


def pool_step(x, buf, g, w, b, scale, *, pos0):
    bsz, d = x.shape
    nbuf = buf.shape[1]
    ng = len(POOL_WINDOWS)
    buf_t = jnp.swapaxes(buf, 0, 1)
    full = lambda *shape: pl.BlockSpec(shape, lambda i: (0,) * len(shape))
    out, xn = pl.pallas_call(
        functools.partial(_pool_step_kernel, pos0=pos0),
        grid=(1,),
        in_specs=[full(bsz, d), full(nbuf, bsz, d), full(1, d), full(ng, d // ng, d // ng), full(1, d), full(1, d)],
        out_specs=[full(bsz, d), full(bsz, d)],
        out_shape=[jax.ShapeDtypeStruct((bsz, d), F32)] * 2,
        compiler_params=_params("arbitrary"),
        name="pool_step",
    )(x, buf_t, g.reshape(1, d), w, b.reshape(1, d), scale.reshape(1, d))
    return out, jnp.concatenate([buf[:, 1:], xn[:, None, :]], axis=1)


def _lf_cumsum_kernel(fl_ref, b_ref, lf_ref, cum_ref, cumt_ref):
    t, w = fl_ref.shape
    lf_ref[...] = -_softplus(-(fl_ref[...] + b_ref[...]))
    row8 = lax.broadcasted_iota(jnp.int32, (SUBLANES, w), 0)

    def body(r, carry):
        off = pl.multiple_of(r * SUBLANES, SUBLANES)
        x8 = lf_ref[pl.ds(off, SUBLANES), :]
        for s in (1, 2, 4):
            x8 = x8 + jnp.where(row8 >= s, pltpu.roll(x8, s, 0), 0.0)
        c8 = x8 + carry
        cum_ref[pl.ds(off, SUBLANES), :] = c8
        return c8[SUBLANES - 1:SUBLANES, :]

    lax.fori_loop(0, t // SUBLANES, body, jnp.zeros((1, w), F32))
    cumt_ref[...] = cum_ref[...].T


def lf_cumsum(fl, b_pad):
    bsz, t, w = fl.shape
    blk = pl.BlockSpec((None, t, w), lambda b: (b, 0, 0))
    return pl.pallas_call(
        _lf_cumsum_kernel,
        grid=(bsz,),
        in_specs=[blk, pl.BlockSpec((1, w), lambda b: (0, 0))],
        out_specs=[blk, blk, pl.BlockSpec((None, w, t), lambda b: (b, 0, 0))],
        out_shape=[jax.ShapeDtypeStruct((bsz, t, w), F32)] * 2 + [jax.ShapeDtypeStruct((bsz, w, t), F32)],
        compiler_params=_params("parallel"),
        name="fox_lf_cumsum",
    )(fl, b_pad)


def _lf_kernel(fl_ref, b_ref, lf_ref):
    lf_ref[...] = -_softplus(-(fl_ref[...] + b_ref[...]))


def lf_only(fl, b_pad):
    m, w = fl.shape
    return pl.pallas_call(
        _lf_kernel,
        grid=(1,),
        in_specs=[pl.BlockSpec((m, w), lambda i: (0, 0)), pl.BlockSpec((1, w), lambda i: (0, 0))],
        out_specs=pl.BlockSpec((m, w), lambda i: (0, 0)),
        out_shape=jax.ShapeDtypeStruct((m, w), F32),
        name="fox_lf",
    )(fl, b_pad)


def _fox_flash_kernel(q_ref, k_ref, v_ref, cq_ref, ck_ref, o_ref, q2_ref, fq_ref, m_ref, l_ref, acc_ref, *, tq, tk):
    hp = pl.program_id(1)
    qi = pl.program_id(2)
    ki = pl.program_id(3)
    lane = lax.broadcasted_iota(jnp.int32, (tq, LANES), 1)
    lo = lane < (LANES // 2)
    last_k = (qi * tq + tq - 1) // tk

    @pl.when(ki == 0)
    def _():
        m_ref[...] = jnp.full_like(m_ref, -jnp.inf)
        l_ref[...] = jnp.zeros_like(l_ref)
        acc_ref[...] = jnp.zeros_like(acc_ref)
        q = q_ref[...].astype(F32)
        cq = cq_ref[...]
        for hh in range(2):
            q2_ref[hh] = jnp.where(lo if hh == 0 else jnp.logical_not(lo), q, 0.0).astype(BF16)
            fq_ref[hh] = jnp.sum(jnp.where(lane == 2 * hp + hh, cq, 0.0), axis=1, keepdims=True)

    @pl.when(ki <= last_k)
    def _():
        kb = k_ref[...].astype(BF16)
        vb = v_ref[...].astype(BF16)
        rowpos = qi * tq + lax.broadcasted_iota(jnp.int32, (tq, tk), 0)
        colpos = ki * tk + lax.broadcasted_iota(jnp.int32, (tq, tk), 1)
        keep = rowpos >= colpos
        alphas, pvs = [], []
        for hh in range(2):
            s = lax.dot_general(q2_ref[hh], kb, _NT, preferred_element_type=F32)
            s = s + fq_ref[hh] - ck_ref[hh:hh + 1, :]
            s = jnp.where(keep, s, NEG_INF)
            m_prev = m_ref[hh]
            m_new = jnp.maximum(m_prev, jnp.max(s, axis=1, keepdims=True))
            alpha = jnp.exp(m_prev - m_new)
            p = jnp.exp(s - m_new)
            l_ref[hh] = alpha * l_ref[hh] + jnp.sum(p, axis=1, keepdims=True)
            m_ref[hh] = m_new
            pvs.append(jnp.dot(p.astype(BF16), vb, preferred_element_type=F32))
            alphas.append(alpha)
        acc_ref[...] = jnp.where(lo, alphas[0], alphas[1]) * acc_ref[...] + jnp.where(lo, pvs[0], pvs[1])

    @pl.when(ki == last_k)
    def _():
        linv = jnp.where(lo, 1.0 / l_ref[0], 1.0 / l_ref[1])
        o_ref[...] = (acc_ref[...] * linv).astype(o_ref.dtype)


def fox_flash(q, k, v, cum, cum_rows):
    bsz, t, d = q.shape
    tq = _tile(t, 512)
    tk = tq
    n_pairs = d // LANES
    last_k = lambda qi: (qi * tq + tq - 1) // tk
    kv_spec = pl.BlockSpec((None, tk, LANES), lambda b, hp, qi, ki: (b, jnp.minimum(ki, last_k(qi)), hp))
    return pl.pallas_call(
        functools.partial(_fox_flash_kernel, tq=tq, tk=tk),
        grid=(bsz, n_pairs, t // tq, t // tk),
        in_specs=[
            pl.BlockSpec((None, tq, LANES), lambda b, hp, qi, ki: (b, qi, hp)),
            kv_spec, kv_spec,
            pl.BlockSpec((None, tq, LANES), lambda b, hp, qi, ki: (b, qi, 0)),
            pl.BlockSpec((None, None, 2, tk), lambda b, hp, qi, ki: (b, hp, 0, jnp.minimum(ki, last_k(qi)))),
        ],
        out_specs=pl.BlockSpec((None, tq, LANES), lambda b, hp, qi, ki: (b, qi, hp)),
        out_shape=jax.ShapeDtypeStruct((bsz, t, d), BF16),
        scratch_shapes=[
            pltpu.VMEM((2, tq, LANES), BF16),
            pltpu.VMEM((2, tq, 1), F32),
            pltpu.VMEM((2, tq, 1), F32),
            pltpu.VMEM((2, tq, 1), F32),
            pltpu.VMEM((tq, LANES), F32),
        ],
        compiler_params=_params("parallel", "parallel", "parallel", "arbitrary"),
        name="fox_flash",
    )(q, k, v, cum, cum_rows)


def _fox_decode_kernel(pt_ref, q_ref, kn_ref, vn_ref, lfn_ref, k_ref, v_ref, lft_ref, o_ref,
                       qm_ref, m_ref, l_ref, acc_ref, carry_ref):
    j = pl.program_id(1)
    d = q_ref.shape[1]
    nh = FOX_HEADS
    dh = d // nh
    page = k_ref.shape[0]
    head = lax.broadcasted_iota(jnp.int32, (nh, d), 0)
    own = head == lax.broadcasted_iota(jnp.int32, (nh, d), 1) // dh

    @pl.when(j == 0)
    def _():
        qm_ref[...] = jnp.where(own, jnp.broadcast_to(q_ref[...], (nh, d)), 0.0).astype(BF16)
        m_ref[...] = jnp.full_like(m_ref, -jnp.inf)
        l_ref[...] = jnp.zeros_like(l_ref)
        acc_ref[...] = jnp.zeros_like(acc_ref)
        carry_ref[...] = jnp.zeros_like(carry_ref)

    hl = lax.broadcasted_iota(jnp.int32, (nh, LANES), 0) == lax.broadcasted_iota(jnp.int32, (nh, LANES), 1)
    lf_new = jnp.sum(jnp.where(hl, jnp.broadcast_to(lfn_ref[...], (nh, LANES)), 0.0), axis=1, keepdims=True)

    s = lax.dot_general(qm_ref[...], k_ref[...].astype(BF16), _NT, preferred_element_type=F32)
    lft = lft_ref[...]
    later = (lax.broadcasted_iota(jnp.int32, (page, page), 0) > lax.broadcasted_iota(jnp.int32, (page, page), 1))
    after = jnp.dot(lft, later.astype(F32), preferred_element_type=F32, precision=lax.Precision.HIGHEST)
    s = s + lf_new + (after + carry_ref[...])
    m_prev = m_ref[...]
    m_new = jnp.maximum(m_prev, jnp.max(s, axis=1, keepdims=True))
    alpha = jnp.exp(m_prev - m_new)
    p = jnp.exp(s - m_new)
    l_ref[...] = alpha * l_ref[...] + jnp.sum(p, axis=1, keepdims=True)
    m_ref[...] = m_new
    acc_ref[...] = alpha * acc_ref[...] + jnp.dot(p.astype(BF16), v_ref[...].astype(BF16), preferred_element_type=F32)
    carry_ref[...] = carry_ref[...] + jnp.sum(lft, axis=1, keepdims=True)

    @pl.when(j == pl.num_programs(1) - 1)
    def _():
        kn = jnp.broadcast_to(kn_ref[...].astype(BF16).astype(F32), (nh, d))
        vn = jnp.broadcast_to(vn_ref[...].astype(BF16).astype(F32), (nh, d))
        s_new = jnp.sum(qm_ref[...].astype(F32) * kn, axis=1, keepdims=True)
        m_prev = m_ref[...]
        m_fin = jnp.maximum(m_prev, s_new)
        alpha = jnp.exp(m_prev - m_fin)
        p_new = jnp.exp(s_new - m_fin)
        l_fin = alpha * l_ref[...] + p_new
        acc = alpha * acc_ref[...] + p_new.astype(BF16).astype(F32) * vn
        o_ref[...] = jnp.sum(jnp.where(own, acc * (1.0 / l_fin), 0.0), axis=0, keepdims=True)


def fox_decode(q, k_new, v_new, lf_new, k_pool, v_pool, lft_pool, page_table, layer):
    bsz, d = q.shape
    n_pages = page_table.shape[1]
    page = k_pool.shape[2]
    nh = FOX_HEADS
    row = pl.BlockSpec((None, 1, d), lambda b, j, pt: (b, 0, 0))
    page_idx = lambda b, j, pt: (layer, pt[b * n_pages + (n_pages - 1 - j)], 0, 0)
    grid_spec = pltpu.PrefetchScalarGridSpec(
        num_scalar_prefetch=1,
        grid=(bsz, n_pages),
        in_specs=[
            row, row, row,
            pl.BlockSpec((None, 1, LANES), lambda b, j, pt: (b, 0, 0)),
            pl.BlockSpec((None, None, page, d), page_idx),
            pl.BlockSpec((None, None, page, d), page_idx),
            pl.BlockSpec((None, None, nh, page), page_idx),
        ],
        out_specs=row,
        scratch_shapes=[
            pltpu.VMEM((nh, d), BF16),
            pltpu.VMEM((nh, 1), F32),
            pltpu.VMEM((nh, 1), F32),
            pltpu.VMEM((nh, d), F32),
            pltpu.VMEM((nh, 1), F32),
        ],
    )
    out = pl.pallas_call(
        _fox_decode_kernel,
        grid_spec=grid_spec,
        out_shape=jax.ShapeDtypeStruct((bsz, 1, d), F32),
        compiler_params=_params("parallel", "arbitrary"),
        name="fox_decode",
    )(page_table.reshape(-1), q.reshape(bsz, 1, d), k_new.reshape(bsz, 1, d), v_new.reshape(bsz, 1, d),
      lf_new.reshape(bsz, 1, LANES), k_pool, v_pool, lft_pool)
    return out.reshape(bsz, d)


def _block_diag_gate_weights(w_rg, w_ig):
    nb, bw, _ = w_rg.shape
    half = nb // 2
    assert (half * bw) % LANES == 0
    eye = jnp.eye(half, dtype=w_rg.dtype)

    def dense(w):
        return (w[:, :, None, :] * eye[:, None, :, None]).reshape(half * bw, half * bw)

    return jnp.stack([
        jnp.concatenate([dense(w_rg[c * half:(c + 1) * half]), dense(w_ig[c * half:(c + 1) * half])], axis=1)
        for c in range(2)]).astype(BF16)


def _trunk(x, bsz, t, pos0, mem_k, mem_v, lru_h, lru_conv, pool_buf, fox_paged, p):
    d = x.shape[1]
    depth = p["norm_mix_g"].shape[0]
    hs, convs, pools, ks, vs, lfs = [], [], [], [], [], []
    y = None
    for layer in range(depth):
        kind, j = layer % N_MIXERS, layer // N_MIXERS
        g_mix = p["norm_mix_g"][layer]
        if kind == 0:
            c = p["w_lru_out"].shape[1]
            gu = norm_matmul(x, g_mix, p["w_lru_in"][j], tn=c, name="lru_in")
            args = (p["lru_conv_w"][j], p["lru_conv_b"][j], p["lru_wg"][j], p["lru_b_rg"][j], p["lru_b_ig"][j],
                    p["lru_lambda"][j])
            if t > 1:
                yl, hl, cb = lru_prompt(gu.reshape(bsz, t, 2 * c), lru_conv[j], lru_h[j], *args, pos0=pos0)
                yl = yl.reshape(bsz * t, c)
            else:
                yl, hl, cb = lru_step(gu, lru_conv[j], lru_h[j], *args, pos0=pos0)
            hs.append(hl)
            convs.append(cb)
            x = matmul_res(yl, p["w_lru_out"][j], x, name="lru_out")
        elif kind == 1:
            dh = d // FOX_HEADS
            q, k, v, fl = fox_proj(x, g_mix, p["w_fox_qkv"][j], p["w_fox_f"][j],
                                   q_dtype=BF16 if fox_paged is None else F32, q_scale=dh ** -0.5)
            if fox_paged is None:
                lf, cum, cum_t = lf_cumsum(fl.reshape(bsz, t, LANES), p["b_fox_f"][j])
                cum_rows = cum_t[:, :FOX_HEADS, :].reshape(bsz, FOX_HEADS // 2, 2, t)
                o = fox_flash(q.reshape(bsz, t, d), k.reshape(bsz, t, d), v.reshape(bsz, t, d), cum, cum_rows)
                o = o.reshape(bsz * t, d)
                lf = lf[:, :, :FOX_HEADS]
            else:
                k_pool, v_pool, lft_pool, page_table = fox_paged
                lf = lf_only(fl, p["b_fox_f"][j])
                o = fox_decode(q, k, v, lf, k_pool, v_pool, lft_pool, page_table, j)
                lf = lf[:, :FOX_HEADS].reshape(bsz, t, FOX_HEADS)
            ks.append(k.reshape(bsz, t, FOX_HEADS, dh))
            vs.append(v.reshape(bsz, t, FOX_HEADS, dh))
            lfs.append(lf)
            x = matmul_res(o, p["w_fox_o"][j], x, name="fox_out")
        else:
            args = (g_mix, p["w_pool"][j], p["b_pool"][j], p["pool_scale"][j])
            if t > 1:
                x3, pb = pool_prompt(x.reshape(bsz, t, d), pool_buf[j], *args, pos0=pos0)
                x = x3.reshape(bsz * t, d)
            else:
                x, pb = pool_step(x, pool_buf[j], *args, pos0=pos0)
            pools.append(pb)

        dx = d // XA_HEADS
        q = norm_matmul(x, p["norm_x_g"][layer], p["w_xq"][layer], tn=d, out_dtype=BF16, scale=dx ** -0.5,
                        name="xattn_q")
        if t > 1:
            o = mem_attn(q.reshape(bsz, t, d), mem_k, mem_v, layer).reshape(bsz * t, d)
        else:
            rows = 2 * SUBLANES
            o = mem_attn(jnp.broadcast_to(q[:, None, :], (bsz, rows, d)), mem_k, mem_v, layer)[:, 0, :]
        x = matmul_res(o, p["w_xo"][layer], x, name="xattn_out")

        final_g = p["final_norm_g"] if layer == depth - 1 else None
        x, y = mlp(x, p["norm_mlp_g"][layer], p["w_up"][layer], p["w_down"][layer], final_g)
    return y, hs, convs, pools, ks, vs, lfs


def kernel(x_prompt, x_sample, mem_prompt, cache_fox_k, cache_fox_v, cache_fox_lf, cache_mem_k, cache_mem_v, state_lru_h, state_lru_conv, state_pool, page_table, norm_mix_g, norm_mem_g, norm_x_g, norm_mlp_g, final_norm_g, w_lru_in, lru_conv_w, lru_conv_b, lru_w_rg, lru_b_rg, lru_w_ig, lru_b_ig, lru_lambda, w_lru_out, w_fox_qkvf, b_fox_f, w_fox_o, w_pool, b_pool, pool_scale, w_xq, w_xkv, w_xo, w_up, w_down):
    bsz, seq, d = x_prompt.shape
    dec, dec_seq, _ = x_sample.shape
    assert dec_seq == 1
    depth = norm_mix_g.shape[0]
    n_mem = mem_prompt.shape[1]
    n_fox = w_fox_qkvf.shape[0]
    n_lru = w_lru_in.shape[0]
    n_pool_layers = w_pool.shape[0]
    c = w_lru_out.shape[1]
    dt = x_prompt.dtype

    bias_pad = jnp.pad(b_fox_f, ((0, 0), (0, LANES - FOX_HEADS))).reshape(n_fox, 1, LANES)
    p = dict(
        norm_mix_g=norm_mix_g, norm_x_g=norm_x_g, norm_mlp_g=norm_mlp_g, final_norm_g=final_norm_g,
        w_lru_in=w_lru_in.astype(BF16), lru_conv_w=lru_conv_w, lru_conv_b=lru_conv_b,
        lru_wg=jnp.stack([_block_diag_gate_weights(lru_w_rg[l], lru_w_ig[l]) for l in range(n_lru)]),
        lru_b_rg=lru_b_rg, lru_b_ig=lru_b_ig, lru_lambda=lru_lambda, w_lru_out=w_lru_out.astype(BF16),
        w_fox_qkv=w_fox_qkvf[:, :, :3 * d].astype(BF16),
        w_fox_f=jnp.pad(w_fox_qkvf[:, :, 3 * d:], ((0, 0), (0, 0), (0, LANES - FOX_HEADS))).astype(BF16),
        b_fox_f=bias_pad, w_fox_o=w_fox_o.astype(BF16),
        w_pool=w_pool.astype(BF16), b_pool=b_pool, pool_scale=pool_scale,
        w_xq=w_xq.astype(BF16), w_xo=w_xo.astype(BF16), w_up=w_up.astype(BF16), w_down=w_down.astype(BF16),
    )

    mem_k_p, mem_v_p = mem_kv(mem_prompt.reshape(bsz * n_mem, d), norm_mem_g, w_xkv.astype(BF16))
    mem_k_p = mem_k_p.reshape(depth, bsz, n_mem, d)
    mem_v_p = mem_v_p.reshape(depth, bsz, n_mem, d)
    h0 = jnp.zeros((n_lru, bsz, c), dt)
    c0 = jnp.zeros((n_lru, bsz, CONV_WIDTH - 1, c), dt)
    pb0 = jnp.zeros((n_pool_layers, bsz, max(POOL_WINDOWS) - 1, d), dt)
    y_p, hs_p, convs_p, pools_p, ks_p, vs_p, lfs_p = _trunk(
        x_prompt.reshape(bsz * seq, d), bsz, seq, 0, mem_k_p, mem_v_p, h0, c0, pb0, None, p)

    n_pool_pages, page = cache_fox_k.shape[1], cache_fox_k.shape[2]
    pos_s = page_table.shape[1] * page
    fox_paged = (cache_fox_k.reshape(n_fox, n_pool_pages, page, d), cache_fox_v.reshape(n_fox, n_pool_pages, page, d),
                 jnp.swapaxes(cache_fox_lf, 2, 3), page_table)
    y_s, hs_s, convs_s, pools_s, ks_s, vs_s, lfs_s = _trunk(
        x_sample.reshape(dec, d), dec, 1, pos_s,
        cache_mem_k.reshape(depth, dec, n_mem, d), cache_mem_v.reshape(depth, dec, n_mem, d),
        state_lru_h, state_lru_conv, state_pool, fox_paged, p)

    xa = (depth, bsz, n_mem, XA_HEADS, d // XA_HEADS)
    return (y_p.reshape(bsz, seq, d), y_s.reshape(dec, 1, d),
            jnp.stack(hs_p), jnp.stack(convs_p), jnp.stack(pools_p), jnp.stack(ks_p), jnp.stack(vs_p),
            jnp.stack(lfs_p), mem_k_p.reshape(xa), mem_v_p.reshape(xa),
            jnp.stack(hs_s), jnp.stack(convs_s), jnp.stack(pools_s), jnp.stack(ks_s), jnp.stack(vs_s),
            jnp.stack(lfs_s))
```

```python
import functools

import jax
import jax.numpy as jnp
from jax import lax
from jax.experimental import pallas as pl
from jax.experimental.pallas import tpu as pltpu

F32 = jnp.float32
BF16 = jnp.bfloat16

RMS_EPS = 1e-6
NEG_INF = -1e30
LRU_C = 8.0
CONV_WIDTH = 4
FOX_HEADS = 16
XA_HEADS = 4
POOL_WINDOWS = (2, 4, 8, 16)
N_MIXERS = 3

LANES = 128
SUBLANES = 8
HALO = 16

_NT = (((1,), (1,)), ((), ()))


def _tile(n, target):
    t = 1
    while t * 2 <= min(n, target):
        t *= 2
    while t > 1 and n % t:
        t //= 2
    return t if (n % t == 0 and t >= SUBLANES) else n


def _params(*sem):
    return pltpu.CompilerParams(dimension_semantics=sem)


def _rmsnorm(x, g):
    x = x.astype(F32)
    x = x * lax.rsqrt(jnp.mean(x * x, axis=-1, keepdims=True) + RMS_EPS)
    return x * g


def _softplus(z):
    return jnp.maximum(z, 0.0) + jnp.log1p(jnp.exp(-jnp.abs(z)))


def _gelu_tanh(x):
    c = 0.7978845608028654
    return x * (0.5 * (1.0 + jnp.tanh(c * (x + 0.044715 * (x * x * x)))))


def _norm_matmul_kernel(x_ref, g_ref, w_ref, o_ref, xn_ref, *, scale):
    @pl.when(pl.program_id(1) == 0)
    def _():
        xn_ref[...] = _rmsnorm(x_ref[...], g_ref[...]).astype(BF16)

    acc = jnp.dot(xn_ref[...], w_ref[...], preferred_element_type=F32)
    if scale != 1.0:
        acc = acc * scale
    o_ref[...] = acc.astype(o_ref.dtype)


def norm_matmul(x, g, w, *, tn, out_dtype=F32, scale=1.0, name="norm_matmul"):
    m, d = x.shape
    n = w.shape[1]
    tm = _tile(m, 1024)
    return pl.pallas_call(
        functools.partial(_norm_matmul_kernel, scale=scale),
        grid=(m // tm, n // tn),
        in_specs=[
            pl.BlockSpec((tm, d), lambda i, j: (i, 0)),
            pl.BlockSpec((1, d), lambda i, j: (0, 0)),
            pl.BlockSpec((d, tn), lambda i, j: (0, j)),
        ],
        out_specs=pl.BlockSpec((tm, tn), lambda i, j: (i, j)),
        out_shape=jax.ShapeDtypeStruct((m, n), out_dtype),
        scratch_shapes=[pltpu.VMEM((tm, d), BF16)],
        compiler_params=_params("parallel", "arbitrary"),
        name=name,
    )(x, g.reshape(1, d), w)


def _fox_proj_kernel(x_ref, g_ref, w_ref, wf_ref, q_ref, k_ref, v_ref, f_ref, xn_ref, *, q_scale):
    j = pl.program_id(1)

    @pl.when(j == 0)
    def _():
        xn_ref[...] = _rmsnorm(x_ref[...], g_ref[...]).astype(BF16)

    @pl.when(j == 0)
    def _():
        acc = jnp.dot(xn_ref[...], w_ref[...], preferred_element_type=F32)
        q_ref[...] = (acc * q_scale).astype(q_ref.dtype)

    @pl.when(j == 1)
    def _():
        k_ref[...] = jnp.dot(xn_ref[...], w_ref[...], preferred_element_type=F32)

    @pl.when(j == 2)
    def _():
        v_ref[...] = jnp.dot(xn_ref[...], w_ref[...], preferred_element_type=F32)

    @pl.when(j == 3)
    def _():
        f_ref[...] = jnp.dot(xn_ref[...], wf_ref[...], preferred_element_type=F32)


def fox_proj(x, g, w_qkv, w_f, *, q_dtype, q_scale):
    m, d = x.shape
    tm = _tile(m, 512)
    row = lambda i, j: (i, 0)
    return pl.pallas_call(
        functools.partial(_fox_proj_kernel, q_scale=q_scale),
        grid=(m // tm, 4),
        in_specs=[
            pl.BlockSpec((tm, d), row),
            pl.BlockSpec((1, d), lambda i, j: (0, 0)),
            pl.BlockSpec((d, d), lambda i, j: (0, jnp.minimum(j, 2))),
            pl.BlockSpec((d, LANES), lambda i, j: (0, 0)),
        ],
        out_specs=[
            pl.BlockSpec((tm, d), row),
            pl.BlockSpec((tm, d), row),
            pl.BlockSpec((tm, d), row),
            pl.BlockSpec((tm, LANES), row),
        ],
        out_shape=[
            jax.ShapeDtypeStruct((m, d), q_dtype),
            jax.ShapeDtypeStruct((m, d), F32),
            jax.ShapeDtypeStruct((m, d), F32),
            jax.ShapeDtypeStruct((m, LANES), F32),
        ],
        scratch_shapes=[pltpu.VMEM((tm, d), BF16)],
        compiler_params=_params("parallel", "arbitrary"),
        name="fox_proj",
    )(x, g.reshape(1, d), w_qkv, w_f)


def _mem_kv_kernel(x_ref, g_ref, w_ref, k_ref, v_ref, xn_ref):
    j = pl.program_id(2)

    @pl.when(j == 0)
    def _():
        xn_ref[...] = _rmsnorm(x_ref[...], g_ref[...]).astype(BF16)
        k_ref[...] = jnp.dot(xn_ref[...], w_ref[...], preferred_element_type=F32)

    @pl.when(j == 1)
    def _():
        v_ref[...] = jnp.dot(xn_ref[...], w_ref[...], preferred_element_type=F32)


def mem_kv(mem, g_all, w_all):
    m, d = mem.shape
    depth = g_all.shape[0]
    tm = _tile(m, 512)
    out_spec = pl.BlockSpec((None, tm, d), lambda l, i, j: (l, i, 0))
    return pl.pallas_call(
        _mem_kv_kernel,
        grid=(depth, m // tm, 2),
        in_specs=[
            pl.BlockSpec((tm, d), lambda l, i, j: (i, 0)),
            pl.BlockSpec((None, 1, d), lambda l, i, j: (l, 0, 0)),
            pl.BlockSpec((None, d, d), lambda l, i, j: (l, 0, j)),
        ],
        out_specs=[out_spec, out_spec],
        out_shape=[jax.ShapeDtypeStruct((depth, m, d), F32)] * 2,
        scratch_shapes=[pltpu.VMEM((tm, d), BF16)],
        compiler_params=_params("parallel", "parallel", "arbitrary"),
        name="mem_kv",
    )(mem, g_all.reshape(depth, 1, d), w_all)


def _matmul_res_kernel(h_ref, w_ref, r_ref, o_ref):
    o_ref[...] = r_ref[...] + jnp.dot(h_ref[...].astype(BF16), w_ref[...], preferred_element_type=F32)


def matmul_res(h, w, res, *, name="matmul_res"):
    m, k = h.shape
    n = w.shape[1]
    tm = _tile(m, 512)
    return pl.pallas_call(
        _matmul_res_kernel,
        grid=(m // tm,),
        in_specs=[
            pl.BlockSpec((tm, k), lambda i: (i, 0)),
            pl.BlockSpec((k, n), lambda i: (0, 0)),
            pl.BlockSpec((tm, n), lambda i: (i, 0)),
        ],
        out_specs=pl.BlockSpec((tm, n), lambda i: (i, 0)),
        out_shape=jax.ShapeDtypeStruct((m, n), F32),
        compiler_params=_params("parallel"),
        name=name,
    )(h, w, res)


def _mlp_kernel(*refs, final_norm):
    if final_norm:
        x_ref, g_ref, wu_ref, wd_ref, gf_ref, o_ref, y_ref, xn_ref, acc_ref = refs
    else:
        x_ref, g_ref, wu_ref, wd_ref, o_ref, xn_ref, acc_ref = refs
    j = pl.program_id(1)

    @pl.when(j == 0)
    def _():
        xn_ref[...] = _rmsnorm(x_ref[...], g_ref[...]).astype(BF16)
        acc_ref[...] = jnp.zeros_like(acc_ref)

    h = jnp.dot(xn_ref[...], wu_ref[...], preferred_element_type=F32)
    h = jnp.square(jnp.maximum(h, 0.0)).astype(BF16)
    acc_ref[...] += jnp.dot(h, wd_ref[...], preferred_element_type=F32)

    @pl.when(j == pl.num_programs(1) - 1)
    def _():
        out = x_ref[...] + acc_ref[...]
        o_ref[...] = out
        if final_norm:
            y_ref[...] = _rmsnorm(out, gf_ref[...])


def mlp(x, g, w_up, w_down, final_g=None):
    m, d = x.shape
    f = w_up.shape[1]
    tm = _tile(m, 1024)
    tf = _tile(f, 512)
    row = lambda i, j: (i, 0)
    vec = pl.BlockSpec((1, d), lambda i, j: (0, 0))
    in_specs = [
        pl.BlockSpec((tm, d), row),
        vec,
        pl.BlockSpec((d, tf), lambda i, j: (0, j)),
        pl.BlockSpec((tf, d), lambda i, j: (j, 0)),
    ]
    args = [x, g.reshape(1, d), w_up, w_down]
    out_specs = [pl.BlockSpec((tm, d), row)]
    out_shape = [jax.ShapeDtypeStruct((m, d), F32)]
    if final_g is not None:
        in_specs.append(vec)
        args.append(final_g.reshape(1, d))
        out_specs.append(pl.BlockSpec((tm, d), row))
        out_shape.append(jax.ShapeDtypeStruct((m, d), F32))
    outs = pl.pallas_call(
        functools.partial(_mlp_kernel, final_norm=final_g is not None),
        grid=(m // tm, f // tf),
        in_specs=in_specs,
        out_specs=out_specs,
        out_shape=out_shape,
        scratch_shapes=[pltpu.VMEM((tm, d), BF16), pltpu.VMEM((tm, d), F32)],
        compiler_params=_params("parallel", "arbitrary"),
        name="mlp",
    )(*args)
    return outs if final_g is not None else (outs[0], None)


def _mem_attn_kernel(q_ref, k_ref, v_ref, o_ref, *, head_major):
    d = q_ref.shape[1]
    dh = d // XA_HEADS
    for h in range(XA_HEADS):
        sl = slice(h * dh, (h + 1) * dh)
        kh = k_ref[h] if head_major else k_ref[:, sl]
        vh = v_ref[h] if head_major else v_ref[:, sl]
        s = lax.dot_general(q_ref[:, sl], kh.astype(BF16), _NT, preferred_element_type=F32)
        e = jnp.exp(s - jnp.max(s, axis=1, keepdims=True))
        p = e * (1.0 / jnp.sum(e, axis=1, keepdims=True))
        o = jnp.dot(p.astype(BF16), vh.astype(BF16), preferred_element_type=F32)
        o_ref[:, sl] = o.astype(o_ref.dtype)


def mem_attn(q, k_all, v_all, layer):
    bsz, t, d = q.shape
    head_major = k_all.ndim == 5
    tm = _tile(t, 512)
    if head_major:
        kv_spec = pl.BlockSpec((None, None) + k_all.shape[2:], lambda b, i: (layer, b, 0, 0, 0))
    else:
        kv_spec = pl.BlockSpec((None, None) + k_all.shape[2:], lambda b, i: (layer, b, 0, 0))
    return pl.pallas_call(
        functools.partial(_mem_attn_kernel, head_major=head_major),
        grid=(bsz, t // tm),
        in_specs=[pl.BlockSpec((None, tm, d), lambda b, i: (b, i, 0)), kv_spec, kv_spec],
        out_specs=pl.BlockSpec((None, tm, d), lambda b, i: (b, i, 0)),
        out_shape=jax.ShapeDtypeStruct((bsz, t, d), BF16),
        compiler_params=_params("parallel", "parallel"),
        name="mem_attn",
    )(q, k_all, v_all)


def _lru_gates(uc, wg_ref, brg, big, lam):
    half = uc.shape[1] // 2
    ucb = uc.astype(BF16)
    g0 = jnp.dot(ucb[:, :half], wg_ref[0], preferred_element_type=F32)
    g1 = jnp.dot(ucb[:, half:], wg_ref[1], preferred_element_type=F32)
    rg = jnp.concatenate([g0[:, :half], g1[:, :half]], axis=1) + brg
    ig = jnp.concatenate([g0[:, half:], g1[:, half:]], axis=1) + big
    r = jax.nn.sigmoid(rg)
    i = jax.nn.sigmoid(ig)
    log_a = (-LRU_C * r) * _softplus(-lam)
    a = jnp.exp(log_a)
    mult = jnp.sqrt(-jnp.tanh(log_a) * (a * a + 1.0))
    return a, mult, i


def _scan8(a8, x8, row8):
    for s in (1, 2, 4):
        keep = row8 >= s
        xs = jnp.where(keep, pltpu.roll(x8, s, 0), 0.0)
        a_s = jnp.where(keep, pltpu.roll(a8, s, 0), 1.0)
        x8 = x8 + a8 * xs
        a8 = a8 * a_s
    return a8, x8


def _lru_prompt_kernel(gate_ref, u_ref, buf_ref, h0_ref, cw_ref, cb_ref, wg_ref, brg_ref, big_ref, lam_ref,
                       y_ref, hl_ref, cbuf_ref, ufull, a_scr, x_scr, hc, *, first_at_zero):
    i = pl.program_id(1)
    tt, c = u_ref.shape
    pad = SUBLANES

    @pl.when(i == 0)
    def _():
        ufull[0:pad, :] = buf_ref[...]
        hc[...] = h0_ref[...]

    @pl.when(i > 0)
    def _():
        ufull[0:pad, :] = ufull[tt:tt + pad, :]

    u = u_ref[...]
    ufull[pad:pad + tt, :] = u
    uc = cb_ref[...]
    for k in range(CONV_WIDTH - 1):
        off = pad - (CONV_WIDTH - 1) + k
        uc = uc + ufull[off:off + tt, :] * cw_ref[k:k + 1, :]
    uc = uc + u * cw_ref[CONV_WIDTH - 1:CONV_WIDTH, :]

    a, mult, ig = _lru_gates(uc, wg_ref, brg_ref[...], big_ref[...], lam_ref[...])
    if first_at_zero:
        row = lax.broadcasted_iota(jnp.int32, (tt, 1), 0)
        mult = jnp.where(jnp.logical_and(row == 0, i == 0), 1.0, mult)
    x_scr[...] = (mult * ig) * uc
    a_scr[...] = a

    row8 = lax.broadcasted_iota(jnp.int32, (SUBLANES, c), 0)

    def body(r, h):
        off = pl.multiple_of(r * SUBLANES, SUBLANES)
        a8, x8 = _scan8(a_scr[pl.ds(off, SUBLANES), :], x_scr[pl.ds(off, SUBLANES), :], row8)
        h8 = x8 + a8 * h
        x_scr[pl.ds(off, SUBLANES), :] = h8
        return h8[SUBLANES - 1:SUBLANES, :]

    h = lax.fori_loop(0, tt // SUBLANES, body, hc[...])
    hc[...] = h
    y_ref[...] = (_gelu_tanh(gate_ref[...]) * x_scr[...]).astype(y_ref.dtype)

    @pl.when(i == pl.num_programs(1) - 1)
    def _():
        hl_ref[...] = h
        cbuf_ref[...] = ufull[tt:tt + pad, :]


def lru_prompt(gu, conv_buf, h0, conv_w, conv_b, wg, b_rg, b_ig, lam, *, pos0):
    bsz, t, c2 = gu.shape
    c = c2 // 2
    tt = _tile(t, 256)
    pad = SUBLANES
    buf8 = jnp.pad(conv_buf, ((0, 0), (pad - (CONV_WIDTH - 1), 0), (0, 0)))
    vec = pl.BlockSpec((1, c), lambda b, i: (0, 0))
    y, h_last, cbuf = pl.pallas_call(
        functools.partial(_lru_prompt_kernel, first_at_zero=(pos0 == 0)),
        grid=(bsz, t // tt),
        in_specs=[
            pl.BlockSpec((None, tt, c), lambda b, i: (b, i, 0)),
            pl.BlockSpec((None, tt, c), lambda b, i: (b, i, 1)),
            pl.BlockSpec((None, pad, c), lambda b, i: (b, 0, 0)),
            pl.BlockSpec((None, 1, c), lambda b, i: (b, 0, 0)),
            pl.BlockSpec((CONV_WIDTH, c), lambda b, i: (0, 0)),
            vec,
            pl.BlockSpec((2, c // 2, c), lambda b, i: (0, 0, 0)),
            vec, vec, vec,
        ],
        out_specs=[
            pl.BlockSpec((None, tt, c), lambda b, i: (b, i, 0)),
            pl.BlockSpec((None, 1, c), lambda b, i: (b, 0, 0)),
            pl.BlockSpec((None, pad, c), lambda b, i: (b, 0, 0)),
        ],
        out_shape=[
            jax.ShapeDtypeStruct((bsz, t, c), BF16),
            jax.ShapeDtypeStruct((bsz, 1, c), F32),
            jax.ShapeDtypeStruct((bsz, pad, c), F32),
        ],
        scratch_shapes=[
            pltpu.VMEM((tt + pad, c), F32),
            pltpu.VMEM((tt, c), F32),
            pltpu.VMEM((tt, c), F32),
            pltpu.VMEM((1, c), F32),
        ],
        compiler_params=_params("parallel", "arbitrary"),
        name="lru_prompt",
    )(gu, gu, buf8, h0.reshape(bsz, 1, c), conv_w, conv_b.reshape(1, c), wg,
      b_rg.reshape(1, c), b_ig.reshape(1, c), lam.reshape(1, c))
    return y, h_last.reshape(bsz, c), cbuf[:, pad - (CONV_WIDTH - 1):, :]


def _lru_step_kernel(gate_ref, u_ref, buf_ref, h0_ref, cw_ref, cb_ref, wg_ref, brg_ref, big_ref, lam_ref,
                     y_ref, h_ref, *, first_at_zero):
    u = u_ref[...]
    uc = cb_ref[...]
    for k in range(CONV_WIDTH - 1):
        uc = uc + buf_ref[k] * cw_ref[k:k + 1, :]
    uc = uc + u * cw_ref[CONV_WIDTH - 1:CONV_WIDTH, :]
    a, mult, ig = _lru_gates(uc, wg_ref, brg_ref[...], big_ref[...], lam_ref[...])
    if first_at_zero:
        mult = jnp.ones_like(mult)
    h = a * h0_ref[...] + (mult * ig) * uc
    h_ref[...] = h
    y_ref[...] = (_gelu_tanh(gate_ref[...]) * h).astype(y_ref.dtype)


def lru_step(gu, conv_buf, h0, conv_w, conv_b, wg, b_rg, b_ig, lam, *, pos0):
    bsz, c2 = gu.shape
    c = c2 // 2
    buf_t = jnp.swapaxes(conv_buf, 0, 1)
    full = lambda *shape: pl.BlockSpec(shape, lambda i: (0,) * len(shape))
    y, h = pl.pallas_call(
        functools.partial(_lru_step_kernel, first_at_zero=(pos0 == 0)),
        grid=(1,),
        in_specs=[
            pl.BlockSpec((bsz, c), lambda i: (0, 0)),
            pl.BlockSpec((bsz, c), lambda i: (0, 1)),
            full(CONV_WIDTH - 1, bsz, c),
            full(bsz, c),
            full(CONV_WIDTH, c),
            full(1, c),
            full(2, c // 2, c),
            full(1, c), full(1, c), full(1, c),
        ],
        out_specs=[full(bsz, c), full(bsz, c)],
        out_shape=[jax.ShapeDtypeStruct((bsz, c), BF16), jax.ShapeDtypeStruct((bsz, c), F32)],
        compiler_params=_params("arbitrary"),
        name="lru_step",
    )(gu, gu, buf_t, h0, conv_w, conv_b.reshape(1, c), wg,
      b_rg.reshape(1, c), b_ig.reshape(1, c), lam.reshape(1, c))
    new_buf = jnp.concatenate([conv_buf[:, 1:], gu[:, None, c:]], axis=1)
    return y, h, new_buf


def _pool_groups(xn, shifted, cnt, w_ref, b, scale):
    d = xn.shape[1]
    gw = d // len(POOL_WINDOWS)
    ys = []
    for gi, win in enumerate(POOL_WINDOWS):
        ch = slice(gi * gw, (gi + 1) * gw)
        s = xn[:, ch]
        for k in range(1, win):
            s = s + shifted(k, ch)
        dd = (s / cnt(win) - xn[:, ch]).astype(BF16)
        ys.append(jnp.dot(dd, w_ref[gi], preferred_element_type=F32))
    return (jnp.concatenate(ys, axis=1) + b) * scale


def _pool_prompt_kernel(x_ref, buf_ref, g_ref, w_ref, b_ref, sc_ref, o_ref, nb_ref, full, *, pos0):
    i = pl.program_id(1)
    tm, d = x_ref.shape

    @pl.when(i == 0)
    def _():
        full[0:HALO, :] = buf_ref[...]

    @pl.when(i > 0)
    def _():
        full[0:HALO, :] = full[tm:tm + HALO, :]

    x = x_ref[...]
    xn = _rmsnorm(x, g_ref[...])
    full[HALO:HALO + tm, :] = xn
    pos = pos0 + i * tm + lax.broadcasted_iota(jnp.int32, (tm, 1), 0)
    y = _pool_groups(
        xn,
        lambda k, ch: full[HALO - k:HALO - k + tm, ch],
        lambda win: jnp.minimum(pos + 1, win).astype(F32),
        w_ref, b_ref[...], sc_ref[...])
    o_ref[...] = x + y

    @pl.when(i == pl.num_programs(1) - 1)
    def _():
        nb_ref[...] = full[tm:tm + HALO, :]


def pool_prompt(x, buf, g, w, b, scale, *, pos0):
    bsz, t, d = x.shape
    nbuf = buf.shape[1]
    tm = _tile(t, 512)
    buf16 = jnp.pad(buf, ((0, 0), (HALO - nbuf, 0), (0, 0)))
    vec = pl.BlockSpec((1, d), lambda b_, i: (0, 0))
    ng = len(POOL_WINDOWS)
    out, nb = pl.pallas_call(
        functools.partial(_pool_prompt_kernel, pos0=pos0),
        grid=(bsz, t // tm),
        in_specs=[
            pl.BlockSpec((None, tm, d), lambda b_, i: (b_, i, 0)),
            pl.BlockSpec((None, HALO, d), lambda b_, i: (b_, 0, 0)),
            vec,
            pl.BlockSpec((ng, d // ng, d // ng), lambda b_, i: (0, 0, 0)),
            vec, vec,
        ],
        out_specs=[
            pl.BlockSpec((None, tm, d), lambda b_, i: (b_, i, 0)),
            pl.BlockSpec((None, HALO, d), lambda b_, i: (b_, 0, 0)),
        ],
        out_shape=[jax.ShapeDtypeStruct((bsz, t, d), F32), jax.ShapeDtypeStruct((bsz, HALO, d), F32)],
        scratch_shapes=[pltpu.VMEM((tm + HALO, d), F32)],
        compiler_params=_params("parallel", "arbitrary"),
        name="pool_prompt",
    )(x, buf16, g.reshape(1, d), w, b.reshape(1, d), scale.reshape(1, d))
    return out, nb[:, HALO - nbuf:, :]


def _pool_step_kernel(x_ref, buf_ref, g_ref, w_ref, b_ref, sc_ref, o_ref, xn_ref, *, pos0):
    x = x_ref[...]
    xn = _rmsnorm(x, g_ref[...])
    nbuf = buf_ref.shape[0]
    y = _pool_groups(
        xn,
        lambda k, ch: buf_ref[nbuf - k, :, ch],
        lambda win: float(min(pos0 + 1, win)),
        w_ref, b_ref[...], sc_ref[...])
    o_ref[...] = x + y
    xn_ref[...] = xn


def pool_step(x, buf, g, w, b, scale, *, pos0):
    bsz, d = x.shape
    nbuf = buf.shape[1]
    ng = len(POOL_WINDOWS)
    buf_t = jnp.swapaxes(buf, 0, 1)
    full = lambda *shape: pl.BlockSpec(shape, lambda i: (0,) * len(shape))
    out, xn = pl.pallas_call(
        functools.partial(_pool_step_kernel, pos0=pos0),
        grid=(1,),
        in_specs=[full(bsz, d), full(nbuf, bsz, d), full(1, d), full(ng, d // ng, d // ng), full(1, d), full(1, d)],
        out_specs=[full(bsz, d), full(bsz, d)],
        out_shape=[jax.ShapeDtypeStruct((bsz, d), F32)] * 2,
        compiler_params=_params("arbitrary"),
        name="pool_step",
    )(x, buf_t, g.reshape(1, d), w, b.reshape(1, d), scale.reshape(1, d))
    return out, jnp.concatenate([buf[:, 1:], xn[:, None, :]], axis=1)


def _lf_cumsum_kernel(fl_ref, b_ref, lf_ref, cum_ref, cumt_ref):
    t, w = fl_ref.shape
    lf_ref[...] = -_softplus(-(fl_ref[...] + b_ref[...]))
    row8 = lax.broadcasted_iota(jnp.int32, (SUBLANES, w), 0)

    def body(r, carry):
        off = pl.multiple_of(r * SUBLANES, SUBLANES)
        x8 = lf_ref[pl.ds(off, SUBLANES), :]
        for s in (1, 2, 4):
            x8 = x8 + jnp.where(row8 >= s, pltpu.roll(x8, s, 0), 0.0)
        c8 = x8 + carry
        cum_ref[pl.ds(off, SUBLANES), :] = c8
        return c8[SUBLANES - 1:SUBLANES, :]

    lax.fori_loop(0, t // SUBLANES, body, jnp.zeros((1, w), F32))
    cumt_ref[...] = cum_ref[...].T


def lf_cumsum(fl, b_pad):
    bsz, t, w = fl.shape
    blk = pl.BlockSpec((None, t, w), lambda b: (b, 0, 0))
    return pl.pallas_call(
        _lf_cumsum_kernel,
        grid=(bsz,),
        in_specs=[blk, pl.BlockSpec((1, w), lambda b: (0, 0))],
        out_specs=[blk, blk, pl.BlockSpec((None, w, t), lambda b: (b, 0, 0))],
        out_shape=[jax.ShapeDtypeStruct((bsz, t, w), F32)] * 2 + [jax.ShapeDtypeStruct((bsz, w, t), F32)],
        compiler_params=_params("parallel"),
        name="fox_lf_cumsum",
    )(fl, b_pad)


def _lf_kernel(fl_ref, b_ref, lf_ref):
    lf_ref[...] = -_softplus(-(fl_ref[...] + b_ref[...]))


def lf_only(fl, b_pad):
    m, w = fl.shape
    return pl.pallas_call(
        _lf_kernel,
        grid=(1,),
        in_specs=[pl.BlockSpec((m, w), lambda i: (0, 0)), pl.BlockSpec((1, w), lambda i: (0, 0))],
        out_specs=pl.BlockSpec((m, w), lambda i: (0, 0)),
        out_shape=jax.ShapeDtypeStruct((m, w), F32),
        name="fox_lf",
    )(fl, b_pad)


FLASH_HEADS = 4
FLASH_ROWS = 128


def _fox_flash_kernel(qi_tab, ki_tab, q_ref, k_ref, v_ref, cq_ref, ck_ref, o_ref,
                      q2_ref, kb_ref, vb_ref, fq_ref, m_ref, l_ref, acc_ref, *, tq, hb, rb):
    hg = pl.program_id(1)
    pair = pl.program_id(2)
    qi = qi_tab[pair]
    ki = ki_tab[pair]
    w = q_ref.shape[1]
    dh = w // hb

    @pl.when(ki == 0)
    def _():
        m_ref[...] = jnp.full_like(m_ref, -jnp.inf)
        l_ref[...] = jnp.zeros_like(l_ref)
        acc_ref[...] = jnp.zeros_like(acc_ref)
        q = q_ref[...].astype(F32)
        cq = cq_ref[...]
        head_of_lane = lax.broadcasted_iota(jnp.int32, (tq, w), 1) // dh
        lane = lax.broadcasted_iota(jnp.int32, (tq, LANES), 1)
        for hh in range(hb):
            q2_ref[hh] = jnp.where(head_of_lane == hh, q, 0.0).astype(BF16)
            fq_ref[hh] = jnp.sum(jnp.where(lane == hg * hb + hh, cq, 0.0), axis=1, keepdims=True)

    kb_ref[...] = k_ref[...].astype(BF16)
    vb_ref[...] = v_ref[...].astype(BF16)
    head_of_lane = lax.broadcasted_iota(jnp.int32, (rb, w), 1) // dh
    rel = lax.broadcasted_iota(jnp.int32, (rb, tq), 0) - lax.broadcasted_iota(jnp.int32, (rb, tq), 1)

    def sub_block(r, masked):
        rows = pl.ds(pl.multiple_of(r * rb, rb), rb)
        acc_old = acc_ref[rows, :]
        fqs = [fq_ref[hh, rows, :] for hh in range(hb)]
        m_prevs = [m_ref[hh, rows, :] for hh in range(hb)]
        l_prevs = [l_ref[hh, rows, :] for hh in range(hb)]
        kb = kb_ref[...]
        vb = vb_ref[...]
        acc_new = acc_old
        m_news, l_news = [], []
        scores = [lax.dot_general(q2_ref[hh, rows, :], kb, _NT, preferred_element_type=F32) for hh in range(hb)]
        for hh in range(hb):
            t = scores[hh] - ck_ref[hh:hh + 1, :]
            if masked:
                t = jnp.where(rel >= -(r * rb), t, NEG_INF)
            m_new = jnp.maximum(m_prevs[hh], jnp.max(t, axis=1, keepdims=True) + fqs[hh])
            alpha = jnp.exp(m_prevs[hh] - m_new)
            p = jnp.exp(t + (fqs[hh] - m_new))
            l_news.append(alpha * l_prevs[hh] + jnp.sum(p, axis=1, keepdims=True))
            m_news.append(m_new)
            pv = jnp.dot(p.astype(BF16), vb, preferred_element_type=F32)
            acc_new = jnp.where(head_of_lane == hh, alpha * acc_old + pv, acc_new)
        for hh in range(hb):
            m_ref[hh, rows, :] = m_news[hh]
            l_ref[hh, rows, :] = l_news[hh]
        acc_ref[rows, :] = acc_new

    @pl.when(ki < qi)
    def _():
        lax.fori_loop(0, tq // rb, lambda r, c: (sub_block(r, False), c)[1], 0)

    @pl.when(ki == qi)
    def _():
        lax.fori_loop(0, tq // rb, lambda r, c: (sub_block(r, True), c)[1], 0)
        linv = 1.0 / l_ref[hb - 1]
        head_q = lax.broadcasted_iota(jnp.int32, (tq, w), 1) // dh
        for hh in range(hb - 2, -1, -1):
            linv = jnp.where(head_q == hh, 1.0 / l_ref[hh], linv)
        o_ref[...] = (acc_ref[...] * linv).astype(o_ref.dtype)


def fox_flash(q, k, v, cum, cum_t):
    bsz, t, d = q.shape
    tq = _tile(t, 512)
    rb = min(FLASH_ROWS, tq)
    hb = FLASH_HEADS
    w = hb * (d // FOX_HEADS)
    nq = t // tq
    pairs = [(qi, ki) for qi in range(nq) for ki in range(qi + 1)]
    qi_tab = jnp.asarray([pq for pq, _ in pairs], jnp.int32)
    ki_tab = jnp.asarray([pk for _, pk in pairs], jnp.int32)
    cum_rows = cum_t[:, :FOX_HEADS, :].reshape(bsz, FOX_HEADS // hb, hb, t)
    q_spec = pl.BlockSpec((None, tq, w), lambda b, hg, p, qt, kt: (b, qt[p], hg))
    kv_spec = pl.BlockSpec((None, tq, w), lambda b, hg, p, qt, kt: (b, kt[p], hg))
    grid_spec = pltpu.PrefetchScalarGridSpec(
        num_scalar_prefetch=2,
        grid=(bsz, d // w, len(pairs)),
        in_specs=[
            q_spec, kv_spec, kv_spec,
            pl.BlockSpec((None, tq, LANES), lambda b, hg, p, qt, kt: (b, qt[p], 0)),
            pl.BlockSpec((None, None, hb, tq), lambda b, hg, p, qt, kt: (b, hg, 0, kt[p])),
        ],
        out_specs=q_spec,
        scratch_shapes=[
            pltpu.VMEM((hb, tq, w), BF16),
            pltpu.VMEM((tq, w), BF16),
            pltpu.VMEM((tq, w), BF16),
            pltpu.VMEM((hb, tq, 1), F32),
            pltpu.VMEM((hb, tq, 1), F32),
            pltpu.VMEM((hb, tq, 1), F32),
            pltpu.VMEM((tq, w), F32),
        ],
    )
    return pl.pallas_call(
        functools.partial(_fox_flash_kernel, tq=tq, hb=hb, rb=rb),
        grid_spec=grid_spec,
        out_shape=jax.ShapeDtypeStruct((bsz, t, d), BF16),
        compiler_params=_params("parallel", "parallel", "arbitrary"),
        name="fox_flash",
    )(qi_tab, ki_tab, q, k, v, cum, cum_rows)


DECODE_PAGES = 4


def _fox_decode_kernel(pt_ref, q_ref, kn_ref, vn_ref, lfn_ref, *refs):
    g = DECODE_PAGES
    k_refs, v_refs, lft_refs = refs[0:g], refs[g:2 * g], refs[2 * g:3 * g]
    o_ref, qm_ref, m_ref, l_ref, acc_ref, carry_ref = refs[3 * g:]
    j = pl.program_id(1)
    d = q_ref.shape[1]
    nh = FOX_HEADS
    dh = d // nh
    page = k_refs[0].shape[1]
    head = lax.broadcasted_iota(jnp.int32, (nh, d), 0)
    own = head == lax.broadcasted_iota(jnp.int32, (nh, d), 1) // dh

    @pl.when(j == 0)
    def _():
        qm_ref[...] = jnp.where(own, jnp.broadcast_to(q_ref[...], (nh, d)), 0.0).astype(BF16)
        m_ref[...] = jnp.full_like(m_ref, -jnp.inf)
        l_ref[...] = jnp.zeros_like(l_ref)
        acc_ref[...] = jnp.zeros_like(acc_ref)
        carry_ref[...] = jnp.zeros_like(carry_ref)

    hl = lax.broadcasted_iota(jnp.int32, (nh, LANES), 0) == lax.broadcasted_iota(jnp.int32, (nh, LANES), 1)
    lf_new = jnp.sum(jnp.where(hl, jnp.broadcast_to(lfn_ref[...], (nh, LANES)), 0.0), axis=1, keepdims=True)
    later = (lax.broadcasted_iota(jnp.int32, (page, page), 0)
             > lax.broadcasted_iota(jnp.int32, (page, page), 1)).astype(F32)

    for k_ref, v_ref, lft_ref in zip(k_refs, v_refs, lft_refs):
        s = jnp.dot(qm_ref[...], k_ref[...].astype(BF16), preferred_element_type=F32)
        lft = lft_ref[...]
        after = jnp.dot(lft, later, preferred_element_type=F32, precision=lax.Precision.HIGHEST)
        s = s + lf_new + (after + carry_ref[...])
        m_prev = m_ref[...]
        m_new = jnp.maximum(m_prev, jnp.max(s, axis=1, keepdims=True))
        alpha = jnp.exp(m_prev - m_new)
        p = jnp.exp(s - m_new)
        l_ref[...] = alpha * l_ref[...] + jnp.sum(p, axis=1, keepdims=True)
        m_ref[...] = m_new
        pv = lax.dot_general(p.astype(BF16), v_ref[...].astype(BF16), _NT, preferred_element_type=F32)
        acc_ref[...] = alpha * acc_ref[...] + pv
        carry_ref[...] = carry_ref[...] + jnp.sum(lft, axis=1, keepdims=True)

    @pl.when(j == pl.num_programs(1) - 1)
    def _():
        kn = jnp.broadcast_to(kn_ref[...].astype(BF16).astype(F32), (nh, d))
        vn = jnp.broadcast_to(vn_ref[...].astype(BF16).astype(F32), (nh, d))
        s_new = jnp.sum(qm_ref[...].astype(F32) * kn, axis=1, keepdims=True)
        m_prev = m_ref[...]
        m_fin = jnp.maximum(m_prev, s_new)
        alpha = jnp.exp(m_prev - m_fin)
        p_new = jnp.exp(s_new - m_fin)
        l_fin = alpha * l_ref[...] + p_new
        acc = alpha * acc_ref[...] + p_new.astype(BF16).astype(F32) * vn
        o_ref[...] = jnp.sum(jnp.where(own, acc * (1.0 / l_fin), 0.0), axis=0, keepdims=True)


def fox_decode(q, k_new, v_new, lf_new, kt_pool, vt_pool, lft_pool, page_table, layer):
    bsz, d = q.shape
    n_pages = page_table.shape[1]
    page = kt_pool.shape[3]
    nh = FOX_HEADS
    g = DECODE_PAGES
    assert n_pages % g == 0
    row = pl.BlockSpec((None, 1, d), lambda b, j, pt: (b, 0, 0))

    def page_idx(i):
        return lambda b, j, pt: (layer, pt[b * n_pages + (n_pages - 1 - (j * g + i))], 0, 0)

    grid_spec = pltpu.PrefetchScalarGridSpec(
        num_scalar_prefetch=1,
        grid=(bsz, n_pages // g),
        in_specs=[row, row, row, pl.BlockSpec((None, 1, LANES), lambda b, j, pt: (b, 0, 0))]
        + [pl.BlockSpec((None, None, d, page), page_idx(i)) for i in range(g)]
        + [pl.BlockSpec((None, None, d, page), page_idx(i)) for i in range(g)]
        + [pl.BlockSpec((None, None, nh, page), page_idx(i)) for i in range(g)],
        out_specs=row,
        scratch_shapes=[
            pltpu.VMEM((nh, d), BF16),
            pltpu.VMEM((nh, 1), F32),
            pltpu.VMEM((nh, 1), F32),
            pltpu.VMEM((nh, d), F32),
            pltpu.VMEM((nh, 1), F32),
        ],
    )
    out = pl.pallas_call(
        _fox_decode_kernel,
        grid_spec=grid_spec,
        out_shape=jax.ShapeDtypeStruct((bsz, 1, d), F32),
        compiler_params=_params("parallel", "arbitrary"),
        name="fox_decode",
    )(page_table.reshape(-1), q.reshape(bsz, 1, d), k_new.reshape(bsz, 1, d), v_new.reshape(bsz, 1, d),
      lf_new.reshape(bsz, 1, LANES), *([kt_pool] * g + [vt_pool] * g + [lft_pool] * g))
    return out.reshape(bsz, d)


def _block_diag_gate_weights(w_rg, w_ig):
    nb, bw, _ = w_rg.shape
    half = nb // 2
    assert (half * bw) % LANES == 0
    eye = jnp.eye(half, dtype=w_rg.dtype)

    def dense(w):
        return (w[:, :, None, :] * eye[:, None, :, None]).reshape(half * bw, half * bw)

    return jnp.stack([
        jnp.concatenate([dense(w_rg[c * half:(c + 1) * half]), dense(w_ig[c * half:(c + 1) * half])], axis=1)
        for c in range(2)]).astype(BF16)


def _trunk(x, bsz, t, pos0, mem_k, mem_v, lru_h, lru_conv, pool_buf, fox_paged, p):
    d = x.shape[1]
    depth = p["norm_mix_g"].shape[0]
    hs, convs, pools, ks, vs, lfs = [], [], [], [], [], []
    y = None
    for layer in range(depth):
        kind, j = layer % N_MIXERS, layer // N_MIXERS
        g_mix = p["norm_mix_g"][layer]
        if kind == 0:
            c = p["w_lru_out"].shape[1]
            gu = norm_matmul(x, g_mix, p["w_lru_in"][j], tn=c, name="lru_in")
            args = (p["lru_conv_w"][j], p["lru_conv_b"][j], p["lru_wg"][j], p["lru_b_rg"][j], p["lru_b_ig"][j],
                    p["lru_lambda"][j])
            if t > 1:
                yl, hl, cb = lru_prompt(gu.reshape(bsz, t, 2 * c), lru_conv[j], lru_h[j], *args, pos0=pos0)
                yl = yl.reshape(bsz * t, c)
            else:
                yl, hl, cb = lru_step(gu, lru_conv[j], lru_h[j], *args, pos0=pos0)
            hs.append(hl)
            convs.append(cb)
            x = matmul_res(yl, p["w_lru_out"][j], x, name="lru_out")
        elif kind == 1:
            dh = d // FOX_HEADS
            q, k, v, fl = fox_proj(x, g_mix, p["w_fox_qkv"][j], p["w_fox_f"][j],
                                   q_dtype=BF16 if fox_paged is None else F32, q_scale=dh ** -0.5)
            if fox_paged is None:
                lf, cum, cum_t = lf_cumsum(fl.reshape(bsz, t, LANES), p["b_fox_f"][j])
                o = fox_flash(q.reshape(bsz, t, d), k.reshape(bsz, t, d), v.reshape(bsz, t, d), cum, cum_t)
                o = o.reshape(bsz * t, d)
                lf = lf[:, :, :FOX_HEADS]
            else:
                k_pool, v_pool, lft_pool, page_table = fox_paged
                lf = lf_only(fl, p["b_fox_f"][j])
                o = fox_decode(q, k, v, lf, k_pool, v_pool, lft_pool, page_table, j)
                lf = lf[:, :FOX_HEADS].reshape(bsz, t, FOX_HEADS)
            ks.append(k.reshape(bsz, t, FOX_HEADS, dh))
            vs.append(v.reshape(bsz, t, FOX_HEADS, dh))
            lfs.append(lf)
            x = matmul_res(o, p["w_fox_o"][j], x, name="fox_out")
        else:
            args = (g_mix, p["w_pool"][j], p["b_pool"][j], p["pool_scale"][j])
            if t > 1:
                x3, pb = pool_prompt(x.reshape(bsz, t, d), pool_buf[j], *args, pos0=pos0)
                x = x3.reshape(bsz * t, d)
            else:
                x, pb = pool_step(x, pool_buf[j], *args, pos0=pos0)
            pools.append(pb)

        dx = d // XA_HEADS
        q = norm_matmul(x, p["norm_x_g"][layer], p["w_xq"][layer], tn=d, out_dtype=BF16, scale=dx ** -0.5,
                        name="xattn_q")
        if t > 1:
            o = mem_attn(q.reshape(bsz, t, d), mem_k, mem_v, layer).reshape(bsz * t, d)
        else:
            rows = 2 * SUBLANES
            o = mem_attn(jnp.broadcast_to(q[:, None, :], (bsz, rows, d)), mem_k, mem_v, layer)[:, 0, :]
        x = matmul_res(o, p["w_xo"][layer], x, name="xattn_out")

        final_g = p["final_norm_g"] if layer == depth - 1 else None
        x, y = mlp(x, p["norm_mlp_g"][layer], p["w_up"][layer], p["w_down"][layer], final_g)
    return y, hs, convs, pools, ks, vs, lfs


def kernel(x_prompt, x_sample, mem_prompt, cache_fox_k, cache_fox_v, cache_fox_lf, cache_mem_k, cache_mem_v, state_lru_h, state_lru_conv, state_pool, page_table, norm_mix_g, norm_mem_g, norm_x_g, norm_mlp_g, final_norm_g, w_lru_in, lru_conv_w, lru_conv_b, lru_w_rg, lru_b_rg, lru_w_ig, lru_b_ig, lru_lambda, w_lru_out, w_fox_qkvf, b_fox_f, w_fox_o, w_pool, b_pool, pool_scale, w_xq, w_xkv, w_xo, w_up, w_down):
    bsz, seq, d = x_prompt.shape
    dec, dec_seq, _ = x_sample.shape
    assert dec_seq == 1
    depth = norm_mix_g.shape[0]
    n_mem = mem_prompt.shape[1]
    n_fox = w_fox_qkvf.shape[0]
    n_lru = w_lru_in.shape[0]
    n_pool_layers = w_pool.shape[0]
    c = w_lru_out.shape[1]
    dt = x_prompt.dtype

    bias_pad = jnp.pad(b_fox_f, ((0, 0), (0, LANES - FOX_HEADS))).reshape(n_fox, 1, LANES)
    p = dict(
        norm_mix_g=norm_mix_g, norm_x_g=norm_x_g, norm_mlp_g=norm_mlp_g, final_norm_g=final_norm_g,
        w_lru_in=w_lru_in.astype(BF16), lru_conv_w=lru_conv_w, lru_conv_b=lru_conv_b,
        lru_wg=jnp.stack([_block_diag_gate_weights(lru_w_rg[l], lru_w_ig[l]) for l in range(n_lru)]),
        lru_b_rg=lru_b_rg, lru_b_ig=lru_b_ig, lru_lambda=lru_lambda, w_lru_out=w_lru_out.astype(BF16),
        w_fox_qkv=w_fox_qkvf[:, :, :3 * d].astype(BF16),
        w_fox_f=jnp.pad(w_fox_qkvf[:, :, 3 * d:], ((0, 0), (0, 0), (0, LANES - FOX_HEADS))).astype(BF16),
        b_fox_f=bias_pad, w_fox_o=w_fox_o.astype(BF16),
        w_pool=w_pool.astype(BF16), b_pool=b_pool, pool_scale=pool_scale,
        w_xq=w_xq.astype(BF16), w_xo=w_xo.astype(BF16), w_up=w_up.astype(BF16), w_down=w_down.astype(BF16),
    )

    mem_k_p, mem_v_p = mem_kv(mem_prompt.reshape(bsz * n_mem, d), norm_mem_g, w_xkv.astype(BF16))
    mem_k_p = mem_k_p.reshape(depth, bsz, n_mem, d)
    mem_v_p = mem_v_p.reshape(depth, bsz, n_mem, d)
    h0 = jnp.zeros((n_lru, bsz, c), dt)
    c0 = jnp.zeros((n_lru, bsz, CONV_WIDTH - 1, c), dt)
    pb0 = jnp.zeros((n_pool_layers, bsz, max(POOL_WINDOWS) - 1, d), dt)
    y_p, hs_p, convs_p, pools_p, ks_p, vs_p, lfs_p = _trunk(
        x_prompt.reshape(bsz * seq, d), bsz, seq, 0, mem_k_p, mem_v_p, h0, c0, pb0, None, p)

    n_pool_pages, page = cache_fox_k.shape[1], cache_fox_k.shape[2]
    pos_s = page_table.shape[1] * page
    channel_major = lambda c: jnp.transpose(c, (0, 1, 3, 4, 2)).reshape(n_fox, n_pool_pages, d, page)
    fox_paged = (channel_major(cache_fox_k), channel_major(cache_fox_v), jnp.swapaxes(cache_fox_lf, 2, 3), page_table)
    head_major = lambda c: jnp.swapaxes(c, 2, 3)
    y_s, hs_s, convs_s, pools_s, ks_s, vs_s, lfs_s = _trunk(
        x_sample.reshape(dec, d), dec, 1, pos_s, head_major(cache_mem_k), head_major(cache_mem_v),
        state_lru_h, state_lru_conv, state_pool, fox_paged, p)

    xa = (depth, bsz, n_mem, XA_HEADS, d // XA_HEADS)
    return (y_p.reshape(bsz, seq, d), y_s.reshape(dec, 1, d),
            jnp.stack(hs_p), jnp.stack(convs_p), jnp.stack(pools_p), jnp.stack(ks_p), jnp.stack(vs_p),
            jnp.stack(lfs_p), mem_k_p.reshape(xa), mem_v_p.reshape(xa),
            jnp.stack(hs_s), jnp.stack(convs_s), jnp.stack(pools_s), jnp.stack(ks_s), jnp.stack(vs_s),
            jnp.stack(lfs_s))
```

```python
import functools

import jax
import jax.numpy as jnp
from jax import lax
from jax.experimental import pallas as pl
from jax.experimental.pallas import tpu as pltpu

F32 = jnp.float32
BF16 = jnp.bfloat16

RMS_EPS = 1e-6
NEG_INF = -1e30
LRU_C = 8.0
CONV_WIDTH = 4
FOX_HEADS = 16
XA_HEADS = 4
POOL_WINDOWS = (2, 4, 8, 16)
N_MIXERS = 3

LANES = 128
SUBLANES = 8
HALO = 16

_NT = (((1,), (1,)), ((), ()))


def _tile(n, target):
    t = 1
    while t * 2 <= min(n, target):
        t *= 2
    while t > 1 and n % t:
        t //= 2
    return t if (n % t == 0 and t >= SUBLANES) else n


def _params(*sem):
    return pltpu.CompilerParams(dimension_semantics=sem)


def _rmsnorm(x, g):
    x = x.astype(F32)
    x = x * lax.rsqrt(jnp.mean(x * x, axis=-1, keepdims=True) + RMS_EPS)
    return x * g


def _softplus(z):
    return jnp.maximum(z, 0.0) + jnp.log1p(jnp.exp(-jnp.abs(z)))


def _sigmoid(x):
    return 0.5 * jnp.tanh(0.5 * x) + 0.5


def _gelu_tanh(x):
    c = 0.7978845608028654
    return x * (0.5 * (1.0 + jnp.tanh(c * (x + 0.044715 * (x * x * x)))))


def _norm_matmul_kernel(x_ref, g_ref, w_ref, o_ref, xn_ref, *, scale):
    @pl.when(pl.program_id(1) == 0)
    def _():
        xn_ref[...] = _rmsnorm(x_ref[...], g_ref[...]).astype(BF16)

    acc = jnp.dot(xn_ref[...], w_ref[...], preferred_element_type=F32)
    if scale != 1.0:
        acc = acc * scale
    o_ref[...] = acc.astype(o_ref.dtype)


def norm_matmul(x, g, w, *, tn, out_dtype=F32, scale=1.0, name="norm_matmul"):
    m, d = x.shape
    n = w.shape[1]
    tm = _tile(m, 1024)
    return pl.pallas_call(
        functools.partial(_norm_matmul_kernel, scale=scale),
        grid=(m // tm, n // tn),
        in_specs=[
            pl.BlockSpec((tm, d), lambda i, j: (i, 0)),
            pl.BlockSpec((1, d), lambda i, j: (0, 0)),
            pl.BlockSpec((d, tn), lambda i, j: (0, j)),
        ],
        out_specs=pl.BlockSpec((tm, tn), lambda i, j: (i, j)),
        out_shape=jax.ShapeDtypeStruct((m, n), out_dtype),
        scratch_shapes=[pltpu.VMEM((tm, d), BF16)],
        compiler_params=_params("parallel", "arbitrary"),
        name=name,
    )(x, g.reshape(1, d), w)


def _fox_proj_kernel(x_ref, g_ref, w_ref, wf_ref, q_ref, k_ref, v_ref, f_ref, xn_ref, *, q_scale):
    j = pl.program_id(1)

    @pl.when(j == 0)
    def _():
        xn_ref[...] = _rmsnorm(x_ref[...], g_ref[...]).astype(BF16)

    @pl.when(j == 0)
    def _():
        acc = jnp.dot(xn_ref[...], w_ref[...], preferred_element_type=F32)
        q_ref[...] = (acc * q_scale).astype(q_ref.dtype)

    @pl.when(j == 1)
    def _():
        k_ref[...] = jnp.dot(xn_ref[...], w_ref[...], preferred_element_type=F32)

    @pl.when(j == 2)
    def _():
        v_ref[...] = jnp.dot(xn_ref[...], w_ref[...], preferred_element_type=F32)

    @pl.when(j == 3)
    def _():
        f_ref[...] = jnp.dot(xn_ref[...], wf_ref[...], preferred_element_type=F32)


def fox_proj(x, g, w_qkv, w_f, *, q_dtype, q_scale):
    m, d = x.shape
    tm = _tile(m, 512)
    row = lambda i, j: (i, 0)
    return pl.pallas_call(
        functools.partial(_fox_proj_kernel, q_scale=q_scale),
        grid=(m // tm, 4),
        in_specs=[
            pl.BlockSpec((tm, d), row),
            pl.BlockSpec((1, d), lambda i, j: (0, 0)),
            pl.BlockSpec((d, d), lambda i, j: (0, jnp.minimum(j, 2))),
            pl.BlockSpec((d, LANES), lambda i, j: (0, 0)),
        ],
        out_specs=[
            pl.BlockSpec((tm, d), row),
            pl.BlockSpec((tm, d), row),
            pl.BlockSpec((tm, d), row),
            pl.BlockSpec((tm, LANES), row),
        ],
        out_shape=[
            jax.ShapeDtypeStruct((m, d), q_dtype),
            jax.ShapeDtypeStruct((m, d), F32),
            jax.ShapeDtypeStruct((m, d), F32),
            jax.ShapeDtypeStruct((m, LANES), F32),
        ],
        scratch_shapes=[pltpu.VMEM((tm, d), BF16)],
        compiler_params=_params("parallel", "arbitrary"),
        name="fox_proj",
    )(x, g.reshape(1, d), w_qkv, w_f)


def _mem_kv_kernel(x_ref, g_ref, w_ref, k_ref, v_ref, xn_ref):
    j = pl.program_id(2)

    @pl.when(j == 0)
    def _():
        xn_ref[...] = _rmsnorm(x_ref[...], g_ref[...]).astype(BF16)
        k_ref[...] = jnp.dot(xn_ref[...], w_ref[...], preferred_element_type=F32)

    @pl.when(j == 1)
    def _():
        v_ref[...] = jnp.dot(xn_ref[...], w_ref[...], preferred_element_type=F32)


def mem_kv(mem, g_all, w_all):
    m, d = mem.shape
    depth = g_all.shape[0]
    tm = _tile(m, 512)
    out_spec = pl.BlockSpec((None, tm, d), lambda l, i, j: (l, i, 0))
    return pl.pallas_call(
        _mem_kv_kernel,
        grid=(depth, m // tm, 2),
        in_specs=[
            pl.BlockSpec((tm, d), lambda l, i, j: (i, 0)),
            pl.BlockSpec((None, 1, d), lambda l, i, j: (l, 0, 0)),
            pl.BlockSpec((None, d, d), lambda l, i, j: (l, 0, j)),
        ],
        out_specs=[out_spec, out_spec],
        out_shape=[jax.ShapeDtypeStruct((depth, m, d), F32)] * 2,
        scratch_shapes=[pltpu.VMEM((tm, d), BF16)],
        compiler_params=_params("parallel", "parallel", "arbitrary"),
        name="mem_kv",
    )(mem, g_all.reshape(depth, 1, d), w_all)


def _matmul_res_kernel(h_ref, w_ref, r_ref, o_ref):
    o_ref[...] = r_ref[...] + jnp.dot(h_ref[...].astype(BF16), w_ref[...], preferred_element_type=F32)


def matmul_res(h, w, res, *, name="matmul_res"):
    m, k = h.shape
    n = w.shape[1]
    tm = _tile(m, 512)
    return pl.pallas_call(
        _matmul_res_kernel,
        grid=(m // tm,),
        in_specs=[
            pl.BlockSpec((tm, k), lambda i: (i, 0)),
            pl.BlockSpec((k, n), lambda i: (0, 0)),
            pl.BlockSpec((tm, n), lambda i: (i, 0)),
        ],
        out_specs=pl.BlockSpec((tm, n), lambda i: (i, 0)),
        out_shape=jax.ShapeDtypeStruct((m, n), F32),
        compiler_params=_params("parallel"),
        name=name,
    )(h, w, res)


def _mlp_kernel(*refs, final_norm):
    if final_norm:
        x_ref, g_ref, wu_ref, wd_ref, gf_ref, o_ref, y_ref, xn_ref, acc_ref = refs
    else:
        x_ref, g_ref, wu_ref, wd_ref, o_ref, xn_ref, acc_ref = refs
    j = pl.program_id(1)

    @pl.when(j == 0)
    def _():
        xn_ref[...] = _rmsnorm(x_ref[...], g_ref[...]).astype(BF16)
        acc_ref[...] = jnp.zeros_like(acc_ref)

    h = jnp.dot(xn_ref[...], wu_ref[...], preferred_element_type=F32)
    h = jnp.square(jnp.maximum(h, 0.0)).astype(BF16)
    acc_ref[...] += jnp.dot(h, wd_ref[...], preferred_element_type=F32)

    @pl.when(j == pl.num_programs(1) - 1)
    def _():
        out = x_ref[...] + acc_ref[...]
        o_ref[...] = out
        if final_norm:
            y_ref[...] = _rmsnorm(out, gf_ref[...])


def mlp(x, g, w_up, w_down, final_g=None):
    m, d = x.shape
    f = w_up.shape[1]
    tm = _tile(m, 1024)
    tf = _tile(f, 512)
    row = lambda i, j: (i, 0)
    vec = pl.BlockSpec((1, d), lambda i, j: (0, 0))
    in_specs = [
        pl.BlockSpec((tm, d), row),
        vec,
        pl.BlockSpec((d, tf), lambda i, j: (0, j)),
        pl.BlockSpec((tf, d), lambda i, j: (j, 0)),
    ]
    args = [x, g.reshape(1, d), w_up, w_down]
    out_specs = [pl.BlockSpec((tm, d), row)]
    out_shape = [jax.ShapeDtypeStruct((m, d), F32)]
    if final_g is not None:
        in_specs.append(vec)
        args.append(final_g.reshape(1, d))
        out_specs.append(pl.BlockSpec((tm, d), row))
        out_shape.append(jax.ShapeDtypeStruct((m, d), F32))
    outs = pl.pallas_call(
        functools.partial(_mlp_kernel, final_norm=final_g is not None),
        grid=(m // tm, f // tf),
        in_specs=in_specs,
        out_specs=out_specs,
        out_shape=out_shape,
        scratch_shapes=[pltpu.VMEM((tm, d), BF16), pltpu.VMEM((tm, d), F32)],
        compiler_params=_params("parallel", "arbitrary"),
        name="mlp",
    )(*args)
    return outs if final_g is not None else (outs[0], None)


def _mem_attn_kernel(q_ref, k_ref, v_ref, o_ref, *, head_major):
    d = q_ref.shape[1]
    dh = d // XA_HEADS
    for h in range(XA_HEADS):
        sl = slice(h * dh, (h + 1) * dh)
        kh = k_ref[h] if head_major else k_ref[:, sl]
        vh = v_ref[h] if head_major else v_ref[:, sl]
        s = lax.dot_general(q_ref[:, sl], kh.astype(BF16), _NT, preferred_element_type=F32)
        e = jnp.exp(s - jnp.max(s, axis=1, keepdims=True))
        p = e * (1.0 / jnp.sum(e, axis=1, keepdims=True))
        o = jnp.dot(p.astype(BF16), vh.astype(BF16), preferred_element_type=F32)
        o_ref[:, sl] = o.astype(o_ref.dtype)


def mem_attn(q, k_all, v_all, layer):
    bsz, t, d = q.shape
    head_major = k_all.ndim == 5
    tm = _tile(t, 512)
    if head_major:
        kv_spec = pl.BlockSpec((None, None) + k_all.shape[2:], lambda b, i: (layer, b, 0, 0, 0))
    else:
        kv_spec = pl.BlockSpec((None, None) + k_all.shape[2:], lambda b, i: (layer, b, 0, 0))
    return pl.pallas_call(
        functools.partial(_mem_attn_kernel, head_major=head_major),
        grid=(bsz, t // tm),
        in_specs=[pl.BlockSpec((None, tm, d), lambda b, i: (b, i, 0)), kv_spec, kv_spec],
        out_specs=pl.BlockSpec((None, tm, d), lambda b, i: (b, i, 0)),
        out_shape=jax.ShapeDtypeStruct((bsz, t, d), BF16),
        compiler_params=_params("parallel", "parallel"),
        name="mem_attn",
    )(q, k_all, v_all)


def _lru_gates(uc, wg_ref, brg, big, lam):
    half = uc.shape[1] // 2
    ucb = uc.astype(BF16)
    g0 = jnp.dot(ucb[:, :half], wg_ref[0], preferred_element_type=F32)
    g1 = jnp.dot(ucb[:, half:], wg_ref[1], preferred_element_type=F32)
    rg = jnp.concatenate([g0[:, :half], g1[:, :half]], axis=1) + brg
    ig = jnp.concatenate([g0[:, half:], g1[:, half:]], axis=1) + big
    r = _sigmoid(rg)
    i = _sigmoid(ig)
    log_a = (-LRU_C * r) * _softplus(-lam)
    a = jnp.exp(log_a)
    mult = jnp.sqrt(-jnp.tanh(log_a) * (a * a + 1.0))
    return a, mult, i


def _scan8(a8, x8, row8):
    for s in (1, 2, 4):
        keep = row8 >= s
        xs = jnp.where(keep, pltpu.roll(x8, s, 0), 0.0)
        a_s = jnp.where(keep, pltpu.roll(a8, s, 0), 1.0)
        x8 = x8 + a8 * xs
        a8 = a8 * a_s
    return a8, x8


def _lru_prompt_kernel(gate_ref, u_ref, buf_ref, h0_ref, cw_ref, cb_ref, wg_ref, brg_ref, big_ref, lam_ref,
                       y_ref, hl_ref, cbuf_ref, ufull, a_scr, x_scr, hc, *, first_at_zero):
    i = pl.program_id(1)
    tt, c = u_ref.shape
    pad = SUBLANES

    @pl.when(i == 0)
    def _():
        ufull[0:pad, :] = buf_ref[...]
        hc[...] = h0_ref[...]

    @pl.when(i > 0)
    def _():
        ufull[0:pad, :] = ufull[tt:tt + pad, :]

    u = u_ref[...]
    ufull[pad:pad + tt, :] = u
    uc = cb_ref[...]
    for k in range(CONV_WIDTH - 1):
        off = pad - (CONV_WIDTH - 1) + k
        uc = uc + ufull[off:off + tt, :] * cw_ref[k:k + 1, :]
    uc = uc + u * cw_ref[CONV_WIDTH - 1:CONV_WIDTH, :]

    a, mult, ig = _lru_gates(uc, wg_ref, brg_ref[...], big_ref[...], lam_ref[...])
    if first_at_zero:
        row = lax.broadcasted_iota(jnp.int32, (tt, 1), 0)
        mult = jnp.where(jnp.logical_and(row == 0, i == 0), 1.0, mult)
    x_scr[...] = (mult * ig) * uc
    a_scr[...] = a

    row8 = lax.broadcasted_iota(jnp.int32, (SUBLANES, c), 0)

    def body(r, h):
        off = pl.multiple_of(r * SUBLANES, SUBLANES)
        a8, x8 = _scan8(a_scr[pl.ds(off, SUBLANES), :], x_scr[pl.ds(off, SUBLANES), :], row8)
        h8 = x8 + a8 * h
        x_scr[pl.ds(off, SUBLANES), :] = h8
        return h8[SUBLANES - 1:SUBLANES, :]

    h = lax.fori_loop(0, tt // SUBLANES, body, hc[...])
    hc[...] = h
    y_ref[...] = (_gelu_tanh(gate_ref[...]) * x_scr[...]).astype(y_ref.dtype)

    @pl.when(i == pl.num_programs(1) - 1)
    def _():
        hl_ref[...] = h
        cbuf_ref[...] = ufull[tt:tt + pad, :]


def lru_prompt(gu, conv_buf, h0, conv_w, conv_b, wg, b_rg, b_ig, lam, *, pos0):
    bsz, t, c2 = gu.shape
    c = c2 // 2
    tt = _tile(t, 256)
    pad = SUBLANES
    buf8 = jnp.pad(conv_buf, ((0, 0), (pad - (CONV_WIDTH - 1), 0), (0, 0)))
    vec = pl.BlockSpec((1, c), lambda b, i: (0, 0))
    y, h_last, cbuf = pl.pallas_call(
        functools.partial(_lru_prompt_kernel, first_at_zero=(pos0 == 0)),
        grid=(bsz, t // tt),
        in_specs=[
            pl.BlockSpec((None, tt, c), lambda b, i: (b, i, 0)),
            pl.BlockSpec((None, tt, c), lambda b, i: (b, i, 1)),
            pl.BlockSpec((None, pad, c), lambda b, i: (b, 0, 0)),
            pl.BlockSpec((None, 1, c), lambda b, i: (b, 0, 0)),
            pl.BlockSpec((CONV_WIDTH, c), lambda b, i: (0, 0)),
            vec,
            pl.BlockSpec((2, c // 2, c), lambda b, i: (0, 0, 0)),
            vec, vec, vec,
        ],
        out_specs=[
            pl.BlockSpec((None, tt, c), lambda b, i: (b, i, 0)),
            pl.BlockSpec((None, 1, c), lambda b, i: (b, 0, 0)),
            pl.BlockSpec((None, pad, c), lambda b, i: (b, 0, 0)),
        ],
        out_shape=[
            jax.ShapeDtypeStruct((bsz, t, c), BF16),
            jax.ShapeDtypeStruct((bsz, 1, c), F32),
            jax.ShapeDtypeStruct((bsz, pad, c), F32),
        ],
        scratch_shapes=[
            pltpu.VMEM((tt + pad, c), F32),
            pltpu.VMEM((tt, c), F32),
            pltpu.VMEM((tt, c), F32),
            pltpu.VMEM((1, c), F32),
        ],
        compiler_params=_params("parallel", "arbitrary"),
        name="lru_prompt",
    )(gu, gu, buf8, h0.reshape(bsz, 1, c), conv_w, conv_b.reshape(1, c), wg,
      b_rg.reshape(1, c), b_ig.reshape(1, c), lam.reshape(1, c))
    return y, h_last.reshape(bsz, c), cbuf[:, pad - (CONV_WIDTH - 1):, :]


def _lru_step_kernel(gate_ref, u_ref, buf_ref, h0_ref, cw_ref, cb_ref, wg_ref, brg_ref, big_ref, lam_ref,
                     y_ref, h_ref, *, first_at_zero):
    u = u_ref[...]
    uc = cb_ref[...]
    for k in range(CONV_WIDTH - 1):
        uc = uc + buf_ref[k] * cw_ref[k:k + 1, :]
    uc = uc + u * cw_ref[CONV_WIDTH - 1:CONV_WIDTH, :]
    a, mult, ig = _lru_gates(uc, wg_ref, brg_ref[...], big_ref[...], lam_ref[...])
    if first_at_zero:
        mult = jnp.ones_like(mult)
    h = a * h0_ref[...] + (mult * ig) * uc
    h_ref[...] = h
    y_ref[...] = (_gelu_tanh(gate_ref[...]) * h).astype(y_ref.dtype)


def lru_step(gu, conv_buf, h0, conv_w, conv_b, wg, b_rg, b_ig, lam, *, pos0):
    bsz, c2 = gu.shape
    c = c2 // 2
    buf_t = jnp.swapaxes(conv_buf, 0, 1)
    full = lambda *shape: pl.BlockSpec(shape, lambda i: (0,) * len(shape))
    y, h = pl.pallas_call(
        functools.partial(_lru_step_kernel, first_at_zero=(pos0 == 0)),
        grid=(1,),
        in_specs=[
            pl.BlockSpec((bsz, c), lambda i: (0, 0)),
            pl.BlockSpec((bsz, c), lambda i: (0, 1)),
            full(CONV_WIDTH - 1, bsz, c),
            full(bsz, c),
            full(CONV_WIDTH, c),
            full(1, c),
            full(2, c // 2, c),
            full(1, c), full(1, c), full(1, c),
        ],
        out_specs=[full(bsz, c), full(bsz, c)],
        out_shape=[jax.ShapeDtypeStruct((bsz, c), BF16), jax.ShapeDtypeStruct((bsz, c), F32)],
        compiler_params=_params("arbitrary"),
        name="lru_step",
    )(gu, gu, buf_t, h0, conv_w, conv_b.reshape(1, c), wg,
      b_rg.reshape(1, c), b_ig.reshape(1, c), lam.reshape(1, c))
    new_buf = jnp.concatenate([conv_buf[:, 1:], gu[:, None, c:]], axis=1)
    return y, h, new_buf


def _pool_groups(xn, shifted, cnt, w_ref, b, scale):
    d = xn.shape[1]
    gw = d // len(POOL_WINDOWS)
    ys = []
    for gi, win in enumerate(POOL_WINDOWS):
        ch = slice(gi * gw, (gi + 1) * gw)
        s = xn[:, ch]
        for k in range(1, win):
            s = s + shifted(k, ch)
        dd = (s / cnt(win) - xn[:, ch]).astype(BF16)
        ys.append(jnp.dot(dd, w_ref[gi], preferred_element_type=F32))
    return (jnp.concatenate(ys, axis=1) + b) * scale


def _pool_prompt_kernel(x_ref, buf_ref, g_ref, w_ref, b_ref, sc_ref, o_ref, nb_ref, full, *, pos0):
    i = pl.program_id(1)
    tm, d = x_ref.shape

    @pl.when(i == 0)
    def _():
        full[0:HALO, :] = buf_ref[...]

    @pl.when(i > 0)
    def _():
        full[0:HALO, :] = full[tm:tm + HALO, :]

    x = x_ref[...]
    xn = _rmsnorm(x, g_ref[...])
    full[HALO:HALO + tm, :] = xn
    pos = pos0 + i * tm + lax.broadcasted_iota(jnp.int32, (tm, 1), 0)
    y = _pool_groups(
        xn,
        lambda k, ch: full[HALO - k:HALO - k + tm, ch],
        lambda win: jnp.minimum(pos + 1, win).astype(F32),
        w_ref, b_ref[...], sc_ref[...])
    o_ref[...] = x + y

    @pl.when(i == pl.num_programs(1) - 1)
    def _():
        nb_ref[...] = full[tm:tm + HALO, :]


def pool_prompt(x, buf, g, w, b, scale, *, pos0):
    bsz, t, d = x.shape
    nbuf = buf.shape[1]
    tm = _tile(t, 512)
    buf16 = jnp.pad(buf, ((0, 0), (HALO - nbuf, 0), (0, 0)))
    vec = pl.BlockSpec((1, d), lambda b_, i: (0, 0))
    ng = len(POOL_WINDOWS)
    out, nb = pl.pallas_call(
        functools.partial(_pool_prompt_kernel, pos0=pos0),
        grid=(bsz, t // tm),
        in_specs=[
            pl.BlockSpec((None, tm, d), lambda b_, i: (b_, i, 0)),
            pl.BlockSpec((None, HALO, d), lambda b_, i: (b_, 0, 0)),
            vec,
            pl.BlockSpec((ng, d // ng, d // ng), lambda b_, i: (0, 0, 0)),
            vec, vec,
        ],
        out_specs=[
            pl.BlockSpec((None, tm, d), lambda b_, i: (b_, i, 0)),
            pl.BlockSpec((None, HALO, d), lambda b_, i: (b_, 0, 0)),
        ],
        out_shape=[jax.ShapeDtypeStruct((bsz, t, d), F32), jax.ShapeDtypeStruct((bsz, HALO, d), F32)],
        scratch_shapes=[pltpu.VMEM((tm + HALO, d), F32)],
        compiler_params=_params("parallel", "arbitrary"),
        name="pool_prompt",
    )(x, buf16, g.reshape(1, d), w, b.reshape(1, d), scale.reshape(1, d))
    return out, nb[:, HALO - nbuf:, :]


def _pool_step_kernel(x_ref, buf_ref, g_ref, w_ref, b_ref, sc_ref, o_ref, xn_ref, *, pos0):
    x = x_ref[...]
    xn = _rmsnorm(x, g_ref[...])
    nbuf = buf_ref.shape[0]
    y = _pool_groups(
        xn,
        lambda k, ch: buf_ref[nbuf - k, :, ch],
        lambda win: float(min(pos0 + 1, win)),
        w_ref, b_ref[...], sc_ref[...])
    o_ref[...] = x + y
    xn_ref[...] = xn


def pool_step(x, buf, g, w, b, scale, *, pos0):
    bsz, d = x.shape
    nbuf = buf.shape[1]
    ng = len(POOL_WINDOWS)
    buf_t = jnp.swapaxes(buf, 0, 1)
    full = lambda *shape: pl.BlockSpec(shape, lambda i: (0,) * len(shape))
    out, xn = pl.pallas_call(
        functools.partial(_pool_step_kernel, pos0=pos0),
        grid=(1,),
        in_specs=[full(bsz, d), full(nbuf, bsz, d), full(1, d), full(ng, d // ng, d // ng), full(1, d), full(1, d)],
        out_specs=[full(bsz, d), full(bsz, d)],
        out_shape=[jax.ShapeDtypeStruct((bsz, d), F32)] * 2,
        compiler_params=_params("arbitrary"),
        name="pool_step",
    )(x, buf_t, g.reshape(1, d), w, b.reshape(1, d), scale.reshape(1, d))
    return out, jnp.concatenate([buf[:, 1:], xn[:, None, :]], axis=1)


def _lf_cumsum_kernel(fl_ref, b_ref, lf_ref, cum_ref, cumt_ref):
    t, w = fl_ref.shape
    lf_ref[...] = -_softplus(-(fl_ref[...] + b_ref[...]))
    row8 = lax.broadcasted_iota(jnp.int32, (SUBLANES, w), 0)

    def body(r, carry):
        off = pl.multiple_of(r * SUBLANES, SUBLANES)
        x8 = lf_ref[pl.ds(off, SUBLANES), :]
        for s in (1, 2, 4):
            x8 = x8 + jnp.where(row8 >= s, pltpu.roll(x8, s, 0), 0.0)
        c8 = x8 + carry
        cum_ref[pl.ds(off, SUBLANES), :] = c8
        return c8[SUBLANES - 1:SUBLANES, :]

    lax.fori_loop(0, t // SUBLANES, body, jnp.zeros((1, w), F32))
    cumt_ref[...] = cum_ref[...].T


def lf_cumsum(fl, b_pad):
    bsz, t, w = fl.shape
    blk = pl.BlockSpec((None, t, w), lambda b: (b, 0, 0))
    return pl.pallas_call(
        _lf_cumsum_kernel,
        grid=(bsz,),
        in_specs=[blk, pl.BlockSpec((1, w), lambda b: (0, 0))],
        out_specs=[blk, blk, pl.BlockSpec((None, w, t), lambda b: (b, 0, 0))],
        out_shape=[jax.ShapeDtypeStruct((bsz, t, w), F32)] * 2 + [jax.ShapeDtypeStruct((bsz, w, t), F32)],
        compiler_params=_params("parallel"),
        name="fox_lf_cumsum",
    )(fl, b_pad)


def _lf_kernel(fl_ref, b_ref, lf_ref):
    lf_ref[...] = -_softplus(-(fl_ref[...] + b_ref[...]))


def lf_only(fl, b_pad):
    m, w = fl.shape
    return pl.pallas_call(
        _lf_kernel,
        grid=(1,),
        in_specs=[pl.BlockSpec((m, w), lambda i: (0, 0)), pl.BlockSpec((1, w), lambda i: (0, 0))],
        out_specs=pl.BlockSpec((m, w), lambda i: (0, 0)),
        out_shape=jax.ShapeDtypeStruct((m, w), F32),
        name="fox_lf",
    )(fl, b_pad)


FLASH_HEADS = 4


def _fox_flash_kernel(qi_tab, ki_tab, q_ref, k_ref, v_ref, fq_ref, ck_ref, o_ref,
                      q2_ref, kb_ref, vt_ref, m_ref, l_ref, acc_ref, *, tq, hb):
    hg = pl.program_id(1)
    pair = pl.program_id(2)
    qi = qi_tab[pair]
    ki = ki_tab[pair]
    w = q_ref.shape[1]
    dh = w // hb

    @pl.when(ki == 0)
    def _():
        m_ref[...] = jnp.full_like(m_ref, -jnp.inf)
        l_ref[...] = jnp.zeros_like(l_ref)
        acc_ref[...] = jnp.zeros_like(acc_ref)
        q = q_ref[...].astype(F32)
        head_of_lane = lax.broadcasted_iota(jnp.int32, (tq, w), 1) // dh
        for hh in range(hb):
            q2_ref[hh] = jnp.where(head_of_lane == hh, q, 0.0).astype(BF16)

    kb_ref[...] = k_ref[...].astype(BF16)
    vt_ref[...] = v_ref[...].T.astype(BF16)
    ck = ck_ref[...]
    lane = lax.broadcasted_iota(jnp.int32, (tq, LANES), 1)
    fk_cols = [jnp.sum(jnp.where(lane == hg * hb + hh, ck, 0.0), axis=1, keepdims=True) for hh in range(hb)]

    def pair_update(masked):
        scores = [lax.dot_general(kb_ref[...], q2_ref[hh], _NT, preferred_element_type=F32) for hh in range(hb)]
        if masked:
            keep = (lax.broadcasted_iota(jnp.int32, (tq, tq), 1) >= lax.broadcasted_iota(jnp.int32, (tq, tq), 0))
        for hh in range(hb):
            t = scores[hh] - fk_cols[hh]
            if masked:
                t = jnp.where(keep, t, NEG_INF)
            fq = fq_ref[hh:hh + 1, :]
            m_prev = m_ref[hh:hh + 1, :]
            m_new = jnp.maximum(m_prev, jnp.max(t, axis=0, keepdims=True) + fq)
            alpha = jnp.exp(m_prev - m_new)
            p = jnp.exp(t + (fq - m_new))
            l_ref[hh:hh + 1, :] = alpha * l_ref[hh:hh + 1, :] + jnp.sum(p, axis=0, keepdims=True)
            m_ref[hh:hh + 1, :] = m_new
            ch = slice(hh * dh, (hh + 1) * dh)
            pv = jnp.dot(vt_ref[ch, :], p.astype(BF16), preferred_element_type=F32)
            acc_ref[ch, :] = alpha * acc_ref[ch, :] + pv

    @pl.when(ki < qi)
    def _():
        pair_update(False)

    @pl.when(ki == qi)
    def _():
        pair_update(True)
        for hh in range(hb):
            ch = slice(hh * dh, (hh + 1) * dh)
            acc_ref[ch, :] = acc_ref[ch, :] * (1.0 / l_ref[hh:hh + 1, :])
        o_ref[...] = acc_ref[...].T.astype(o_ref.dtype)


def fox_flash(q, k, v, cum, cum_t):
    bsz, t, d = q.shape
    tq = _tile(t, 512)
    hb = FLASH_HEADS
    w = hb * (d // FOX_HEADS)
    nq = t // tq
    pairs = [(qi, ki) for qi in range(nq) for ki in range(qi + 1)]
    qi_tab = jnp.asarray([pq for pq, _ in pairs], jnp.int32)
    ki_tab = jnp.asarray([pk for _, pk in pairs], jnp.int32)
    cum_rows = cum_t[:, :FOX_HEADS, :].reshape(bsz, FOX_HEADS // hb, hb, t)
    q_spec = pl.BlockSpec((None, tq, w), lambda b, hg, p, qt, kt: (b, qt[p], hg))
    kv_spec = pl.BlockSpec((None, tq, w), lambda b, hg, p, qt, kt: (b, kt[p], hg))
    grid_spec = pltpu.PrefetchScalarGridSpec(
        num_scalar_prefetch=2,
        grid=(bsz, d // w, len(pairs)),
        in_specs=[
            q_spec, kv_spec, kv_spec,
            pl.BlockSpec((None, None, hb, tq), lambda b, hg, p, qt, kt: (b, hg, 0, qt[p])),
            pl.BlockSpec((None, tq, LANES), lambda b, hg, p, qt, kt: (b, kt[p], 0)),
        ],
        out_specs=q_spec,
        scratch_shapes=[
            pltpu.VMEM((hb, tq, w), BF16),
            pltpu.VMEM((tq, w), BF16),
            pltpu.VMEM((w, tq), BF16),
            pltpu.VMEM((hb, tq), F32),
            pltpu.VMEM((hb, tq), F32),
            pltpu.VMEM((w, tq), F32),
        ],
    )
    return pl.pallas_call(
        functools.partial(_fox_flash_kernel, tq=tq, hb=hb),
        grid_spec=grid_spec,
        out_shape=jax.ShapeDtypeStruct((bsz, t, d), BF16),
        compiler_params=_params("parallel", "parallel", "arbitrary"),
        name="fox_flash",
    )(qi_tab, ki_tab, q, k, v, cum_rows, cum)


DECODE_PAGES = 4


def _fox_decode_kernel(pt_ref, q_ref, kn_ref, vn_ref, lfn_ref, *refs):
    g = DECODE_PAGES
    k_refs, v_refs, lft_refs = refs[0:g], refs[g:2 * g], refs[2 * g:3 * g]
    o_ref, qm_ref, m_ref, l_ref, acc_ref, carry_ref = refs[3 * g:]
    j = pl.program_id(1)
    d = q_ref.shape[1]
    nh = FOX_HEADS
    dh = d // nh
    page = k_refs[0].shape[1]
    head = lax.broadcasted_iota(jnp.int32, (nh, d), 0)
    own = head == lax.broadcasted_iota(jnp.int32, (nh, d), 1) // dh

    @pl.when(j == 0)
    def _():
        qm_ref[...] = jnp.where(own, jnp.broadcast_to(q_ref[...], (nh, d)), 0.0).astype(BF16)
        m_ref[...] = jnp.full_like(m_ref, -jnp.inf)
        l_ref[...] = jnp.zeros_like(l_ref)
        acc_ref[...] = jnp.zeros_like(acc_ref)
        carry_ref[...] = jnp.zeros_like(carry_ref)

    hl = lax.broadcasted_iota(jnp.int32, (nh, LANES), 0) == lax.broadcasted_iota(jnp.int32, (nh, LANES), 1)
    lf_new = jnp.sum(jnp.where(hl, jnp.broadcast_to(lfn_ref[...], (nh, LANES)), 0.0), axis=1, keepdims=True)
    later = (lax.broadcasted_iota(jnp.int32, (page, page), 0)
             > lax.broadcasted_iota(jnp.int32, (page, page), 1)).astype(F32)

    qm = qm_ref[...]
    scores = [jnp.dot(qm, k_ref[...].astype(BF16), preferred_element_type=F32) for k_ref in k_refs]
    lfts = [lft_ref[...] for lft_ref in lft_refs]
    within = jnp.dot(jnp.concatenate(lfts, axis=0), later, preferred_element_type=F32,
                     precision=lax.Precision.HIGHEST)
    carry = carry_ref[...]
    logits = []
    for i in range(g):
        logits.append(scores[i] + lf_new + (within[i * nh:(i + 1) * nh, :] + carry))
        carry = carry + jnp.sum(lfts[i], axis=1, keepdims=True)
    carry_ref[...] = carry
    s = jnp.concatenate(logits, axis=1)
    m_prev = m_ref[...]
    m_new = jnp.maximum(m_prev, jnp.max(s, axis=1, keepdims=True))
    alpha = jnp.exp(m_prev - m_new)
    p = jnp.exp(s - m_new)
    l_ref[...] = alpha * l_ref[...] + jnp.sum(p, axis=1, keepdims=True)
    m_ref[...] = m_new
    pb = p.astype(BF16)
    pv = lax.dot_general(pb[:, 0:page], v_refs[0][...].astype(BF16), _NT, preferred_element_type=F32)
    for i in range(1, g):
        pv = pv + lax.dot_general(pb[:, i * page:(i + 1) * page], v_refs[i][...].astype(BF16), _NT,
                                  preferred_element_type=F32)
    acc_ref[...] = alpha * acc_ref[...] + pv

    @pl.when(j == pl.num_programs(1) - 1)
    def _():
        kn = jnp.broadcast_to(kn_ref[...].astype(BF16).astype(F32), (nh, d))
        vn = jnp.broadcast_to(vn_ref[...].astype(BF16).astype(F32), (nh, d))
        s_new = jnp.sum(qm_ref[...].astype(F32) * kn, axis=1, keepdims=True)
        m_prev = m_ref[...]
        m_fin = jnp.maximum(m_prev, s_new)
        alpha = jnp.exp(m_prev - m_fin)
        p_new = jnp.exp(s_new - m_fin)
        l_fin = alpha * l_ref[...] + p_new
        acc = alpha * acc_ref[...] + p_new.astype(BF16).astype(F32) * vn
        o_ref[...] = jnp.sum(jnp.where(own, acc * (1.0 / l_fin), 0.0), axis=0, keepdims=True)


def fox_decode(q, k_new, v_new, lf_new, kt_pool, vt_pool, lft_pool, page_table, layer):
    bsz, d = q.shape
    n_pages = page_table.shape[1]
    page = kt_pool.shape[3]
    nh = FOX_HEADS
    g = DECODE_PAGES
    assert n_pages % g == 0
    row = pl.BlockSpec((None, 1, d), lambda b, j, pt: (b, 0, 0))

    def page_idx(i):
        return lambda b, j, pt: (layer, pt[b * n_pages + (n_pages - 1 - (j * g + i))], 0, 0)

    grid_spec = pltpu.PrefetchScalarGridSpec(
        num_scalar_prefetch=1,
        grid=(bsz, n_pages // g),
        in_specs=[row, row, row, pl.BlockSpec((None, 1, LANES), lambda b, j, pt: (b, 0, 0))]
        + [pl.BlockSpec((None, None, d, page), page_idx(i)) for i in range(g)]
        + [pl.BlockSpec((None, None, d, page), page_idx(i)) for i in range(g)]
        + [pl.BlockSpec((None, None, nh, page), page_idx(i)) for i in range(g)],
        out_specs=row,
        scratch_shapes=[
            pltpu.VMEM((nh, d), BF16),
            pltpu.VMEM((nh, 1), F32),
            pltpu.VMEM((nh, 1), F32),
            pltpu.VMEM((nh, d), F32),
            pltpu.VMEM((nh, 1), F32),
        ],
    )
    out = pl.pallas_call(
        _fox_decode_kernel,
        grid_spec=grid_spec,
        out_shape=jax.ShapeDtypeStruct((bsz, 1, d), F32),
        compiler_params=_params("parallel", "arbitrary"),
        name="fox_decode",
    )(page_table.reshape(-1), q.reshape(bsz, 1, d), k_new.reshape(bsz, 1, d), v_new.reshape(bsz, 1, d),
      lf_new.reshape(bsz, 1, LANES), *([kt_pool] * g + [vt_pool] * g + [lft_pool] * g))
    return out.reshape(bsz, d)


def _block_diag_gate_weights(w_rg, w_ig):
    nb, bw, _ = w_rg.shape
    half = nb // 2
    assert (half * bw) % LANES == 0
    eye = jnp.eye(half, dtype=w_rg.dtype)

    def dense(w):
        return (w[:, :, None, :] * eye[:, None, :, None]).reshape(half * bw, half * bw)

    return jnp.stack([
        jnp.concatenate([dense(w_rg[c * half:(c + 1) * half]), dense(w_ig[c * half:(c + 1) * half])], axis=1)
        for c in range(2)]).astype(BF16)


def _trunk(x, bsz, t, pos0, mem_k, mem_v, lru_h, lru_conv, pool_buf, fox_paged, p):
    d = x.shape[1]
    depth = p["norm_mix_g"].shape[0]
    hs, convs, pools, ks, vs, lfs = [], [], [], [], [], []
    y = None
    for layer in range(depth):
        kind, j = layer % N_MIXERS, layer // N_MIXERS
        g_mix = p["norm_mix_g"][layer]
        if kind == 0:
            c = p["w_lru_out"].shape[1]
            gu = norm_matmul(x, g_mix, p["w_lru_in"][j], tn=c, name="lru_in")
            args = (p["lru_conv_w"][j], p["lru_conv_b"][j], p["lru_wg"][j], p["lru_b_rg"][j], p["lru_b_ig"][j],
                    p["lru_lambda"][j])
            if t > 1:
                yl, hl, cb = lru_prompt(gu.reshape(bsz, t, 2 * c), lru_conv[j], lru_h[j], *args, pos0=pos0)
                yl = yl.reshape(bsz * t, c)
            else:
                yl, hl, cb = lru_step(gu, lru_conv[j], lru_h[j], *args, pos0=pos0)
            hs.append(hl)
            convs.append(cb)
            x = matmul_res(yl, p["w_lru_out"][j], x, name="lru_out")
        elif kind == 1:
            dh = d // FOX_HEADS
            q, k, v, fl = fox_proj(x, g_mix, p["w_fox_qkv"][j], p["w_fox_f"][j],
                                   q_dtype=BF16 if fox_paged is None else F32, q_scale=dh ** -0.5)
            if fox_paged is None:
                lf, cum, cum_t = lf_cumsum(fl.reshape(bsz, t, LANES), p["b_fox_f"][j])
                o = fox_flash(q.reshape(bsz, t, d), k.reshape(bsz, t, d), v.reshape(bsz, t, d), cum, cum_t)
                o = o.reshape(bsz * t, d)
                lf = lf[:, :, :FOX_HEADS]
            else:
                k_pool, v_pool, lft_pool, page_table = fox_paged
                lf = lf_only(fl, p["b_fox_f"][j])
                o = fox_decode(q, k, v, lf, k_pool, v_pool, lft_pool, page_table, j)
                lf = lf[:, :FOX_HEADS].reshape(bsz, t, FOX_HEADS)
            ks.append(k.reshape(bsz, t, FOX_HEADS, dh))
            vs.append(v.reshape(bsz, t, FOX_HEADS, dh))
            lfs.append(lf)
            x = matmul_res(o, p["w_fox_o"][j], x, name="fox_out")
        else:
            args = (g_mix, p["w_pool"][j], p["b_pool"][j], p["pool_scale"][j])
            if t > 1:
                x3, pb = pool_prompt(x.reshape(bsz, t, d), pool_buf[j], *args, pos0=pos0)
                x = x3.reshape(bsz * t, d)
            else:
                x, pb = pool_step(x, pool_buf[j], *args, pos0=pos0)
            pools.append(pb)

        dx = d // XA_HEADS
        q = norm_matmul(x, p["norm_x_g"][layer], p["w_xq"][layer], tn=d, out_dtype=BF16, scale=dx ** -0.5,
                        name="xattn_q")
        if t > 1:
            o = mem_attn(q.reshape(bsz, t, d), mem_k, mem_v, layer).reshape(bsz * t, d)
        else:
            rows = 2 * SUBLANES
            o = mem_attn(jnp.broadcast_to(q[:, None, :], (bsz, rows, d)), mem_k, mem_v, layer)[:, 0, :]
        x = matmul_res(o, p["w_xo"][layer], x, name="xattn_out")

        final_g = p["final_norm_g"] if layer == depth - 1 else None
        x, y = mlp(x, p["norm_mlp_g"][layer], p["w_up"][layer], p["w_down"][layer], final_g)
    return y, hs, convs, pools, ks, vs, lfs


def kernel(x_prompt, x_sample, mem_prompt, cache_fox_k, cache_fox_v, cache_fox_lf, cache_mem_k, cache_mem_v, state_lru_h, state_lru_conv, state_pool, page_table, norm_mix_g, norm_mem_g, norm_x_g, norm_mlp_g, final_norm_g, w_lru_in, lru_conv_w, lru_conv_b, lru_w_rg, lru_b_rg, lru_w_ig, lru_b_ig, lru_lambda, w_lru_out, w_fox_qkvf, b_fox_f, w_fox_o, w_pool, b_pool, pool_scale, w_xq, w_xkv, w_xo, w_up, w_down):
    bsz, seq, d = x_prompt.shape
    dec, dec_seq, _ = x_sample.shape
    assert dec_seq == 1
    depth = norm_mix_g.shape[0]
    n_mem = mem_prompt.shape[1]
    n_fox = w_fox_qkvf.shape[0]
    n_lru = w_lru_in.shape[0]
    n_pool_layers = w_pool.shape[0]
    c = w_lru_out.shape[1]
    dt = x_prompt.dtype

    bias_pad = jnp.pad(b_fox_f, ((0, 0), (0, LANES - FOX_HEADS))).reshape(n_fox, 1, LANES)
    p = dict(
        norm_mix_g=norm_mix_g, norm_x_g=norm_x_g, norm_mlp_g=norm_mlp_g, final_norm_g=final_norm_g,
        w_lru_in=w_lru_in.astype(BF16), lru_conv_w=lru_conv_w, lru_conv_b=lru_conv_b,
        lru_wg=jnp.stack([_block_diag_gate_weights(lru_w_rg[l], lru_w_ig[l]) for l in range(n_lru)]),
        lru_b_rg=lru_b_rg, lru_b_ig=lru_b_ig, lru_lambda=lru_lambda, w_lru_out=w_lru_out.astype(BF16),
        w_fox_qkv=w_fox_qkvf[:, :, :3 * d].astype(BF16),
        w_fox_f=jnp.pad(w_fox_qkvf[:, :, 3 * d:], ((0, 0), (0, 0), (0, LANES - FOX_HEADS))).astype(BF16),
        b_fox_f=bias_pad, w_fox_o=w_fox_o.astype(BF16),
        w_pool=w_pool.astype(BF16), b_pool=b_pool, pool_scale=pool_scale,
        w_xq=w_xq.astype(BF16), w_xo=w_xo.astype(BF16), w_up=w_up.astype(BF16), w_down=w_down.astype(BF16),
    )

    mem_k_p, mem_v_p = mem_kv(mem_prompt.reshape(bsz * n_mem, d), norm_mem_g, w_xkv.astype(BF16))
    mem_k_p = mem_k_p.reshape(depth, bsz, n_mem, d)
    mem_v_p = mem_v_p.reshape(depth, bsz, n_mem, d)
    h0 = jnp.zeros((n_lru, bsz, c), dt)
    c0 = jnp.zeros((n_lru, bsz, CONV_WIDTH - 1, c), dt)
    pb0 = jnp.zeros((n_pool_layers, bsz, max(POOL_WINDOWS) - 1, d), dt)
    y_p, hs_p, convs_p, pools_p, ks_p, vs_p, lfs_p = _trunk(
        x_prompt.reshape(bsz * seq, d), bsz, seq, 0, mem_k_p, mem_v_p, h0, c0, pb0, None, p)

    n_pool_pages, page = cache_fox_k.shape[1], cache_fox_k.shape[2]
    pos_s = page_table.shape[1] * page
    channel_major = lambda c: jnp.transpose(c, (0, 1, 3, 4, 2)).reshape(n_fox, n_pool_pages, d, page)
    fox_paged = (channel_major(cache_fox_k), channel_major(cache_fox_v), jnp.swapaxes(cache_fox_lf, 2, 3), page_table)
    head_major = lambda c: jnp.swapaxes(c, 2, 3)
    y_s, hs_s, convs_s, pools_s, ks_s, vs_s, lfs_s = _trunk(
        x_sample.reshape(dec, d), dec, 1, pos_s, head_major(cache_mem_k), head_major(cache_mem_v),
        state_lru_h, state_lru_conv, state_pool, fox_paged, p)

    xa = (depth, bsz, n_mem, XA_HEADS, d // XA_HEADS)
    return (y_p.reshape(bsz, seq, d), y_s.reshape(dec, 1, d),
            jnp.stack(hs_p), jnp.stack(convs_p), jnp.stack(pools_p), jnp.stack(ks_p), jnp.stack(vs_p),
            jnp.stack(lfs_p), mem_k_p.reshape(xa), mem_v_p.reshape(xa),
            jnp.stack(hs_s), jnp.stack(convs_s), jnp.stack(pools_s), jnp.stack(ks_s), jnp.stack(vs_s),
            jnp.stack(lfs_s))
```

```python
import functools

import jax
import jax.numpy as jnp
from jax import lax
from jax.experimental import pallas as pl
from jax.experimental.pallas import tpu as pltpu

F32 = jnp.float32
BF16 = jnp.bfloat16

RMS_EPS = 1e-6
NEG_INF = -1e30
LRU_C = 8.0
CONV_WIDTH = 4
FOX_HEADS = 16
XA_HEADS = 4
POOL_WINDOWS = (2, 4, 8, 16)
N_MIXERS = 3

LANES = 128
SUBLANES = 8
HALO = 16

_NT = (((1,), (1,)), ((), ()))


def _tile(n, target):
    t = 1
    while t * 2 <= min(n, target):
        t *= 2
    while t > 1 and n % t:
        t //= 2
    return t if (n % t == 0 and t >= SUBLANES) else n


def _params(*sem):
    return pltpu.CompilerParams(dimension_semantics=sem)


def _rmsnorm(x, g):
    x = x.astype(F32)
    x = x * lax.rsqrt(jnp.mean(x * x, axis=-1, keepdims=True) + RMS_EPS)
    return x * g


def _softplus(z):
    return jnp.maximum(z, 0.0) + jnp.log1p(jnp.exp(-jnp.abs(z)))


def _sigmoid(x):
    return 0.5 * jnp.tanh(0.5 * x) + 0.5


def _gelu_tanh(x):
    c = 0.7978845608028654
    return x * (0.5 * (1.0 + jnp.tanh(c * (x + 0.044715 * (x * x * x)))))


def _norm_matmul_kernel(x_ref, g_ref, w_ref, o_ref, xn_ref, *, scale):
    @pl.when(pl.program_id(1) == 0)
    def _():
        xn_ref[...] = _rmsnorm(x_ref[...], g_ref[...]).astype(BF16)

    acc = jnp.dot(xn_ref[...], w_ref[...], preferred_element_type=F32)
    if scale != 1.0:
        acc = acc * scale
    o_ref[...] = acc.astype(o_ref.dtype)


def norm_matmul(x, g, w, *, tn, out_dtype=F32, scale=1.0, name="norm_matmul"):
    m, d = x.shape
    n = w.shape[1]
    tm = _tile(m, 1024)
    return pl.pallas_call(
        functools.partial(_norm_matmul_kernel, scale=scale),
        grid=(m // tm, n // tn),
        in_specs=[
            pl.BlockSpec((tm, d), lambda i, j: (i, 0)),
            pl.BlockSpec((1, d), lambda i, j: (0, 0)),
            pl.BlockSpec((d, tn), lambda i, j: (0, j)),
        ],
        out_specs=pl.BlockSpec((tm, tn), lambda i, j: (i, j)),
        out_shape=jax.ShapeDtypeStruct((m, n), out_dtype),
        scratch_shapes=[pltpu.VMEM((tm, d), BF16)],
        compiler_params=_params("parallel", "arbitrary"),
        name=name,
    )(x, g.reshape(1, d), w)


def _fox_proj_kernel(x_ref, g_ref, w_ref, wf_ref, q_ref, k_ref, v_ref, f_ref, xn_ref, *, q_scale):
    j = pl.program_id(1)

    @pl.when(j == 0)
    def _():
        xn_ref[...] = _rmsnorm(x_ref[...], g_ref[...]).astype(BF16)

    @pl.when(j == 0)
    def _():
        acc = jnp.dot(xn_ref[...], w_ref[...], preferred_element_type=F32)
        q_ref[...] = (acc * q_scale).astype(q_ref.dtype)

    @pl.when(j == 1)
    def _():
        k_ref[...] = jnp.dot(xn_ref[...], w_ref[...], preferred_element_type=F32)

    @pl.when(j == 2)
    def _():
        v_ref[...] = jnp.dot(xn_ref[...], w_ref[...], preferred_element_type=F32)

    @pl.when(j == 3)
    def _():
        f_ref[...] = jnp.dot(xn_ref[...], wf_ref[...], preferred_element_type=F32)


def fox_proj(x, g, w_qkv, w_f, *, q_dtype, q_scale):
    m, d = x.shape
    tm = _tile(m, 512)
    row = lambda i, j: (i, 0)
    return pl.pallas_call(
        functools.partial(_fox_proj_kernel, q_scale=q_scale),
        grid=(m // tm, 4),
        in_specs=[
            pl.BlockSpec((tm, d), row),
            pl.BlockSpec((1, d), lambda i, j: (0, 0)),
            pl.BlockSpec((d, d), lambda i, j: (0, jnp.minimum(j, 2))),
            pl.BlockSpec((d, LANES), lambda i, j: (0, 0)),
        ],
        out_specs=[
            pl.BlockSpec((tm, d), row),
            pl.BlockSpec((tm, d), row),
            pl.BlockSpec((tm, d), row),
            pl.BlockSpec((tm, LANES), row),
        ],
        out_shape=[
            jax.ShapeDtypeStruct((m, d), q_dtype),
            jax.ShapeDtypeStruct((m, d), F32),
            jax.ShapeDtypeStruct((m, d), F32),
            jax.ShapeDtypeStruct((m, LANES), F32),
        ],
        scratch_shapes=[pltpu.VMEM((tm, d), BF16)],
        compiler_params=_params("parallel", "arbitrary"),
        name="fox_proj",
    )(x, g.reshape(1, d), w_qkv, w_f)


def _mem_kv_kernel(x_ref, g_ref, w_ref, k_ref, v_ref, xn_ref):
    j = pl.program_id(2)

    @pl.when(j == 0)
    def _():
        xn_ref[...] = _rmsnorm(x_ref[...], g_ref[...]).astype(BF16)
        k_ref[...] = jnp.dot(xn_ref[...], w_ref[...], preferred_element_type=F32)

    @pl.when(j == 1)
    def _():
        v_ref[...] = jnp.dot(xn_ref[...], w_ref[...], preferred_element_type=F32)


def mem_kv(mem, g_all, w_all):
    m, d = mem.shape
    depth = g_all.shape[0]
    tm = _tile(m, 512)
    out_spec = pl.BlockSpec((None, tm, d), lambda l, i, j: (l, i, 0))
    return pl.pallas_call(
        _mem_kv_kernel,
        grid=(depth, m // tm, 2),
        in_specs=[
            pl.BlockSpec((tm, d), lambda l, i, j: (i, 0)),
            pl.BlockSpec((None, 1, d), lambda l, i, j: (l, 0, 0)),
            pl.BlockSpec((None, d, d), lambda l, i, j: (l, 0, j)),
        ],
        out_specs=[out_spec, out_spec],
        out_shape=[jax.ShapeDtypeStruct((depth, m, d), F32)] * 2,
        scratch_shapes=[pltpu.VMEM((tm, d), BF16)],
        compiler_params=_params("parallel", "parallel", "arbitrary"),
        name="mem_kv",
    )(mem, g_all.reshape(depth, 1, d), w_all)


def _matmul_res_kernel(h_ref, w_ref, r_ref, o_ref):
    o_ref[...] = r_ref[...] + jnp.dot(h_ref[...].astype(BF16), w_ref[...], preferred_element_type=F32)


def matmul_res(h, w, res, *, name="matmul_res"):
    m, k = h.shape
    n = w.shape[1]
    tm = _tile(m, 512)
    return pl.pallas_call(
        _matmul_res_kernel,
        grid=(m // tm,),
        in_specs=[
            pl.BlockSpec((tm, k), lambda i: (i, 0)),
            pl.BlockSpec((k, n), lambda i: (0, 0)),
            pl.BlockSpec((tm, n), lambda i: (i, 0)),
        ],
        out_specs=pl.BlockSpec((tm, n), lambda i: (i, 0)),
        out_shape=jax.ShapeDtypeStruct((m, n), F32),
        compiler_params=_params("parallel"),
        name=name,
    )(h, w, res)


def _mlp_kernel(*refs, final_norm):
    if final_norm:
        x_ref, g_ref, wu_ref, wd_ref, gf_ref, o_ref, y_ref, xn_ref, acc_ref = refs
    else:
        x_ref, g_ref, wu_ref, wd_ref, o_ref, xn_ref, acc_ref = refs
    j = pl.program_id(1)

    @pl.when(j == 0)
    def _():
        xn_ref[...] = _rmsnorm(x_ref[...], g_ref[...]).astype(BF16)
        acc_ref[...] = jnp.zeros_like(acc_ref)

    h = jnp.dot(xn_ref[...], wu_ref[...], preferred_element_type=F32)
    h = jnp.square(jnp.maximum(h, 0.0)).astype(BF16)
    acc_ref[...] += jnp.dot(h, wd_ref[...], preferred_element_type=F32)

    @pl.when(j == pl.num_programs(1) - 1)
    def _():
        out = x_ref[...] + acc_ref[...]
        o_ref[...] = out
        if final_norm:
            y_ref[...] = _rmsnorm(out, gf_ref[...])


def mlp(x, g, w_up, w_down, final_g=None):
    m, d = x.shape
    f = w_up.shape[1]
    tm = _tile(m, 1024)
    tf = _tile(f, 512)
    row = lambda i, j: (i, 0)
    vec = pl.BlockSpec((1, d), lambda i, j: (0, 0))
    in_specs = [
        pl.BlockSpec((tm, d), row),
        vec,
        pl.BlockSpec((d, tf), lambda i, j: (0, j)),
        pl.BlockSpec((tf, d), lambda i, j: (j, 0)),
    ]
    args = [x, g.reshape(1, d), w_up, w_down]
    out_specs = [pl.BlockSpec((tm, d), row)]
    out_shape = [jax.ShapeDtypeStruct((m, d), F32)]
    if final_g is not None:
        in_specs.append(vec)
        args.append(final_g.reshape(1, d))
        out_specs.append(pl.BlockSpec((tm, d), row))
        out_shape.append(jax.ShapeDtypeStruct((m, d), F32))
    outs = pl.pallas_call(
        functools.partial(_mlp_kernel, final_norm=final_g is not None),
        grid=(m // tm, f // tf),
        in_specs=in_specs,
        out_specs=out_specs,
        out_shape=out_shape,
        scratch_shapes=[pltpu.VMEM((tm, d), BF16), pltpu.VMEM((tm, d), F32)],
        compiler_params=_params("parallel", "arbitrary"),
        name="mlp",
    )(*args)
    return outs if final_g is not None else (outs[0], None)


def _mem_attn_kernel(q_ref, k_ref, v_ref, o_ref, *, head_major):
    d = q_ref.shape[1]
    dh = d // XA_HEADS
    for h in range(XA_HEADS):
        sl = slice(h * dh, (h + 1) * dh)
        kh = k_ref[h] if head_major else k_ref[:, sl]
        vh = v_ref[h] if head_major else v_ref[:, sl]
        s = lax.dot_general(q_ref[:, sl], kh.astype(BF16), _NT, preferred_element_type=F32)
        e = jnp.exp(s - jnp.max(s, axis=1, keepdims=True))
        p = e * (1.0 / jnp.sum(e, axis=1, keepdims=True))
        o = jnp.dot(p.astype(BF16), vh.astype(BF16), preferred_element_type=F32)
        o_ref[:, sl] = o.astype(o_ref.dtype)


def mem_attn(q, k_all, v_all, layer):
    bsz, t, d = q.shape
    head_major = k_all.ndim == 5
    tm = _tile(t, 512)
    if head_major:
        kv_spec = pl.BlockSpec((None, None) + k_all.shape[2:], lambda b, i: (layer, b, 0, 0, 0))
    else:
        kv_spec = pl.BlockSpec((None, None) + k_all.shape[2:], lambda b, i: (layer, b, 0, 0))
    return pl.pallas_call(
        functools.partial(_mem_attn_kernel, head_major=head_major),
        grid=(bsz, t // tm),
        in_specs=[pl.BlockSpec((None, tm, d), lambda b, i: (b, i, 0)), kv_spec, kv_spec],
        out_specs=pl.BlockSpec((None, tm, d), lambda b, i: (b, i, 0)),
        out_shape=jax.ShapeDtypeStruct((bsz, t, d), BF16),
        compiler_params=_params("parallel", "parallel"),
        name="mem_attn",
    )(q, k_all, v_all)


def _xattn_kernel(x_ref, g_ref, wq_ref, k_ref, v_ref, wo_ref, o_ref, q_scr, oh_scr, *, scale):
    d = x_ref.shape[1]
    dh = d // XA_HEADS
    x = x_ref[...]
    xn = _rmsnorm(x, g_ref[...]).astype(BF16)
    q_scr[...] = (jnp.dot(xn, wq_ref[...], preferred_element_type=F32) * scale).astype(BF16)
    for h in range(XA_HEADS):
        sl = slice(h * dh, (h + 1) * dh)
        s = lax.dot_general(q_scr[:, sl], k_ref[:, sl].astype(BF16), _NT, preferred_element_type=F32)
        e = jnp.exp(s - jnp.max(s, axis=1, keepdims=True))
        p = e * (1.0 / jnp.sum(e, axis=1, keepdims=True))
        oh = jnp.dot(p.astype(BF16), v_ref[:, sl].astype(BF16), preferred_element_type=F32)
        oh_scr[:, sl] = oh.astype(BF16)
    o_ref[...] = x + jnp.dot(oh_scr[...], wo_ref[...], preferred_element_type=F32)


def xattn(x, g, w_q, k_all, v_all, w_o, layer):
    bsz, t, d = x.shape
    n_mem = k_all.shape[2]
    tm = _tile(t, 512)
    kv_spec = pl.BlockSpec((None, None, n_mem, d), lambda b, i: (layer, b, 0, 0))
    w_spec = pl.BlockSpec((d, d), lambda b, i: (0, 0))
    x_spec = pl.BlockSpec((None, tm, d), lambda b, i: (b, i, 0))
    return pl.pallas_call(
        functools.partial(_xattn_kernel, scale=(d // XA_HEADS) ** -0.5),
        grid=(bsz, t // tm),
        in_specs=[x_spec, pl.BlockSpec((1, d), lambda b, i: (0, 0)), w_spec, kv_spec, kv_spec, w_spec],
        out_specs=x_spec,
        out_shape=jax.ShapeDtypeStruct((bsz, t, d), F32),
        scratch_shapes=[pltpu.VMEM((tm, d), BF16), pltpu.VMEM((tm, d), BF16)],
        compiler_params=_params("parallel", "parallel"),
        name="xattn",
    )(x, g.reshape(1, d), w_q, k_all, v_all, w_o)


def _lru_gates(uc, wg_ref, brg, big, lam):
    half = uc.shape[1] // 2
    ucb = uc.astype(BF16)
    g0 = jnp.dot(ucb[:, :half], wg_ref[0], preferred_element_type=F32)
    g1 = jnp.dot(ucb[:, half:], wg_ref[1], preferred_element_type=F32)
    rg = jnp.concatenate([g0[:, :half], g1[:, :half]], axis=1) + brg
    ig = jnp.concatenate([g0[:, half:], g1[:, half:]], axis=1) + big
    r = _sigmoid(rg)
    i = _sigmoid(ig)
    log_a = (-LRU_C * r) * _softplus(-lam)
    a = jnp.exp(log_a)
    mult = jnp.sqrt(-jnp.tanh(log_a) * (a * a + 1.0))
    return a, mult, i


def _scan8(a8, x8, row8):
    for s in (1, 2, 4):
        keep = row8 >= s
        xs = jnp.where(keep, pltpu.roll(x8, s, 0), 0.0)
        a_s = jnp.where(keep, pltpu.roll(a8, s, 0), 1.0)
        x8 = x8 + a8 * xs
        a8 = a8 * a_s
    return a8, x8


def _lru_prompt_kernel(x_ref, g_ref, win_ref, buf_ref, h0_ref, cw_ref, cb_ref, wg_ref, brg_ref, big_ref, lam_ref,
                       wout_ref, o_ref, hl_ref, cbuf_ref, ufull, a_scr, x_scr, gate_scr, hc, *, first_at_zero):
    i = pl.program_id(1)
    tt = x_ref.shape[0]
    c = wout_ref.shape[0]
    pad = SUBLANES

    @pl.when(i == 0)
    def _():
        ufull[0:pad, :] = buf_ref[...]
        hc[...] = h0_ref[...]

    @pl.when(i > 0)
    def _():
        ufull[0:pad, :] = ufull[tt:tt + pad, :]

    xn = _rmsnorm(x_ref[...], g_ref[...]).astype(BF16)
    gu = jnp.dot(xn, win_ref[...], preferred_element_type=F32)
    gate_scr[...] = _gelu_tanh(gu[:, :c])
    u = gu[:, c:]
    ufull[pad:pad + tt, :] = u
    uc = cb_ref[...]
    for k in range(CONV_WIDTH - 1):
        off = pad - (CONV_WIDTH - 1) + k
        uc = uc + ufull[off:off + tt, :] * cw_ref[k:k + 1, :]
    uc = uc + u * cw_ref[CONV_WIDTH - 1:CONV_WIDTH, :]

    a, mult, ig = _lru_gates(uc, wg_ref, brg_ref[...], big_ref[...], lam_ref[...])
    if first_at_zero:
        row = lax.broadcasted_iota(jnp.int32, (tt, 1), 0)
        mult = jnp.where(jnp.logical_and(row == 0, i == 0), 1.0, mult)
    x_scr[...] = (mult * ig) * uc
    a_scr[...] = a

    row8 = lax.broadcasted_iota(jnp.int32, (SUBLANES, c), 0)

    def body(r, h):
        off = pl.multiple_of(r * SUBLANES, SUBLANES)
        a8, x8 = _scan8(a_scr[pl.ds(off, SUBLANES), :], x_scr[pl.ds(off, SUBLANES), :], row8)
        h8 = x8 + a8 * h
        x_scr[pl.ds(off, SUBLANES), :] = h8
        return h8[SUBLANES - 1:SUBLANES, :]

    h = lax.fori_loop(0, tt // SUBLANES, body, hc[...])
    hc[...] = h
    y = (gate_scr[...] * x_scr[...]).astype(BF16)
    o_ref[...] = x_ref[...] + jnp.dot(y, wout_ref[...], preferred_element_type=F32)

    @pl.when(i == pl.num_programs(1) - 1)
    def _():
        hl_ref[...] = h
        cbuf_ref[...] = ufull[tt:tt + pad, :]


def lru_prompt(x, g, w_in, conv_buf, h0, conv_w, conv_b, wg, b_rg, b_ig, lam, w_out, *, pos0):
    bsz, t, d = x.shape
    c = w_out.shape[0]
    tt = _tile(t, 256)
    pad = SUBLANES
    buf8 = jnp.pad(conv_buf, ((0, 0), (pad - (CONV_WIDTH - 1), 0), (0, 0)))
    vec = pl.BlockSpec((1, c), lambda b, i: (0, 0))
    out, h_last, cbuf = pl.pallas_call(
        functools.partial(_lru_prompt_kernel, first_at_zero=(pos0 == 0)),
        grid=(bsz, t // tt),
        in_specs=[
            pl.BlockSpec((None, tt, d), lambda b, i: (b, i, 0)),
            pl.BlockSpec((1, d), lambda b, i: (0, 0)),
            pl.BlockSpec((d, 2 * c), lambda b, i: (0, 0)),
            pl.BlockSpec((None, pad, c), lambda b, i: (b, 0, 0)),
            pl.BlockSpec((None, 1, c), lambda b, i: (b, 0, 0)),
            pl.BlockSpec((CONV_WIDTH, c), lambda b, i: (0, 0)),
            vec,
            pl.BlockSpec((2, c // 2, c), lambda b, i: (0, 0, 0)),
            vec, vec, vec,
            pl.BlockSpec((c, d), lambda b, i: (0, 0)),
        ],
        out_specs=[
            pl.BlockSpec((None, tt, d), lambda b, i: (b, i, 0)),
            pl.BlockSpec((None, 1, c), lambda b, i: (b, 0, 0)),
            pl.BlockSpec((None, pad, c), lambda b, i: (b, 0, 0)),
        ],
        out_shape=[
            jax.ShapeDtypeStruct((bsz, t, d), F32),
            jax.ShapeDtypeStruct((bsz, 1, c), F32),
            jax.ShapeDtypeStruct((bsz, pad, c), F32),
        ],
        scratch_shapes=[
            pltpu.VMEM((tt + pad, c), F32),
            pltpu.VMEM((tt, c), F32),
            pltpu.VMEM((tt, c), F32),
            pltpu.VMEM((tt, c), F32),
            pltpu.VMEM((1, c), F32),
        ],
        compiler_params=_params("parallel", "arbitrary"),
        name="lru_prompt",
    )(x, g.reshape(1, d), w_in, buf8, h0.reshape(bsz, 1, c), conv_w, conv_b.reshape(1, c), wg,
      b_rg.reshape(1, c), b_ig.reshape(1, c), lam.reshape(1, c), w_out)
    return out, h_last.reshape(bsz, c), cbuf[:, pad - (CONV_WIDTH - 1):, :]


def _lru_step_kernel(gate_ref, u_ref, buf_ref, h0_ref, cw_ref, cb_ref, wg_ref, brg_ref, big_ref, lam_ref,
                     y_ref, h_ref, *, first_at_zero):
    u = u_ref[...]
    uc = cb_ref[...]
    for k in range(CONV_WIDTH - 1):
        uc = uc + buf_ref[k] * cw_ref[k:k + 1, :]
    uc = uc + u * cw_ref[CONV_WIDTH - 1:CONV_WIDTH, :]
    a, mult, ig = _lru_gates(uc, wg_ref, brg_ref[...], big_ref[...], lam_ref[...])
    if first_at_zero:
        mult = jnp.ones_like(mult)
    h = a * h0_ref[...] + (mult * ig) * uc
    h_ref[...] = h
    y_ref[...] = (_gelu_tanh(gate_ref[...]) * h).astype(y_ref.dtype)


def lru_step(gu, conv_buf, h0, conv_w, conv_b, wg, b_rg, b_ig, lam, *, pos0):
    bsz, c2 = gu.shape
    c = c2 // 2
    buf_t = jnp.swapaxes(conv_buf, 0, 1)
    full = lambda *shape: pl.BlockSpec(shape, lambda i: (0,) * len(shape))
    y, h = pl.pallas_call(
        functools.partial(_lru_step_kernel, first_at_zero=(pos0 == 0)),
        grid=(1,),
        in_specs=[
            pl.BlockSpec((bsz, c), lambda i: (0, 0)),
            pl.BlockSpec((bsz, c), lambda i: (0, 1)),
            full(CONV_WIDTH - 1, bsz, c),
            full(bsz, c),
            full(CONV_WIDTH, c),
            full(1, c),
            full(2, c // 2, c),
            full(1, c), full(1, c), full(1, c),
        ],
        out_specs=[full(bsz, c), full(bsz, c)],
        out_shape=[jax.ShapeDtypeStruct((bsz, c), BF16), jax.ShapeDtypeStruct((bsz, c), F32)],
        compiler_params=_params("arbitrary"),
        name="lru_step",
    )(gu, gu, buf_t, h0, conv_w, conv_b.reshape(1, c), wg,
      b_rg.reshape(1, c), b_ig.reshape(1, c), lam.reshape(1, c))
    new_buf = jnp.concatenate([conv_buf[:, 1:], gu[:, None, c:]], axis=1)
    return y, h, new_buf


def _pool_groups(xn, shifted, cnt, w_ref, b, scale):
    d = xn.shape[1]
    gw = d // len(POOL_WINDOWS)
    ys = []
    for gi, win in enumerate(POOL_WINDOWS):
        ch = slice(gi * gw, (gi + 1) * gw)
        s = xn[:, ch]
        for k in range(1, win):
            s = s + shifted(k, ch)
        dd = (s / cnt(win) - xn[:, ch]).astype(BF16)
        ys.append(jnp.dot(dd, w_ref[gi], preferred_element_type=F32))
    return (jnp.concatenate(ys, axis=1) + b) * scale


def _pool_prompt_kernel(x_ref, buf_ref, g_ref, w_ref, b_ref, sc_ref, o_ref, nb_ref, full, *, pos0):
    i = pl.program_id(1)
    tm, d = x_ref.shape

    @pl.when(i == 0)
    def _():
        full[0:HALO, :] = buf_ref[...]

    @pl.when(i > 0)
    def _():
        full[0:HALO, :] = full[tm:tm + HALO, :]

    x = x_ref[...]
    xn = _rmsnorm(x, g_ref[...])
    full[HALO:HALO + tm, :] = xn
    pos = pos0 + i * tm + lax.broadcasted_iota(jnp.int32, (tm, 1), 0)
    y = _pool_groups(
        xn,
        lambda k, ch: full[HALO - k:HALO - k + tm, ch],
        lambda win: jnp.minimum(pos + 1, win).astype(F32),
        w_ref, b_ref[...], sc_ref[...])
    o_ref[...] = x + y

    @pl.when(i == pl.num_programs(1) - 1)
    def _():
        nb_ref[...] = full[tm:tm + HALO, :]


def pool_prompt(x, buf, g, w, b, scale, *, pos0):
    bsz, t, d = x.shape
    nbuf = buf.shape[1]
    tm = _tile(t, 512)
    buf16 = jnp.pad(buf, ((0, 0), (HALO - nbuf, 0), (0, 0)))
    vec = pl.BlockSpec((1, d), lambda b_, i: (0, 0))
    ng = len(POOL_WINDOWS)
    out, nb = pl.pallas_call(
        functools.partial(_pool_prompt_kernel, pos0=pos0),
        grid=(bsz, t // tm),
        in_specs=[
            pl.BlockSpec((None, tm, d), lambda b_, i: (b_, i, 0)),
            pl.BlockSpec((None, HALO, d), lambda b_, i: (b_, 0, 0)),
            vec,
            pl.BlockSpec((ng, d // ng, d // ng), lambda b_, i: (0, 0, 0)),
            vec, vec,
        ],
        out_specs=[
            pl.BlockSpec((None, tm, d), lambda b_, i: (b_, i, 0)),
            pl.BlockSpec((None, HALO, d), lambda b_, i: (b_, 0, 0)),
        ],
        out_shape=[jax.ShapeDtypeStruct((bsz, t, d), F32), jax.ShapeDtypeStruct((bsz, HALO, d), F32)],
        scratch_shapes=[pltpu.VMEM((tm + HALO, d), F32)],
        compiler_params=_params("parallel", "arbitrary"),
        name="pool_prompt",
    )(x, buf16, g.reshape(1, d), w, b.reshape(1, d), scale.reshape(1, d))
    return out, nb[:, HALO - nbuf:, :]


def _pool_step_kernel(x_ref, buf_ref, g_ref, w_ref, b_ref, sc_ref, o_ref, xn_ref, *, pos0):
    x = x_ref[...]
    xn = _rmsnorm(x, g_ref[...])
    nbuf = buf_ref.shape[0]
    y = _pool_groups(
        xn,
        lambda k, ch: buf_ref[nbuf - k, :, ch],
        lambda win: float(min(pos0 + 1, win)),
        w_ref, b_ref[...], sc_ref[...])
    o_ref[...] = x + y
    xn_ref[...] = xn


def pool_step(x, buf, g, w, b, scale, *, pos0):
    bsz, d = x.shape
    nbuf = buf.shape[1]
    ng = len(POOL_WINDOWS)
    buf_t = jnp.swapaxes(buf, 0, 1)
    full = lambda *shape: pl.BlockSpec(shape, lambda i: (0,) * len(shape))
    out, xn = pl.pallas_call(
        functools.partial(_pool_step_kernel, pos0=pos0),
        grid=(1,),
        in_specs=[full(bsz, d), full(nbuf, bsz, d), full(1, d), full(ng, d // ng, d // ng), full(1, d), full(1, d)],
        out_specs=[full(bsz, d), full(bsz, d)],
        out_shape=[jax.ShapeDtypeStruct((bsz, d), F32)] * 2,
        compiler_params=_params("arbitrary"),
        name="pool_step",
    )(x, buf_t, g.reshape(1, d), w, b.reshape(1, d), scale.reshape(1, d))
    return out, jnp.concatenate([buf[:, 1:], xn[:, None, :]], axis=1)


def _lf_cumsum_kernel(fl_ref, b_ref, lf_ref, cum_ref, cumt_ref):
    t, w = fl_ref.shape
    lf_ref[...] = -_softplus(-(fl_ref[...] + b_ref[...]))
    row8 = lax.broadcasted_iota(jnp.int32, (SUBLANES, w), 0)

    def body(r, carry):
        off = pl.multiple_of(r * SUBLANES, SUBLANES)
        x8 = lf_ref[pl.ds(off, SUBLANES), :]
        for s in (1, 2, 4):
            x8 = x8 + jnp.where(row8 >= s, pltpu.roll(x8, s, 0), 0.0)
        c8 = x8 + carry
        cum_ref[pl.ds(off, SUBLANES), :] = c8
        return c8[SUBLANES - 1:SUBLANES, :]

    lax.fori_loop(0, t // SUBLANES, body, jnp.zeros((1, w), F32))
    cumt_ref[...] = cum_ref[...].T


def lf_cumsum(fl, b_pad):
    bsz, t, w = fl.shape
    blk = pl.BlockSpec((None, t, w), lambda b: (b, 0, 0))
    return pl.pallas_call(
        _lf_cumsum_kernel,
        grid=(bsz,),
        in_specs=[blk, pl.BlockSpec((1, w), lambda b: (0, 0))],
        out_specs=[blk, blk, pl.BlockSpec((None, w, t), lambda b: (b, 0, 0))],
        out_shape=[jax.ShapeDtypeStruct((bsz, t, w), F32)] * 2 + [jax.ShapeDtypeStruct((bsz, w, t), F32)],
        compiler_params=_params("parallel"),
        name="fox_lf_cumsum",
    )(fl, b_pad)


def _lf_kernel(fl_ref, b_ref, lf_ref):
    lf_ref[...] = -_softplus(-(fl_ref[...] + b_ref[...]))


def lf_only(fl, b_pad):
    m, w = fl.shape
    return pl.pallas_call(
        _lf_kernel,
        grid=(1,),
        in_specs=[pl.BlockSpec((m, w), lambda i: (0, 0)), pl.BlockSpec((1, w), lambda i: (0, 0))],
        out_specs=pl.BlockSpec((m, w), lambda i: (0, 0)),
        out_shape=jax.ShapeDtypeStruct((m, w), F32),
        name="fox_lf",
    )(fl, b_pad)


FLASH_HEADS = 4


def _fox_flash_kernel(qi_tab, ki_tab, q_ref, k_ref, v_ref, fq_ref, ck_ref, o_ref,
                      q2_ref, kb_ref, vt_ref, m_ref, l_ref, acc_ref, *, tq, hb):
    hg = pl.program_id(1)
    pair = pl.program_id(2)
    qi = qi_tab[pair]
    ki = ki_tab[pair]
    w = q_ref.shape[1]
    dh = w // hb

    @pl.when(ki == 0)
    def _():
        m_ref[...] = jnp.full_like(m_ref, -jnp.inf)
        l_ref[...] = jnp.zeros_like(l_ref)
        acc_ref[...] = jnp.zeros_like(acc_ref)
        q = q_ref[...].astype(F32)
        head_of_lane = lax.broadcasted_iota(jnp.int32, (tq, w), 1) // dh
        for hh in range(hb):
            q2_ref[hh] = jnp.where(head_of_lane == hh, q, 0.0).astype(BF16)

    kb_ref[...] = k_ref[...].astype(BF16)
    vt_ref[...] = v_ref[...].T.astype(BF16)
    ck = ck_ref[...]
    lane = lax.broadcasted_iota(jnp.int32, (tq, LANES), 1)
    fk_cols = [jnp.sum(jnp.where(lane == hg * hb + hh, ck, 0.0), axis=1, keepdims=True) for hh in range(hb)]

    def pair_update(masked):
        scores = [lax.dot_general(kb_ref[...], q2_ref[hh], _NT, preferred_element_type=F32) for hh in range(hb)]
        if masked:
            keep = (lax.broadcasted_iota(jnp.int32, (tq, tq), 1) >= lax.broadcasted_iota(jnp.int32, (tq, tq), 0))
        for hh in range(hb):
            t = scores[hh] - fk_cols[hh]
            if masked:
                t = jnp.where(keep, t, NEG_INF)
            fq = fq_ref[hh:hh + 1, :]
            m_prev = m_ref[hh:hh + 1, :]
            m_new = jnp.maximum(m_prev, jnp.max(t, axis=0, keepdims=True) + fq)
            alpha = jnp.exp(m_prev - m_new)
            p = jnp.exp(t + (fq - m_new))
            l_ref[hh:hh + 1, :] = alpha * l_ref[hh:hh + 1, :] + jnp.sum(p, axis=0, keepdims=True)
            m_ref[hh:hh + 1, :] = m_new
            ch = slice(hh * dh, (hh + 1) * dh)
            pv = jnp.dot(vt_ref[ch, :], p.astype(BF16), preferred_element_type=F32)
            acc_ref[ch, :] = alpha * acc_ref[ch, :] + pv

    @pl.when(ki < qi)
    def _():
        pair_update(False)

    @pl.when(ki == qi)
    def _():
        pair_update(True)
        for hh in range(hb):
            ch = slice(hh * dh, (hh + 1) * dh)
            acc_ref[ch, :] = acc_ref[ch, :] * (1.0 / l_ref[hh:hh + 1, :])
        o_ref[...] = acc_ref[...].T.astype(o_ref.dtype)


def fox_flash(q, k, v, cum, cum_t):
    bsz, t, d = q.shape
    tq = _tile(t, 512)
    hb = FLASH_HEADS
    w = hb * (d // FOX_HEADS)
    nq = t // tq
    pairs = [(qi, ki) for qi in range(nq) for ki in range(qi + 1)]
    qi_tab = jnp.asarray([pq for pq, _ in pairs], jnp.int32)
    ki_tab = jnp.asarray([pk for _, pk in pairs], jnp.int32)
    cum_rows = cum_t[:, :FOX_HEADS, :].reshape(bsz, FOX_HEADS // hb, hb, t)
    q_spec = pl.BlockSpec((None, tq, w), lambda b, hg, p, qt, kt: (b, qt[p], hg))
    kv_spec = pl.BlockSpec((None, tq, w), lambda b, hg, p, qt, kt: (b, kt[p], hg))
    grid_spec = pltpu.PrefetchScalarGridSpec(
        num_scalar_prefetch=2,
        grid=(bsz, d // w, len(pairs)),
        in_specs=[
            q_spec, kv_spec, kv_spec,
            pl.BlockSpec((None, None, hb, tq), lambda b, hg, p, qt, kt: (b, hg, 0, qt[p])),
            pl.BlockSpec((None, tq, LANES), lambda b, hg, p, qt, kt: (b, kt[p], 0)),
        ],
        out_specs=q_spec,
        scratch_shapes=[
            pltpu.VMEM((hb, tq, w), BF16),
            pltpu.VMEM((tq, w), BF16),
            pltpu.VMEM((w, tq), BF16),
            pltpu.VMEM((hb, tq), F32),
            pltpu.VMEM((hb, tq), F32),
            pltpu.VMEM((w, tq), F32),
        ],
    )
    return pl.pallas_call(
        functools.partial(_fox_flash_kernel, tq=tq, hb=hb),
        grid_spec=grid_spec,
        out_shape=jax.ShapeDtypeStruct((bsz, t, d), BF16),
        compiler_params=_params("parallel", "parallel", "arbitrary"),
        name="fox_flash",
    )(qi_tab, ki_tab, q, k, v, cum_rows, cum)


DECODE_PAGES = 8


def _fox_decode_kernel(pt_ref, q_ref, kn_ref, vn_ref, lfn_ref, *refs):
    g = DECODE_PAGES
    k_refs, v_refs, lft_refs = refs[0:g], refs[g:2 * g], refs[2 * g:3 * g]
    o_ref, qm_ref, m_ref, l_ref, acc_ref, carry_ref = refs[3 * g:]
    j = pl.program_id(1)
    d = q_ref.shape[1]
    nh = FOX_HEADS
    dh = d // nh
    page = k_refs[0].shape[1]
    head = lax.broadcasted_iota(jnp.int32, (nh, d), 0)
    own = head == lax.broadcasted_iota(jnp.int32, (nh, d), 1) // dh

    @pl.when(j == 0)
    def _():
        qm_ref[...] = jnp.where(own, jnp.broadcast_to(q_ref[...], (nh, d)), 0.0).astype(BF16)
        m_ref[...] = jnp.full_like(m_ref, -jnp.inf)
        l_ref[...] = jnp.zeros_like(l_ref)
        acc_ref[...] = jnp.zeros_like(acc_ref)
        carry_ref[...] = jnp.zeros_like(carry_ref)

    hl = lax.broadcasted_iota(jnp.int32, (nh, LANES), 0) == lax.broadcasted_iota(jnp.int32, (nh, LANES), 1)
    lf_new = jnp.sum(jnp.where(hl, jnp.broadcast_to(lfn_ref[...], (nh, LANES)), 0.0), axis=1, keepdims=True)
    later = (lax.broadcasted_iota(jnp.int32, (page, page), 0)
             > lax.broadcasted_iota(jnp.int32, (page, page), 1)).astype(F32)

    qm = qm_ref[...]
    scores = [jnp.dot(qm, k_ref[...].astype(BF16), preferred_element_type=F32) for k_ref in k_refs]
    lfts = [lft_ref[...] for lft_ref in lft_refs]
    within = jnp.dot(jnp.concatenate(lfts, axis=0), later, preferred_element_type=F32,
                     precision=lax.Precision.HIGHEST)
    carry = carry_ref[...]
    logits = []
    for i in range(g):
        logits.append(scores[i] + lf_new + (within[i * nh:(i + 1) * nh, :] + carry))
        carry = carry + jnp.sum(lfts[i], axis=1, keepdims=True)
    carry_ref[...] = carry
    s = jnp.concatenate(logits, axis=1)
    m_prev = m_ref[...]
    m_new = jnp.maximum(m_prev, jnp.max(s, axis=1, keepdims=True))
    alpha = jnp.exp(m_prev - m_new)
    p = jnp.exp(s - m_new)
    l_ref[...] = alpha * l_ref[...] + jnp.sum(p, axis=1, keepdims=True)
    m_ref[...] = m_new
    pb = p.astype(BF16)
    pv = lax.dot_general(pb[:, 0:page], v_refs[0][...].astype(BF16), _NT, preferred_element_type=F32)
    for i in range(1, g):
        pv = pv + lax.dot_general(pb[:, i * page:(i + 1) * page], v_refs[i][...].astype(BF16), _NT,
                                  preferred_element_type=F32)
    acc_ref[...] = alpha * acc_ref[...] + pv

    @pl.when(j == pl.num_programs(1) - 1)
    def _():
        kn = jnp.broadcast_to(kn_ref[...].astype(BF16).astype(F32), (nh, d))
        vn = jnp.broadcast_to(vn_ref[...].astype(BF16).astype(F32), (nh, d))
        s_new = jnp.sum(qm_ref[...].astype(F32) * kn, axis=1, keepdims=True)
        m_prev = m_ref[...]
        m_fin = jnp.maximum(m_prev, s_new)
        alpha = jnp.exp(m_prev - m_fin)
        p_new = jnp.exp(s_new - m_fin)
        l_fin = alpha * l_ref[...] + p_new
        acc = alpha * acc_ref[...] + p_new.astype(BF16).astype(F32) * vn
        o_ref[...] = jnp.sum(jnp.where(own, acc * (1.0 / l_fin), 0.0), axis=0, keepdims=True)


def fox_decode(q, k_new, v_new, lf_new, kt_pool, vt_pool, lft_pool, page_table, layer):
    bsz, d = q.shape
    n_pages = page_table.shape[1]
    page = kt_pool.shape[3]
    nh = FOX_HEADS
    g = DECODE_PAGES
    assert n_pages % g == 0
    row = pl.BlockSpec((None, 1, d), lambda b, j, pt: (b, 0, 0))

    def page_idx(i):
        return lambda b, j, pt: (layer, pt[b * n_pages + (n_pages - 1 - (j * g + i))], 0, 0)

    grid_spec = pltpu.PrefetchScalarGridSpec(
        num_scalar_prefetch=1,
        grid=(bsz, n_pages // g),
        in_specs=[row, row, row, pl.BlockSpec((None, 1, LANES), lambda b, j, pt: (b, 0, 0))]
        + [pl.BlockSpec((None, None, d, page), page_idx(i)) for i in range(g)]
        + [pl.BlockSpec((None, None, d, page), page_idx(i)) for i in range(g)]
        + [pl.BlockSpec((None, None, nh, page), page_idx(i)) for i in range(g)],
        out_specs=row,
        scratch_shapes=[
            pltpu.VMEM((nh, d), BF16),
            pltpu.VMEM((nh, 1), F32),
            pltpu.VMEM((nh, 1), F32),
            pltpu.VMEM((nh, d), F32),
            pltpu.VMEM((nh, 1), F32),
        ],
    )
    out = pl.pallas_call(
        _fox_decode_kernel,
        grid_spec=grid_spec,
        out_shape=jax.ShapeDtypeStruct((bsz, 1, d), F32),
        compiler_params=_params("parallel", "arbitrary"),
        name="fox_decode",
    )(page_table.reshape(-1), q.reshape(bsz, 1, d), k_new.reshape(bsz, 1, d), v_new.reshape(bsz, 1, d),
      lf_new.reshape(bsz, 1, LANES), *([kt_pool] * g + [vt_pool] * g + [lft_pool] * g))
    return out.reshape(bsz, d)


def _block_diag_gate_weights(w_rg, w_ig):
    nb, bw, _ = w_rg.shape
    half = nb // 2
    assert (half * bw) % LANES == 0
    eye = jnp.eye(half, dtype=w_rg.dtype)

    def dense(w):
        return (w[:, :, None, :] * eye[:, None, :, None]).reshape(half * bw, half * bw)

    return jnp.stack([
        jnp.concatenate([dense(w_rg[c * half:(c + 1) * half]), dense(w_ig[c * half:(c + 1) * half])], axis=1)
        for c in range(2)]).astype(BF16)


def _trunk(x, bsz, t, pos0, mem_k, mem_v, lru_h, lru_conv, pool_buf, fox_paged, p):
    d = x.shape[1]
    depth = p["norm_mix_g"].shape[0]
    hs, convs, pools, ks, vs, lfs = [], [], [], [], [], []
    y = None
    for layer in range(depth):
        kind, j = layer % N_MIXERS, layer // N_MIXERS
        g_mix = p["norm_mix_g"][layer]
        if kind == 0:
            c = p["w_lru_out"].shape[1]
            args = (p["lru_conv_w"][j], p["lru_conv_b"][j], p["lru_wg"][j], p["lru_b_rg"][j], p["lru_b_ig"][j],
                    p["lru_lambda"][j])
            if t > 1:
                x3, hl, cb = lru_prompt(x.reshape(bsz, t, d), g_mix, p["w_lru_in"][j], lru_conv[j], lru_h[j],
                                        *args, p["w_lru_out"][j], pos0=pos0)
                x = x3.reshape(bsz * t, d)
            else:
                gu = norm_matmul(x, g_mix, p["w_lru_in"][j], tn=c, name="lru_in")
                yl, hl, cb = lru_step(gu, lru_conv[j], lru_h[j], *args, pos0=pos0)
                x = matmul_res(yl, p["w_lru_out"][j], x, name="lru_out")
            hs.append(hl)
            convs.append(cb)
        elif kind == 1:
            dh = d // FOX_HEADS
            q, k, v, fl = fox_proj(x, g_mix, p["w_fox_qkv"][j], p["w_fox_f"][j],
                                   q_dtype=BF16 if fox_paged is None else F32, q_scale=dh ** -0.5)
            if fox_paged is None:
                lf, cum, cum_t = lf_cumsum(fl.reshape(bsz, t, LANES), p["b_fox_f"][j])
                o = fox_flash(q.reshape(bsz, t, d), k.reshape(bsz, t, d), v.reshape(bsz, t, d), cum, cum_t)
                o = o.reshape(bsz * t, d)
                lf = lf[:, :, :FOX_HEADS]
            else:
                k_pool, v_pool, lft_pool, page_table = fox_paged
                lf = lf_only(fl, p["b_fox_f"][j])
                o = fox_decode(q, k, v, lf, k_pool, v_pool, lft_pool, page_table, j)
                lf = lf[:, :FOX_HEADS].reshape(bsz, t, FOX_HEADS)
            ks.append(k.reshape(bsz, t, FOX_HEADS, dh))
            vs.append(v.reshape(bsz, t, FOX_HEADS, dh))
            lfs.append(lf)
            x = matmul_res(o, p["w_fox_o"][j], x, name="fox_out")
        else:
            args = (g_mix, p["w_pool"][j], p["b_pool"][j], p["pool_scale"][j])
            if t > 1:
                x3, pb = pool_prompt(x.reshape(bsz, t, d), pool_buf[j], *args, pos0=pos0)
                x = x3.reshape(bsz * t, d)
            else:
                x, pb = pool_step(x, pool_buf[j], *args, pos0=pos0)
            pools.append(pb)

        if t > 1:
            x = xattn(x.reshape(bsz, t, d), p["norm_x_g"][layer], p["w_xq"][layer], mem_k, mem_v,
                      p["w_xo"][layer], layer).reshape(bsz * t, d)
        else:
            q = norm_matmul(x, p["norm_x_g"][layer], p["w_xq"][layer], tn=d, out_dtype=BF16,
                            scale=(d // XA_HEADS) ** -0.5, name="xattn_q")
            rows = 2 * SUBLANES
            o = mem_attn(jnp.broadcast_to(q[:, None, :], (bsz, rows, d)), mem_k, mem_v, layer)[:, 0, :]
            x = matmul_res(o, p["w_xo"][layer], x, name="xattn_out")

        final_g = p["final_norm_g"] if layer == depth - 1 else None
        x, y = mlp(x, p["norm_mlp_g"][layer], p["w_up"][layer], p["w_down"][layer], final_g)
    return y, hs, convs, pools, ks, vs, lfs


def kernel(x_prompt, x_sample, mem_prompt, cache_fox_k, cache_fox_v, cache_fox_lf, cache_mem_k, cache_mem_v, state_lru_h, state_lru_conv, state_pool, page_table, norm_mix_g, norm_mem_g, norm_x_g, norm_mlp_g, final_norm_g, w_lru_in, lru_conv_w, lru_conv_b, lru_w_rg, lru_b_rg, lru_w_ig, lru_b_ig, lru_lambda, w_lru_out, w_fox_qkvf, b_fox_f, w_fox_o, w_pool, b_pool, pool_scale, w_xq, w_xkv, w_xo, w_up, w_down):
    bsz, seq, d = x_prompt.shape
    dec, dec_seq, _ = x_sample.shape
    assert dec_seq == 1
    depth = norm_mix_g.shape[0]
    n_mem = mem_prompt.shape[1]
    n_fox = w_fox_qkvf.shape[0]
    n_lru = w_lru_in.shape[0]
    n_pool_layers = w_pool.shape[0]
    c = w_lru_out.shape[1]
    dt = x_prompt.dtype

    bias_pad = jnp.pad(b_fox_f, ((0, 0), (0, LANES - FOX_HEADS))).reshape(n_fox, 1, LANES)
    p = dict(
        norm_mix_g=norm_mix_g, norm_x_g=norm_x_g, norm_mlp_g=norm_mlp_g, final_norm_g=final_norm_g,
        w_lru_in=w_lru_in.astype(BF16), lru_conv_w=lru_conv_w, lru_conv_b=lru_conv_b,
        lru_wg=jnp.stack([_block_diag_gate_weights(lru_w_rg[l], lru_w_ig[l]) for l in range(n_lru)]),
        lru_b_rg=lru_b_rg, lru_b_ig=lru_b_ig, lru_lambda=lru_lambda, w_lru_out=w_lru_out.astype(BF16),
        w_fox_qkv=w_fox_qkvf[:, :, :3 * d].astype(BF16),
        w_fox_f=jnp.pad(w_fox_qkvf[:, :, 3 * d:], ((0, 0), (0, 0), (0, LANES - FOX_HEADS))).astype(BF16),
        b_fox_f=bias_pad, w_fox_o=w_fox_o.astype(BF16),
        w_pool=w_pool.astype(BF16), b_pool=b_pool, pool_scale=pool_scale,
        w_xq=w_xq.astype(BF16), w_xo=w_xo.astype(BF16), w_up=w_up.astype(BF16), w_down=w_down.astype(BF16),
    )

    mem_k_p, mem_v_p = mem_kv(mem_prompt.reshape(bsz * n_mem, d), norm_mem_g, w_xkv.astype(BF16))
    mem_k_p = mem_k_p.reshape(depth, bsz, n_mem, d)
    mem_v_p = mem_v_p.reshape(depth, bsz, n_mem, d)
    h0 = jnp.zeros((n_lru, bsz, c), dt)
    c0 = jnp.zeros((n_lru, bsz, CONV_WIDTH - 1, c), dt)
    pb0 = jnp.zeros((n_pool_layers, bsz, max(POOL_WINDOWS) - 1, d), dt)
    y_p, hs_p, convs_p, pools_p, ks_p, vs_p, lfs_p = _trunk(
        x_prompt.reshape(bsz * seq, d), bsz, seq, 0, mem_k_p, mem_v_p, h0, c0, pb0, None, p)

    n_pool_pages, page = cache_fox_k.shape[1], cache_fox_k.shape[2]
    pos_s = page_table.shape[1] * page
    channel_major = lambda c: jnp.transpose(c, (0, 1, 3, 4, 2)).reshape(n_fox, n_pool_pages, d, page)
    fox_paged = (channel_major(cache_fox_k), channel_major(cache_fox_v), jnp.swapaxes(cache_fox_lf, 2, 3), page_table)
    head_major = lambda c: jnp.swapaxes(c, 2, 3)
    y_s, hs_s, convs_s, pools_s, ks_s, vs_s, lfs_s = _trunk(
        x_sample.reshape(dec, d), dec, 1, pos_s, head_major(cache_mem_k), head_major(cache_mem_v),
        state_lru_h, state_lru_conv, state_pool, fox_paged, p)

    xa = (depth, bsz, n_mem, XA_HEADS, d // XA_HEADS)
    return (y_p.reshape(bsz, seq, d), y_s.reshape(dec, 1, d),
            jnp.stack(hs_p), jnp.stack(convs_p), jnp.stack(pools_p), jnp.stack(ks_p), jnp.stack(vs_p),
            jnp.stack(lfs_p), mem_k_p.reshape(xa), mem_v_p.reshape(xa),
            jnp.stack(hs_s), jnp.stack(convs_s), jnp.stack(pools_s), jnp.stack(ks_s), jnp.stack(vs_s),
            jnp.stack(lfs_s))
```

```python
import functools

import jax
import jax.numpy as jnp
from jax import lax
from jax.experimental import pallas as pl
from jax.experimental.pallas import tpu as pltpu

F32 = jnp.float32
BF16 = jnp.bfloat16

RMS_EPS = 1e-6
NEG_INF = -1e30
LRU_C = 8.0
CONV_WIDTH = 4
FOX_HEADS = 16
XA_HEADS = 4
POOL_WINDOWS = (2, 4, 8, 16)
N_MIXERS = 3

LANES = 128
SUBLANES = 8
HALO = 16

_NT = (((1,), (1,)), ((), ()))


def _tile(n, target):
    t = 1
    while t * 2 <= min(n, target):
        t *= 2
    while t > 1 and n % t:
        t //= 2
    return t if (n % t == 0 and t >= SUBLANES) else n


def _params(*sem):
    return pltpu.CompilerParams(dimension_semantics=sem)


def _rmsnorm(x, g):
    x = x.astype(F32)
    x = x * lax.rsqrt(jnp.mean(x * x, axis=-1, keepdims=True) + RMS_EPS)
    return x * g


def _softplus(z):
    return jnp.maximum(z, 0.0) + jnp.log1p(jnp.exp(-jnp.abs(z)))


def _sigmoid(x):
    return 0.5 * jnp.tanh(0.5 * x) + 0.5


def _gelu_tanh(x):
    c = 0.7978845608028654
    return x * (0.5 * (1.0 + jnp.tanh(c * (x + 0.044715 * (x * x * x)))))


def _norm_matmul_kernel(x_ref, g_ref, w_ref, o_ref, xn_ref, *, scale):
    @pl.when(pl.program_id(1) == 0)
    def _():
        xn_ref[...] = _rmsnorm(x_ref[...], g_ref[...]).astype(BF16)

    acc = jnp.dot(xn_ref[...], w_ref[...], preferred_element_type=F32)
    if scale != 1.0:
        acc = acc * scale
    o_ref[...] = acc.astype(o_ref.dtype)


def norm_matmul(x, g, w, *, tn, out_dtype=F32, scale=1.0, name="norm_matmul"):
    m, d = x.shape
    n = w.shape[1]
    tm = _tile(m, 1024)
    return pl.pallas_call(
        functools.partial(_norm_matmul_kernel, scale=scale),
        grid=(m // tm, n // tn),
        in_specs=[
            pl.BlockSpec((tm, d), lambda i, j: (i, 0)),
            pl.BlockSpec((1, d), lambda i, j: (0, 0)),
            pl.BlockSpec((d, tn), lambda i, j: (0, j)),
        ],
        out_specs=pl.BlockSpec((tm, tn), lambda i, j: (i, j)),
        out_shape=jax.ShapeDtypeStruct((m, n), out_dtype),
        scratch_shapes=[pltpu.VMEM((tm, d), BF16)],
        compiler_params=_params("parallel", "arbitrary"),
        name=name,
    )(x, g.reshape(1, d), w)


def _fox_proj_kernel(x_ref, g_ref, w_ref, wf_ref, q_ref, k_ref, v_ref, f_ref, xn_ref, *, q_scale):
    j = pl.program_id(1)

    @pl.when(j == 0)
    def _():
        xn_ref[...] = _rmsnorm(x_ref[...], g_ref[...]).astype(BF16)

    @pl.when(j == 0)
    def _():
        acc = jnp.dot(xn_ref[...], w_ref[...], preferred_element_type=F32)
        q_ref[...] = (acc * q_scale).astype(q_ref.dtype)

    @pl.when(j == 1)
    def _():
        k_ref[...] = jnp.dot(xn_ref[...], w_ref[...], preferred_element_type=F32)

    @pl.when(j == 2)
    def _():
        v_ref[...] = jnp.dot(xn_ref[...], w_ref[...], preferred_element_type=F32)

    @pl.when(j == 3)
    def _():
        f_ref[...] = jnp.dot(xn_ref[...], wf_ref[...], preferred_element_type=F32)


def fox_proj(x, g, w_qkv, w_f, *, q_dtype, q_scale):
    m, d = x.shape
    tm = _tile(m, 1024)
    row = lambda i, j: (i, 0)
    return pl.pallas_call(
        functools.partial(_fox_proj_kernel, q_scale=q_scale),
        grid=(m // tm, 4),
        in_specs=[
            pl.BlockSpec((tm, d), row),
            pl.BlockSpec((1, d), lambda i, j: (0, 0)),
            pl.BlockSpec((d, d), lambda i, j: (0, jnp.minimum(j, 2))),
            pl.BlockSpec((d, LANES), lambda i, j: (0, 0)),
        ],
        out_specs=[
            pl.BlockSpec((tm, d), row),
            pl.BlockSpec((tm, d), row),
            pl.BlockSpec((tm, d), row),
            pl.BlockSpec((tm, LANES), row),
        ],
        out_shape=[
            jax.ShapeDtypeStruct((m, d), q_dtype),
            jax.ShapeDtypeStruct((m, d), F32),
            jax.ShapeDtypeStruct((m, d), F32),
            jax.ShapeDtypeStruct((m, LANES), F32),
        ],
        scratch_shapes=[pltpu.VMEM((tm, d), BF16)],
        compiler_params=_params("parallel", "arbitrary"),
        name="fox_proj",
    )(x, g.reshape(1, d), w_qkv, w_f)


def _mem_kv_kernel(x_ref, g_ref, w_ref, k_ref, v_ref, xn_ref):
    j = pl.program_id(2)

    @pl.when(j == 0)
    def _():
        xn_ref[...] = _rmsnorm(x_ref[...], g_ref[...]).astype(BF16)
        k_ref[...] = jnp.dot(xn_ref[...], w_ref[...], preferred_element_type=F32)

    @pl.when(j == 1)
    def _():
        v_ref[...] = jnp.dot(xn_ref[...], w_ref[...], preferred_element_type=F32)


def mem_kv(mem, g_all, w_all):
    m, d = mem.shape
    depth = g_all.shape[0]
    tm = _tile(m, 512)
    out_spec = pl.BlockSpec((None, tm, d), lambda l, i, j: (l, i, 0))
    return pl.pallas_call(
        _mem_kv_kernel,
        grid=(depth, m // tm, 2),
        in_specs=[
            pl.BlockSpec((tm, d), lambda l, i, j: (i, 0)),
            pl.BlockSpec((None, 1, d), lambda l, i, j: (l, 0, 0)),
            pl.BlockSpec((None, d, d), lambda l, i, j: (l, 0, j)),
        ],
        out_specs=[out_spec, out_spec],
        out_shape=[jax.ShapeDtypeStruct((depth, m, d), F32)] * 2,
        scratch_shapes=[pltpu.VMEM((tm, d), BF16)],
        compiler_params=_params("parallel", "parallel", "arbitrary"),
        name="mem_kv",
    )(mem, g_all.reshape(depth, 1, d), w_all)


def _matmul_res_kernel(h_ref, w_ref, r_ref, o_ref):
    o_ref[...] = r_ref[...] + jnp.dot(h_ref[...].astype(BF16), w_ref[...], preferred_element_type=F32)


def matmul_res(h, w, res, *, name="matmul_res"):
    m, k = h.shape
    n = w.shape[1]
    tm = _tile(m, 512)
    return pl.pallas_call(
        _matmul_res_kernel,
        grid=(m // tm,),
        in_specs=[
            pl.BlockSpec((tm, k), lambda i: (i, 0)),
            pl.BlockSpec((k, n), lambda i: (0, 0)),
            pl.BlockSpec((tm, n), lambda i: (i, 0)),
        ],
        out_specs=pl.BlockSpec((tm, n), lambda i: (i, 0)),
        out_shape=jax.ShapeDtypeStruct((m, n), F32),
        compiler_params=_params("parallel"),
        name=name,
    )(h, w, res)


def _mlp_kernel(*refs, final_norm):
    if final_norm:
        x_ref, g_ref, wu_ref, wd_ref, gf_ref, o_ref, y_ref, xn_ref, acc_ref = refs
    else:
        x_ref, g_ref, wu_ref, wd_ref, o_ref, xn_ref, acc_ref = refs
    j = pl.program_id(1)

    @pl.when(j == 0)
    def _():
        xn_ref[...] = _rmsnorm(x_ref[...], g_ref[...]).astype(BF16)
        acc_ref[...] = jnp.zeros_like(acc_ref)

    h = jnp.dot(xn_ref[...], wu_ref[...], preferred_element_type=F32)
    h = jnp.square(jnp.maximum(h, 0.0)).astype(BF16)
    acc_ref[...] += jnp.dot(h, wd_ref[...], preferred_element_type=F32)

    @pl.when(j == pl.num_programs(1) - 1)
    def _():
        out = x_ref[...] + acc_ref[...]
        o_ref[...] = out
        if final_norm:
            y_ref[...] = _rmsnorm(out, gf_ref[...])


def mlp(x, g, w_up, w_down, final_g=None):
    m, d = x.shape
    f = w_up.shape[1]
    tm = _tile(m, 1024)
    tf = _tile(f, 1024)
    row = lambda i, j: (i, 0)
    vec = pl.BlockSpec((1, d), lambda i, j: (0, 0))
    in_specs = [
        pl.BlockSpec((tm, d), row),
        vec,
        pl.BlockSpec((d, tf), lambda i, j: (0, j)),
        pl.BlockSpec((tf, d), lambda i, j: (j, 0)),
    ]
    args = [x, g.reshape(1, d), w_up, w_down]
    out_specs = [pl.BlockSpec((tm, d), row)]
    out_shape = [jax.ShapeDtypeStruct((m, d), F32)]
    if final_g is not None:
        in_specs.append(vec)
        args.append(final_g.reshape(1, d))
        out_specs.append(pl.BlockSpec((tm, d), row))
        out_shape.append(jax.ShapeDtypeStruct((m, d), F32))
    outs = pl.pallas_call(
        functools.partial(_mlp_kernel, final_norm=final_g is not None),
        grid=(m // tm, f // tf),
        in_specs=in_specs,
        out_specs=out_specs,
        out_shape=out_shape,
        scratch_shapes=[pltpu.VMEM((tm, d), BF16), pltpu.VMEM((tm, d), F32)],
        compiler_params=_params("parallel", "arbitrary"),
        name="mlp",
    )(*args)
    return outs if final_g is not None else (outs[0], None)


MEM_ATTN_SEQS = 4


def _mem_attn_kernel(q_ref, k_ref, v_ref, o_ref):
    nb, _, d = q_ref.shape
    dh = d // XA_HEADS
    pairs = [(n, h) for n in range(nb) for h in range(XA_HEADS)]
    scores = [lax.dot_general(q_ref[n, :, h * dh:(h + 1) * dh], k_ref[n, h].astype(BF16), _NT,
                              preferred_element_type=F32) for n, h in pairs]
    probs = []
    for s in scores:
        e = jnp.exp(s - jnp.max(s, axis=1, keepdims=True))
        probs.append((e * (1.0 / jnp.sum(e, axis=1, keepdims=True))).astype(BF16))
    for (n, h), p in zip(pairs, probs):
        o = jnp.dot(p, v_ref[n, h].astype(BF16), preferred_element_type=F32)
        o_ref[n, :, h * dh:(h + 1) * dh] = o.astype(o_ref.dtype)


def mem_attn(q, k_all, v_all, layer):
    bsz, t, d = q.shape
    nb = MEM_ATTN_SEQS if bsz % MEM_ATTN_SEQS == 0 else 1
    kv_spec = pl.BlockSpec((None, nb) + k_all.shape[2:], lambda b: (layer, b, 0, 0, 0))
    q_spec = pl.BlockSpec((nb, t, d), lambda b: (b, 0, 0))
    return pl.pallas_call(
        _mem_attn_kernel,
        grid=(bsz // nb,),
        in_specs=[q_spec, kv_spec, kv_spec],
        out_specs=q_spec,
        out_shape=jax.ShapeDtypeStruct((bsz, t, d), BF16),
        compiler_params=_params("parallel"),
        name="mem_attn",
    )(q, k_all, v_all)


def _xattn_kernel(x_ref, g_ref, wq_ref, k_ref, v_ref, wo_ref, o_ref, q_scr, oh_scr, *, scale):
    d = x_ref.shape[1]
    dh = d // XA_HEADS
    x = x_ref[...]
    xn = _rmsnorm(x, g_ref[...]).astype(BF16)
    q_scr[...] = (jnp.dot(xn, wq_ref[...], preferred_element_type=F32) * scale).astype(BF16)
    heads = [slice(h * dh, (h + 1) * dh) for h in range(XA_HEADS)]
    scores = [lax.dot_general(q_scr[:, sl], k_ref[:, sl].astype(BF16), _NT, preferred_element_type=F32)
              for sl in heads]
    probs = []
    for s in scores:
        e = jnp.exp(s - jnp.max(s, axis=1, keepdims=True))
        probs.append((e * (1.0 / jnp.sum(e, axis=1, keepdims=True))).astype(BF16))
    for sl, p in zip(heads, probs):
        oh_scr[:, sl] = jnp.dot(p, v_ref[:, sl].astype(BF16), preferred_element_type=F32).astype(BF16)
    o_ref[...] = x + jnp.dot(oh_scr[...], wo_ref[...], preferred_element_type=F32)


def xattn(x, g, w_q, k_all, v_all, w_o, layer):
    bsz, t, d = x.shape
    n_mem = k_all.shape[2]
    tm = _tile(t, 512)
    kv_spec = pl.BlockSpec((None, None, n_mem, d), lambda b, i: (layer, b, 0, 0))
    w_spec = pl.BlockSpec((d, d), lambda b, i: (0, 0))
    x_spec = pl.BlockSpec((None, tm, d), lambda b, i: (b, i, 0))
    return pl.pallas_call(
        functools.partial(_xattn_kernel, scale=(d // XA_HEADS) ** -0.5),
        grid=(bsz, t // tm),
        in_specs=[x_spec, pl.BlockSpec((1, d), lambda b, i: (0, 0)), w_spec, kv_spec, kv_spec, w_spec],
        out_specs=x_spec,
        out_shape=jax.ShapeDtypeStruct((bsz, t, d), F32),
        scratch_shapes=[pltpu.VMEM((tm, d), BF16), pltpu.VMEM((tm, d), BF16)],
        compiler_params=_params("parallel", "parallel"),
        name="xattn",
    )(x, g.reshape(1, d), w_q, k_all, v_all, w_o)


def _lru_gates(uc, wg_ref, brg, big, lam):
    half = uc.shape[1] // 2
    ucb = uc.astype(BF16)
    g0 = jnp.dot(ucb[:, :half], wg_ref[0], preferred_element_type=F32)
    g1 = jnp.dot(ucb[:, half:], wg_ref[1], preferred_element_type=F32)
    rg = jnp.concatenate([g0[:, :half], g1[:, :half]], axis=1) + brg
    ig = jnp.concatenate([g0[:, half:], g1[:, half:]], axis=1) + big
    r = _sigmoid(rg)
    i = _sigmoid(ig)
    log_a = (-LRU_C * r) * _softplus(-lam)
    a = jnp.exp(log_a)
    mult = jnp.sqrt(-jnp.tanh(log_a) * (a * a + 1.0))
    return a, mult, i


def _scan8(a8, x8, row8):
    for s in (1, 2, 4):
        keep = row8 >= s
        xs = jnp.where(keep, pltpu.roll(x8, s, 0), 0.0)
        a_s = jnp.where(keep, pltpu.roll(a8, s, 0), 1.0)
        x8 = x8 + a8 * xs
        a8 = a8 * a_s
    return a8, x8


def _lru_prompt_kernel(x_ref, g_ref, win_ref, buf_ref, h0_ref, cw_ref, cb_ref, wg_ref, brg_ref, big_ref, lam_ref,
                       wout_ref, o_ref, hl_ref, cbuf_ref, ufull, a_scr, x_scr, gate_scr, hc, *, first_at_zero):
    i = pl.program_id(1)
    tt = x_ref.shape[0]
    c = wout_ref.shape[0]
    pad = SUBLANES

    @pl.when(i == 0)
    def _():
        ufull[0:pad, :] = buf_ref[...]
        hc[...] = h0_ref[...]

    @pl.when(i > 0)
    def _():
        ufull[0:pad, :] = ufull[tt:tt + pad, :]

    xn = _rmsnorm(x_ref[...], g_ref[...]).astype(BF16)
    gu = jnp.dot(xn, win_ref[...], preferred_element_type=F32)
    gate_scr[...] = _gelu_tanh(gu[:, :c])
    u = gu[:, c:]
    ufull[pad:pad + tt, :] = u
    uc = cb_ref[...]
    for k in range(CONV_WIDTH - 1):
        off = pad - (CONV_WIDTH - 1) + k
        uc = uc + ufull[off:off + tt, :] * cw_ref[k:k + 1, :]
    uc = uc + u * cw_ref[CONV_WIDTH - 1:CONV_WIDTH, :]

    a, mult, ig = _lru_gates(uc, wg_ref, brg_ref[...], big_ref[...], lam_ref[...])
    if first_at_zero:
        row = lax.broadcasted_iota(jnp.int32, (tt, 1), 0)
        mult = jnp.where(jnp.logical_and(row == 0, i == 0), 1.0, mult)
    xin = (mult * ig) * uc

    row8 = lax.broadcasted_iota(jnp.int32, (SUBLANES, c), 0)
    for grp in range(tt // SUBLANES):
        rows = slice(grp * SUBLANES, (grp + 1) * SUBLANES)
        a8, x8 = _scan8(a[rows, :], xin[rows, :], row8)
        a_scr[rows, :] = a8
        x_scr[rows, :] = x8

    def body(r, h):
        off = pl.multiple_of(r * SUBLANES, SUBLANES)
        h8 = x_scr[pl.ds(off, SUBLANES), :] + a_scr[pl.ds(off, SUBLANES), :] * h
        x_scr[pl.ds(off, SUBLANES), :] = h8
        return h8[SUBLANES - 1:SUBLANES, :]

    h = lax.fori_loop(0, tt // SUBLANES, body, hc[...])
    hc[...] = h
    y = (gate_scr[...] * x_scr[...]).astype(BF16)
    o_ref[...] = x_ref[...] + jnp.dot(y, wout_ref[...], preferred_element_type=F32)

    @pl.when(i == pl.num_programs(1) - 1)
    def _():
        hl_ref[...] = h
        cbuf_ref[...] = ufull[tt:tt + pad, :]


def lru_prompt(x, g, w_in, conv_buf, h0, conv_w, conv_b, wg, b_rg, b_ig, lam, w_out, *, pos0):
    bsz, t, d = x.shape
    c = w_out.shape[0]
    tt = _tile(t, 256)
    pad = SUBLANES
    buf8 = jnp.pad(conv_buf, ((0, 0), (pad - (CONV_WIDTH - 1), 0), (0, 0)))
    vec = pl.BlockSpec((1, c), lambda b, i: (0, 0))
    out, h_last, cbuf = pl.pallas_call(
        functools.partial(_lru_prompt_kernel, first_at_zero=(pos0 == 0)),
        grid=(bsz, t // tt),
        in_specs=[
            pl.BlockSpec((None, tt, d), lambda b, i: (b, i, 0)),
            pl.BlockSpec((1, d), lambda b, i: (0, 0)),
            pl.BlockSpec((d, 2 * c), lambda b, i: (0, 0)),
            pl.BlockSpec((None, pad, c), lambda b, i: (b, 0, 0)),
            pl.BlockSpec((None, 1, c), lambda b, i: (b, 0, 0)),
            pl.BlockSpec((CONV_WIDTH, c), lambda b, i: (0, 0)),
            vec,
            pl.BlockSpec((2, c // 2, c), lambda b, i: (0, 0, 0)),
            vec, vec, vec,
            pl.BlockSpec((c, d), lambda b, i: (0, 0)),
        ],
        out_specs=[
            pl.BlockSpec((None, tt, d), lambda b, i: (b, i, 0)),
            pl.BlockSpec((None, 1, c), lambda b, i: (b, 0, 0)),
            pl.BlockSpec((None, pad, c), lambda b, i: (b, 0, 0)),
        ],
        out_shape=[
            jax.ShapeDtypeStruct((bsz, t, d), F32),
            jax.ShapeDtypeStruct((bsz, 1, c), F32),
            jax.ShapeDtypeStruct((bsz, pad, c), F32),
        ],
        scratch_shapes=[
            pltpu.VMEM((tt + pad, c), F32),
            pltpu.VMEM((tt, c), F32),
            pltpu.VMEM((tt, c), F32),
            pltpu.VMEM((tt, c), F32),
            pltpu.VMEM((1, c), F32),
        ],
        compiler_params=_params("parallel", "arbitrary"),
        name="lru_prompt",
    )(x, g.reshape(1, d), w_in, buf8, h0.reshape(bsz, 1, c), conv_w, conv_b.reshape(1, c), wg,
      b_rg.reshape(1, c), b_ig.reshape(1, c), lam.reshape(1, c), w_out)
    return out, h_last.reshape(bsz, c), cbuf[:, pad - (CONV_WIDTH - 1):, :]


def _lru_step_kernel(gate_ref, u_ref, buf_ref, h0_ref, cw_ref, cb_ref, wg_ref, brg_ref, big_ref, lam_ref,
                     y_ref, h_ref, *, first_at_zero):
    u = u_ref[...]
    uc = cb_ref[...]
    for k in range(CONV_WIDTH - 1):
        uc = uc + buf_ref[k] * cw_ref[k:k + 1, :]
    uc = uc + u * cw_ref[CONV_WIDTH - 1:CONV_WIDTH, :]
    a, mult, ig = _lru_gates(uc, wg_ref, brg_ref[...], big_ref[...], lam_ref[...])
    if first_at_zero:
        mult = jnp.ones_like(mult)
    h = a * h0_ref[...] + (mult * ig) * uc
    h_ref[...] = h
    y_ref[...] = (_gelu_tanh(gate_ref[...]) * h).astype(y_ref.dtype)


def lru_step(gu, conv_buf, h0, conv_w, conv_b, wg, b_rg, b_ig, lam, *, pos0):
    bsz, c2 = gu.shape
    c = c2 // 2
    buf_t = jnp.swapaxes(conv_buf, 0, 1)
    full = lambda *shape: pl.BlockSpec(shape, lambda i: (0,) * len(shape))
    y, h = pl.pallas_call(
        functools.partial(_lru_step_kernel, first_at_zero=(pos0 == 0)),
        grid=(1,),
        in_specs=[
            pl.BlockSpec((bsz, c), lambda i: (0, 0)),
            pl.BlockSpec((bsz, c), lambda i: (0, 1)),
            full(CONV_WIDTH - 1, bsz, c),
            full(bsz, c),
            full(CONV_WIDTH, c),
            full(1, c),
            full(2, c // 2, c),
            full(1, c), full(1, c), full(1, c),
        ],
        out_specs=[full(bsz, c), full(bsz, c)],
        out_shape=[jax.ShapeDtypeStruct((bsz, c), BF16), jax.ShapeDtypeStruct((bsz, c), F32)],
        compiler_params=_params("arbitrary"),
        name="lru_step",
    )(gu, gu, buf_t, h0, conv_w, conv_b.reshape(1, c), wg,
      b_rg.reshape(1, c), b_ig.reshape(1, c), lam.reshape(1, c))
    new_buf = jnp.concatenate([conv_buf[:, 1:], gu[:, None, c:]], axis=1)
    return y, h, new_buf


def _pool_groups(xn, shifted, cnt, w_ref, b, scale):
    d = xn.shape[1]
    gw = d // len(POOL_WINDOWS)
    ys = []
    for gi, win in enumerate(POOL_WINDOWS):
        ch = slice(gi * gw, (gi + 1) * gw)
        s = xn[:, ch]
        for k in range(1, win):
            s = s + shifted(k, ch)
        dd = (s / cnt(win) - xn[:, ch]).astype(BF16)
        ys.append(jnp.dot(dd, w_ref[gi], preferred_element_type=F32))
    return (jnp.concatenate(ys, axis=1) + b) * scale


def _pool_prompt_kernel(x_ref, buf_ref, g_ref, w_ref, b_ref, sc_ref, o_ref, nb_ref, full, *, pos0):
    i = pl.program_id(1)
    tm, d = x_ref.shape

    @pl.when(i == 0)
    def _():
        full[0:HALO, :] = buf_ref[...]

    @pl.when(i > 0)
    def _():
        full[0:HALO, :] = full[tm:tm + HALO, :]

    x = x_ref[...]
    xn = _rmsnorm(x, g_ref[...])
    full[HALO:HALO + tm, :] = xn
    pos = pos0 + i * tm + lax.broadcasted_iota(jnp.int32, (tm, 1), 0)
    y = _pool_groups(
        xn,
        lambda k, ch: full[HALO - k:HALO - k + tm, ch],
        lambda win: jnp.minimum(pos + 1, win).astype(F32),
        w_ref, b_ref[...], sc_ref[...])
    o_ref[...] = x + y

    @pl.when(i == pl.num_programs(1) - 1)
    def _():
        nb_ref[...] = full[tm:tm + HALO, :]


def pool_prompt(x, buf, g, w, b, scale, *, pos0):
    bsz, t, d = x.shape
    nbuf = buf.shape[1]
    tm = _tile(t, 512)
    buf16 = jnp.pad(buf, ((0, 0), (HALO - nbuf, 0), (0, 0)))
    vec = pl.BlockSpec((1, d), lambda b_, i: (0, 0))
    ng = len(POOL_WINDOWS)
    out, nb = pl.pallas_call(
        functools.partial(_pool_prompt_kernel, pos0=pos0),
        grid=(bsz, t // tm),
        in_specs=[
            pl.BlockSpec((None, tm, d), lambda b_, i: (b_, i, 0)),
            pl.BlockSpec((None, HALO, d), lambda b_, i: (b_, 0, 0)),
            vec,
            pl.BlockSpec((ng, d // ng, d // ng), lambda b_, i: (0, 0, 0)),
            vec, vec,
        ],
        out_specs=[
            pl.BlockSpec((None, tm, d), lambda b_, i: (b_, i, 0)),
            pl.BlockSpec((None, HALO, d), lambda b_, i: (b_, 0, 0)),
        ],
        out_shape=[jax.ShapeDtypeStruct((bsz, t, d), F32), jax.ShapeDtypeStruct((bsz, HALO, d), F32)],
        scratch_shapes=[pltpu.VMEM((tm + HALO, d), F32)],
        compiler_params=_params("parallel", "arbitrary"),
        name="pool_prompt",
    )(x, buf16, g.reshape(1, d), w, b.reshape(1, d), scale.reshape(1, d))
    return out, nb[:, HALO - nbuf:, :]


def _pool_step_kernel(x_ref, buf_ref, g_ref, w_ref, b_ref, sc_ref, o_ref, xn_ref, *, pos0):
    x = x_ref[...]
    xn = _rmsnorm(x, g_ref[...])
    nbuf = buf_ref.shape[0]
    y = _pool_groups(
        xn,
        lambda k, ch: buf_ref[nbuf - k, :, ch],
        lambda win: float(min(pos0 + 1, win)),
        w_ref, b_ref[...], sc_ref[...])
    o_ref[...] = x + y
    xn_ref[...] = xn


def pool_step(x, buf, g, w, b, scale, *, pos0):
    bsz, d = x.shape
    nbuf = buf.shape[1]
    ng = len(POOL_WINDOWS)
    buf_t = jnp.swapaxes(buf, 0, 1)
    full = lambda *shape: pl.BlockSpec(shape, lambda i: (0,) * len(shape))
    out, xn = pl.pallas_call(
        functools.partial(_pool_step_kernel, pos0=pos0),
        grid=(1,),
        in_specs=[full(bsz, d), full(nbuf, bsz, d), full(1, d), full(ng, d // ng, d // ng), full(1, d), full(1, d)],
        out_specs=[full(bsz, d), full(bsz, d)],
        out_shape=[jax.ShapeDtypeStruct((bsz, d), F32)] * 2,
        compiler_params=_params("arbitrary"),
        name="pool_step",
    )(x, buf_t, g.reshape(1, d), w, b.reshape(1, d), scale.reshape(1, d))
    return out, jnp.concatenate([buf[:, 1:], xn[:, None, :]], axis=1)


def _lf_cumsum_kernel(fl_ref, b_ref, lf_ref, cum_ref, cumt_ref):
    t, w = fl_ref.shape
    lf_ref[...] = -_softplus(-(fl_ref[...] + b_ref[...]))
    row8 = lax.broadcasted_iota(jnp.int32, (SUBLANES, w), 0)

    def body(r, carry):
        off = pl.multiple_of(r * SUBLANES, SUBLANES)
        x8 = lf_ref[pl.ds(off, SUBLANES), :]
        for s in (1, 2, 4):
            x8 = x8 + jnp.where(row8 >= s, pltpu.roll(x8, s, 0), 0.0)
        c8 = x8 + carry
        cum_ref[pl.ds(off, SUBLANES), :] = c8
        return c8[SUBLANES - 1:SUBLANES, :]

    lax.fori_loop(0, t // SUBLANES, body, jnp.zeros((1, w), F32))
    cumt_ref[...] = cum_ref[...].T


def lf_cumsum(fl, b_pad):
    bsz, t, w = fl.shape
    blk = pl.BlockSpec((None, t, w), lambda b: (b, 0, 0))
    return pl.pallas_call(
        _lf_cumsum_kernel,
        grid=(bsz,),
        in_specs=[blk, pl.BlockSpec((1, w), lambda b: (0, 0))],
        out_specs=[blk, blk, pl.BlockSpec((None, w, t), lambda b: (b, 0, 0))],
        out_shape=[jax.ShapeDtypeStruct((bsz, t, w), F32)] * 2 + [jax.ShapeDtypeStruct((bsz, w, t), F32)],
        compiler_params=_params("parallel"),
        name="fox_lf_cumsum",
    )(fl, b_pad)


def _lf_kernel(fl_ref, b_ref, lf_ref):
    lf_ref[...] = -_softplus(-(fl_ref[...] + b_ref[...]))


def lf_only(fl, b_pad):
    m, w = fl.shape
    return pl.pallas_call(
        _lf_kernel,
        grid=(1,),
        in_specs=[pl.BlockSpec((m, w), lambda i: (0, 0)), pl.BlockSpec((1, w), lambda i: (0, 0))],
        out_specs=pl.BlockSpec((m, w), lambda i: (0, 0)),
        out_shape=jax.ShapeDtypeStruct((m, w), F32),
        name="fox_lf",
    )(fl, b_pad)


FLASH_HEADS = 4


def _fox_flash_kernel(qi_tab, ki_tab, q_ref, k_ref, v_ref, fq_ref, ck_ref, o_ref,
                      q2_ref, kb_ref, vt_ref, m_ref, l_ref, acc_ref, *, tq, hb):
    hg = pl.program_id(1)
    pair = pl.program_id(2)
    qi = qi_tab[pair]
    ki = ki_tab[pair]
    w = q_ref.shape[1]
    dh = w // hb

    @pl.when(ki == 0)
    def _():
        m_ref[...] = jnp.full_like(m_ref, -jnp.inf)
        l_ref[...] = jnp.zeros_like(l_ref)
        acc_ref[...] = jnp.zeros_like(acc_ref)
        q = q_ref[...].astype(F32)
        head_of_lane = lax.broadcasted_iota(jnp.int32, (tq, w), 1) // dh
        for hh in range(hb):
            q2_ref[hh] = jnp.where(head_of_lane == hh, q, 0.0).astype(BF16)

    kb_ref[...] = k_ref[...].astype(BF16)
    vt_ref[...] = v_ref[...].T.astype(BF16)
    ck = ck_ref[...]
    lane = lax.broadcasted_iota(jnp.int32, (tq, LANES), 1)
    fk_cols = [jnp.sum(jnp.where(lane == hg * hb + hh, ck, 0.0), axis=1, keepdims=True) for hh in range(hb)]

    def pair_update(masked):
        scores = [lax.dot_general(kb_ref[...], q2_ref[hh], _NT, preferred_element_type=F32) for hh in range(hb)]
        if masked:
            keep = (lax.broadcasted_iota(jnp.int32, (tq, tq), 1) >= lax.broadcasted_iota(jnp.int32, (tq, tq), 0))
        for hh in range(hb):
            t = scores[hh] - fk_cols[hh]
            if masked:
                t = jnp.where(keep, t, NEG_INF)
            fq = fq_ref[hh:hh + 1, :]
            m_prev = m_ref[hh:hh + 1, :]
            m_new = jnp.maximum(m_prev, jnp.max(t, axis=0, keepdims=True) + fq)
            alpha = jnp.exp(m_prev - m_new)
            p = jnp.exp(t + (fq - m_new))
            l_ref[hh:hh + 1, :] = alpha * l_ref[hh:hh + 1, :] + jnp.sum(p, axis=0, keepdims=True)
            m_ref[hh:hh + 1, :] = m_new
            ch = slice(hh * dh, (hh + 1) * dh)
            pv = jnp.dot(vt_ref[ch, :], p.astype(BF16), preferred_element_type=F32)
            acc_ref[ch, :] = alpha * acc_ref[ch, :] + pv

    @pl.when(ki < qi)
    def _():
        pair_update(False)

    @pl.when(ki == qi)
    def _():
        pair_update(True)
        for hh in range(hb):
            ch = slice(hh * dh, (hh + 1) * dh)
            acc_ref[ch, :] = acc_ref[ch, :] * (1.0 / l_ref[hh:hh + 1, :])
        o_ref[...] = acc_ref[...].T.astype(o_ref.dtype)


def fox_flash(q, k, v, cum, cum_t):
    bsz, t, d = q.shape
    tq = _tile(t, 512)
    hb = FLASH_HEADS
    w = hb * (d // FOX_HEADS)
    nq = t // tq
    pairs = [(qi, ki) for qi in range(nq) for ki in range(qi + 1)]
    qi_tab = jnp.asarray([pq for pq, _ in pairs], jnp.int32)
    ki_tab = jnp.asarray([pk for _, pk in pairs], jnp.int32)
    cum_rows = cum_t[:, :FOX_HEADS, :].reshape(bsz, FOX_HEADS // hb, hb, t)
    q_spec = pl.BlockSpec((None, tq, w), lambda b, hg, p, qt, kt: (b, qt[p], hg))
    kv_spec = pl.BlockSpec((None, tq, w), lambda b, hg, p, qt, kt: (b, kt[p], hg))
    grid_spec = pltpu.PrefetchScalarGridSpec(
        num_scalar_prefetch=2,
        grid=(bsz, d // w, len(pairs)),
        in_specs=[
            q_spec, kv_spec, kv_spec,
            pl.BlockSpec((None, None, hb, tq), lambda b, hg, p, qt, kt: (b, hg, 0, qt[p])),
            pl.BlockSpec((None, tq, LANES), lambda b, hg, p, qt, kt: (b, kt[p], 0)),
        ],
        out_specs=q_spec,
        scratch_shapes=[
            pltpu.VMEM((hb, tq, w), BF16),
            pltpu.VMEM((tq, w), BF16),
            pltpu.VMEM((w, tq), BF16),
            pltpu.VMEM((hb, tq), F32),
            pltpu.VMEM((hb, tq), F32),
            pltpu.VMEM((w, tq), F32),
        ],
    )
    return pl.pallas_call(
        functools.partial(_fox_flash_kernel, tq=tq, hb=hb),
        grid_spec=grid_spec,
        out_shape=jax.ShapeDtypeStruct((bsz, t, d), BF16),
        compiler_params=_params("parallel", "parallel", "arbitrary"),
        name="fox_flash",
    )(qi_tab, ki_tab, q, k, v, cum_rows, cum)


DECODE_PAGES = 8


def _fox_decode_kernel(pt_ref, q_ref, kn_ref, vn_ref, lfn_ref, *refs):
    g = DECODE_PAGES
    k_refs, v_refs, lft_refs = refs[0:g], refs[g:2 * g], refs[2 * g:3 * g]
    o_ref, qm_ref, m_ref, l_ref, acc_ref, carry_ref = refs[3 * g:]
    j = pl.program_id(1)
    d = q_ref.shape[1]
    nh = FOX_HEADS
    dh = d // nh
    page = k_refs[0].shape[1]
    head = lax.broadcasted_iota(jnp.int32, (nh, d), 0)
    own = head == lax.broadcasted_iota(jnp.int32, (nh, d), 1) // dh

    @pl.when(j == 0)
    def _():
        qm_ref[...] = jnp.where(own, jnp.broadcast_to(q_ref[...], (nh, d)), 0.0).astype(BF16)
        m_ref[...] = jnp.full_like(m_ref, -jnp.inf)
        l_ref[...] = jnp.zeros_like(l_ref)
        acc_ref[...] = jnp.zeros_like(acc_ref)
        carry_ref[...] = jnp.zeros_like(carry_ref)

    hl = lax.broadcasted_iota(jnp.int32, (nh, LANES), 0) == lax.broadcasted_iota(jnp.int32, (nh, LANES), 1)
    lf_new = jnp.sum(jnp.where(hl, jnp.broadcast_to(lfn_ref[...], (nh, LANES)), 0.0), axis=1, keepdims=True)
    later = (lax.broadcasted_iota(jnp.int32, (page, page), 0)
             > lax.broadcasted_iota(jnp.int32, (page, page), 1)).astype(F32)

    qm = qm_ref[...]
    scores = [jnp.dot(qm, k_ref[...].astype(BF16), preferred_element_type=F32) for k_ref in k_refs]
    lfts = [lft_ref[...] for lft_ref in lft_refs]
    within = jnp.dot(jnp.concatenate(lfts, axis=0), later, preferred_element_type=F32,
                     precision=lax.Precision.HIGHEST)
    carry = carry_ref[...]
    logits = []
    for i in range(g):
        logits.append(scores[i] + lf_new + (within[i * nh:(i + 1) * nh, :] + carry))
        carry = carry + jnp.sum(lfts[i], axis=1, keepdims=True)
    carry_ref[...] = carry
    s = jnp.concatenate(logits, axis=1)
    m_prev = m_ref[...]
    m_new = jnp.maximum(m_prev, jnp.max(s, axis=1, keepdims=True))
    alpha = jnp.exp(m_prev - m_new)
    p = jnp.exp(s - m_new)
    l_ref[...] = alpha * l_ref[...] + jnp.sum(p, axis=1, keepdims=True)
    m_ref[...] = m_new
    pb = p.astype(BF16)
    pv = lax.dot_general(pb[:, 0:page], v_refs[0][...].astype(BF16), _NT, preferred_element_type=F32)
    for i in range(1, g):
        pv = pv + lax.dot_general(pb[:, i * page:(i + 1) * page], v_refs[i][...].astype(BF16), _NT,
                                  preferred_element_type=F32)
    acc_ref[...] = alpha * acc_ref[...] + pv

    @pl.when(j == pl.num_programs(1) - 1)
    def _():
        kn = jnp.broadcast_to(kn_ref[...].astype(BF16).astype(F32), (nh, d))
        vn = jnp.broadcast_to(vn_ref[...].astype(BF16).astype(F32), (nh, d))
        s_new = jnp.sum(qm_ref[...].astype(F32) * kn, axis=1, keepdims=True)
        m_prev = m_ref[...]
        m_fin = jnp.maximum(m_prev, s_new)
        alpha = jnp.exp(m_prev - m_fin)
        p_new = jnp.exp(s_new - m_fin)
        l_fin = alpha * l_ref[...] + p_new
        acc = alpha * acc_ref[...] + p_new.astype(BF16).astype(F32) * vn
        o_ref[...] = jnp.sum(jnp.where(own, acc * (1.0 / l_fin), 0.0), axis=0, keepdims=True)


def fox_decode(q, k_new, v_new, lf_new, kt_pool, vt_pool, lft_pool, page_table, layer):
    bsz, d = q.shape
    n_pages = page_table.shape[1]
    page = kt_pool.shape[3]
    nh = FOX_HEADS
    g = DECODE_PAGES
    assert n_pages % g == 0
    row = pl.BlockSpec((None, 1, d), lambda b, j, pt: (b, 0, 0))

    def page_idx(i):
        return lambda b, j, pt: (layer, pt[b * n_pages + (n_pages - 1 - (j * g + i))], 0, 0)

    grid_spec = pltpu.PrefetchScalarGridSpec(
        num_scalar_prefetch=1,
        grid=(bsz, n_pages // g),
        in_specs=[row, row, row, pl.BlockSpec((None, 1, LANES), lambda b, j, pt: (b, 0, 0))]
        + [pl.BlockSpec((None, None, d, page), page_idx(i)) for i in range(g)]
        + [pl.BlockSpec((None, None, d, page), page_idx(i)) for i in range(g)]
        + [pl.BlockSpec((None, None, nh, page), page_idx(i)) for i in range(g)],
        out_specs=row,
        scratch_shapes=[
            pltpu.VMEM((nh, d), BF16),
            pltpu.VMEM((nh, 1), F32),
            pltpu.VMEM((nh, 1), F32),
            pltpu.VMEM((nh, d), F32),
            pltpu.VMEM((nh, 1), F32),
        ],
    )
    out = pl.pallas_call(
        _fox_decode_kernel,
        grid_spec=grid_spec,
        out_shape=jax.ShapeDtypeStruct((bsz, 1, d), F32),
        compiler_params=_params("parallel", "arbitrary"),
        name="fox_decode",
    )(page_table.reshape(-1), q.reshape(bsz, 1, d), k_new.reshape(bsz, 1, d), v_new.reshape(bsz, 1, d),
      lf_new.reshape(bsz, 1, LANES), *([kt_pool] * g + [vt_pool] * g + [lft_pool] * g))
    return out.reshape(bsz, d)


def _block_diag_gate_weights(w_rg, w_ig):
    nb, bw, _ = w_rg.shape
    half = nb // 2
    assert (half * bw) % LANES == 0
    eye = jnp.eye(half, dtype=w_rg.dtype)

    def dense(w):
        return (w[:, :, None, :] * eye[:, None, :, None]).reshape(half * bw, half * bw)

    return jnp.stack([
        jnp.concatenate([dense(w_rg[c * half:(c + 1) * half]), dense(w_ig[c * half:(c + 1) * half])], axis=1)
        for c in range(2)]).astype(BF16)


def _trunk(x, bsz, t, pos0, mem_k, mem_v, lru_h, lru_conv, pool_buf, fox_paged, p):
    d = x.shape[1]
    depth = p["norm_mix_g"].shape[0]
    hs, convs, pools, ks, vs, lfs = [], [], [], [], [], []
    y = None
    for layer in range(depth):
        kind, j = layer % N_MIXERS, layer // N_MIXERS
        g_mix = p["norm_mix_g"][layer]
        if kind == 0:
            c = p["w_lru_out"].shape[1]
            args = (p["lru_conv_w"][j], p["lru_conv_b"][j], p["lru_wg"][j], p["lru_b_rg"][j], p["lru_b_ig"][j],
                    p["lru_lambda"][j])
            if t > 1:
                x3, hl, cb = lru_prompt(x.reshape(bsz, t, d), g_mix, p["w_lru_in"][j], lru_conv[j], lru_h[j],
                                        *args, p["w_lru_out"][j], pos0=pos0)
                x = x3.reshape(bsz * t, d)
            else:
                gu = norm_matmul(x, g_mix, p["w_lru_in"][j], tn=c, name="lru_in")
                yl, hl, cb = lru_step(gu, lru_conv[j], lru_h[j], *args, pos0=pos0)
                x = matmul_res(yl, p["w_lru_out"][j], x, name="lru_out")
            hs.append(hl)
            convs.append(cb)
        elif kind == 1:
            dh = d // FOX_HEADS
            q, k, v, fl = fox_proj(x, g_mix, p["w_fox_qkv"][j], p["w_fox_f"][j],
                                   q_dtype=BF16 if fox_paged is None else F32, q_scale=dh ** -0.5)
            if fox_paged is None:
                lf, cum, cum_t = lf_cumsum(fl.reshape(bsz, t, LANES), p["b_fox_f"][j])
                o = fox_flash(q.reshape(bsz, t, d), k.reshape(bsz, t, d), v.reshape(bsz, t, d), cum, cum_t)
                o = o.reshape(bsz * t, d)
                lf = lf[:, :, :FOX_HEADS]
            else:
                k_pool, v_pool, lft_pool, page_table = fox_paged
                lf = lf_only(fl, p["b_fox_f"][j])
                o = fox_decode(q, k, v, lf, k_pool, v_pool, lft_pool, page_table, j)
                lf = lf[:, :FOX_HEADS].reshape(bsz, t, FOX_HEADS)
            ks.append(k.reshape(bsz, t, FOX_HEADS, dh))
            vs.append(v.reshape(bsz, t, FOX_HEADS, dh))
            lfs.append(lf)
            x = matmul_res(o, p["w_fox_o"][j], x, name="fox_out")
        else:
            args = (g_mix, p["w_pool"][j], p["b_pool"][j], p["pool_scale"][j])
            if t > 1:
                x3, pb = pool_prompt(x.reshape(bsz, t, d), pool_buf[j], *args, pos0=pos0)
                x = x3.reshape(bsz * t, d)
            else:
                x, pb = pool_step(x, pool_buf[j], *args, pos0=pos0)
            pools.append(pb)

        if t > 1:
            x = xattn(x.reshape(bsz, t, d), p["norm_x_g"][layer], p["w_xq"][layer], mem_k, mem_v,
                      p["w_xo"][layer], layer).reshape(bsz * t, d)
        else:
            q = norm_matmul(x, p["norm_x_g"][layer], p["w_xq"][layer], tn=d, out_dtype=BF16,
                            scale=(d // XA_HEADS) ** -0.5, name="xattn_q")
            rows = 2 * SUBLANES
            o = mem_attn(jnp.broadcast_to(q[:, None, :], (bsz, rows, d)), mem_k, mem_v, layer)[:, 0, :]
            x = matmul_res(o, p["w_xo"][layer], x, name="xattn_out")

        final_g = p["final_norm_g"] if layer == depth - 1 else None
        x, y = mlp(x, p["norm_mlp_g"][layer], p["w_up"][layer], p["w_down"][layer], final_g)
    return y, hs, convs, pools, ks, vs, lfs


def kernel(x_prompt, x_sample, mem_prompt, cache_fox_k, cache_fox_v, cache_fox_lf, cache_mem_k, cache_mem_v, state_lru_h, state_lru_conv, state_pool, page_table, norm_mix_g, norm_mem_g, norm_x_g, norm_mlp_g, final_norm_g, w_lru_in, lru_conv_w, lru_conv_b, lru_w_rg, lru_b_rg, lru_w_ig, lru_b_ig, lru_lambda, w_lru_out, w_fox_qkvf, b_fox_f, w_fox_o, w_pool, b_pool, pool_scale, w_xq, w_xkv, w_xo, w_up, w_down):
    bsz, seq, d = x_prompt.shape
    dec, dec_seq, _ = x_sample.shape
    assert dec_seq == 1
    depth = norm_mix_g.shape[0]
    n_mem = mem_prompt.shape[1]
    n_fox = w_fox_qkvf.shape[0]
    n_lru = w_lru_in.shape[0]
    n_pool_layers = w_pool.shape[0]
    c = w_lru_out.shape[1]
    dt = x_prompt.dtype

    bias_pad = jnp.pad(b_fox_f, ((0, 0), (0, LANES - FOX_HEADS))).reshape(n_fox, 1, LANES)
    p = dict(
        norm_mix_g=norm_mix_g, norm_x_g=norm_x_g, norm_mlp_g=norm_mlp_g, final_norm_g=final_norm_g,
        w_lru_in=w_lru_in.astype(BF16), lru_conv_w=lru_conv_w, lru_conv_b=lru_conv_b,
        lru_wg=jnp.stack([_block_diag_gate_weights(lru_w_rg[l], lru_w_ig[l]) for l in range(n_lru)]),
        lru_b_rg=lru_b_rg, lru_b_ig=lru_b_ig, lru_lambda=lru_lambda, w_lru_out=w_lru_out.astype(BF16),
        w_fox_qkv=w_fox_qkvf[:, :, :3 * d].astype(BF16),
        w_fox_f=jnp.pad(w_fox_qkvf[:, :, 3 * d:], ((0, 0), (0, 0), (0, LANES - FOX_HEADS))).astype(BF16),
        b_fox_f=bias_pad, w_fox_o=w_fox_o.astype(BF16),
        w_pool=w_pool.astype(BF16), b_pool=b_pool, pool_scale=pool_scale,
        w_xq=w_xq.astype(BF16), w_xo=w_xo.astype(BF16), w_up=w_up.astype(BF16), w_down=w_down.astype(BF16),
    )

    mem_k_p, mem_v_p = mem_kv(mem_prompt.reshape(bsz * n_mem, d), norm_mem_g, w_xkv.astype(BF16))
    mem_k_p = mem_k_p.reshape(depth, bsz, n_mem, d)
    mem_v_p = mem_v_p.reshape(depth, bsz, n_mem, d)
    h0 = jnp.zeros((n_lru, bsz, c), dt)
    c0 = jnp.zeros((n_lru, bsz, CONV_WIDTH - 1, c), dt)
    pb0 = jnp.zeros((n_pool_layers, bsz, max(POOL_WINDOWS) - 1, d), dt)
    y_p, hs_p, convs_p, pools_p, ks_p, vs_p, lfs_p = _trunk(
        x_prompt.reshape(bsz * seq, d), bsz, seq, 0, mem_k_p, mem_v_p, h0, c0, pb0, None, p)

    n_pool_pages, page = cache_fox_k.shape[1], cache_fox_k.shape[2]
    pos_s = page_table.shape[1] * page
    channel_major = lambda c: jnp.transpose(c, (0, 1, 3, 4, 2)).reshape(n_fox, n_pool_pages, d, page)
    fox_paged = (channel_major(cache_fox_k), channel_major(cache_fox_v), jnp.swapaxes(cache_fox_lf, 2, 3), page_table)
    head_major = lambda c: jnp.swapaxes(c, 2, 3)
    y_s, hs_s, convs_s, pools_s, ks_s, vs_s, lfs_s = _trunk(
        x_sample.reshape(dec, d), dec, 1, pos_s, head_major(cache_mem_k), head_major(cache_mem_v),
        state_lru_h, state_lru_conv, state_pool, fox_paged, p)

    xa = (depth, bsz, n_mem, XA_HEADS, d // XA_HEADS)
    return (y_p.reshape(bsz, seq, d), y_s.reshape(dec, 1, d),
            jnp.stack(hs_p), jnp.stack(convs_p), jnp.stack(pools_p), jnp.stack(ks_p), jnp.stack(vs_p),
            jnp.stack(lfs_p), mem_k_p.reshape(xa), mem_v_p.reshape(xa),
            jnp.stack(hs_s), jnp.stack(convs_s), jnp.stack(pools_s), jnp.stack(ks_s), jnp.stack(vs_s),
            jnp.stack(lfs_s))
```

```python
import functools

import jax
import jax.numpy as jnp
from jax import lax
from jax.experimental import pallas as pl
from jax.experimental.pallas import tpu as pltpu

F32 = jnp.float32
BF16 = jnp.bfloat16

RMS_EPS = 1e-6
NEG_INF = -1e30
LRU_C = 8.0
CONV_WIDTH = 4
FOX_HEADS = 16
XA_HEADS = 4
POOL_WINDOWS = (2, 4, 8, 16)
N_MIXERS = 3

LANES = 128
SUBLANES = 8
HALO = 16

_NT = (((1,), (1,)), ((), ()))


def _tile(n, target):
    t = 1
    while t * 2 <= min(n, target):
        t *= 2
    while t > 1 and n % t:
        t //= 2
    return t if (n % t == 0 and t >= SUBLANES) else n


def _params(*sem):
    return pltpu.CompilerParams(dimension_semantics=sem)


def _rmsnorm(x, g):
    x = x.astype(F32)
    x = x * lax.rsqrt(jnp.mean(x * x, axis=-1, keepdims=True) + RMS_EPS)
    return x * g


def _softplus(z):
    return jnp.maximum(z, 0.0) + jnp.log1p(jnp.exp(-jnp.abs(z)))


def _sigmoid(x):
    return 0.5 * jnp.tanh(0.5 * x) + 0.5


def _gelu_tanh(x):
    c = 0.7978845608028654
    return x * (0.5 * (1.0 + jnp.tanh(c * (x + 0.044715 * (x * x * x)))))


def _norm_matmul_kernel(x_ref, g_ref, w_ref, o_ref, xn_ref, *, scale):
    @pl.when(pl.program_id(1) == 0)
    def _():
        xn_ref[...] = _rmsnorm(x_ref[...], g_ref[...]).astype(BF16)

    acc = jnp.dot(xn_ref[...], w_ref[...], preferred_element_type=F32)
    if scale != 1.0:
        acc = acc * scale
    o_ref[...] = acc.astype(o_ref.dtype)


def norm_matmul(x, g, w, *, tn, out_dtype=F32, scale=1.0, name="norm_matmul"):
    m, d = x.shape
    n = w.shape[1]
    tm = _tile(m, 1024)
    return pl.pallas_call(
        functools.partial(_norm_matmul_kernel, scale=scale),
        grid=(m // tm, n // tn),
        in_specs=[
            pl.BlockSpec((tm, d), lambda i, j: (i, 0)),
            pl.BlockSpec((1, d), lambda i, j: (0, 0)),
            pl.BlockSpec((d, tn), lambda i, j: (0, j)),
        ],
        out_specs=pl.BlockSpec((tm, tn), lambda i, j: (i, j)),
        out_shape=jax.ShapeDtypeStruct((m, n), out_dtype),
        scratch_shapes=[pltpu.VMEM((tm, d), BF16)],
        compiler_params=_params("parallel", "arbitrary"),
        name=name,
    )(x, g.reshape(1, d), w)


def _fox_proj_kernel(x_ref, g_ref, w_ref, wf_ref, q_ref, k_ref, v_ref, f_ref, *rest, q_scale, split_heads):
    if split_heads:
        k3_ref, v3_ref, xn_ref = rest
    else:
        (xn_ref,) = rest
    j = pl.program_id(1)

    @pl.when(j == 0)
    def _():
        xn_ref[...] = _rmsnorm(x_ref[...], g_ref[...]).astype(BF16)

    @pl.when(j == 0)
    def _():
        acc = jnp.dot(xn_ref[...], w_ref[...], preferred_element_type=F32)
        q_ref[...] = (acc * q_scale).astype(q_ref.dtype)

    @pl.when(j == 1)
    def _():
        acc = jnp.dot(xn_ref[...], w_ref[...], preferred_element_type=F32)
        k_ref[...] = acc
        if split_heads:
            k3_ref[...] = acc.reshape(k3_ref.shape)

    @pl.when(j == 2)
    def _():
        acc = jnp.dot(xn_ref[...], w_ref[...], preferred_element_type=F32)
        v_ref[...] = acc
        if split_heads:
            v3_ref[...] = acc.reshape(v3_ref.shape)

    @pl.when(j == 3)
    def _():
        f_ref[...] = jnp.dot(xn_ref[...], wf_ref[...], preferred_element_type=F32)


def fox_proj(x, g, w_qkv, w_f, *, q_dtype, q_scale, split_heads):
    m, d = x.shape
    tm = _tile(m, 512 if split_heads else 1024)
    dh = d // FOX_HEADS
    row = lambda i, j: (i, 0)
    out_specs = [pl.BlockSpec((tm, d), row), pl.BlockSpec((tm, d), row), pl.BlockSpec((tm, d), row),
                 pl.BlockSpec((tm, LANES), row)]
    out_shape = [jax.ShapeDtypeStruct((m, d), q_dtype), jax.ShapeDtypeStruct((m, d), F32),
                 jax.ShapeDtypeStruct((m, d), F32), jax.ShapeDtypeStruct((m, LANES), F32)]
    if split_heads:
        out_specs += [pl.BlockSpec((tm, FOX_HEADS, dh), lambda i, j: (i, 0, 0))] * 2
        out_shape += [jax.ShapeDtypeStruct((m, FOX_HEADS, dh), F32)] * 2
    return pl.pallas_call(
        functools.partial(_fox_proj_kernel, q_scale=q_scale, split_heads=split_heads),
        grid=(m // tm, 4),
        in_specs=[
            pl.BlockSpec((tm, d), row),
            pl.BlockSpec((1, d), lambda i, j: (0, 0)),
            pl.BlockSpec((d, d), lambda i, j: (0, jnp.minimum(j, 2))),
            pl.BlockSpec((d, LANES), lambda i, j: (0, 0)),
        ],
        out_specs=out_specs,
        out_shape=out_shape,
        scratch_shapes=[pltpu.VMEM((tm, d), BF16)],
        compiler_params=_params("parallel", "arbitrary"),
        name="fox_proj",
    )(x, g.reshape(1, d), w_qkv, w_f)


def _mem_kv_kernel(x_ref, g_ref, w_ref, k_ref, v_ref, xn_ref):
    j = pl.program_id(2)

    @pl.when(j == 0)
    def _():
        xn_ref[...] = _rmsnorm(x_ref[...], g_ref[...]).astype(BF16)
        k_ref[...] = jnp.dot(xn_ref[...], w_ref[...], preferred_element_type=F32)

    @pl.when(j == 1)
    def _():
        v_ref[...] = jnp.dot(xn_ref[...], w_ref[...], preferred_element_type=F32)


def mem_kv(mem, g_all, w_all):
    m, d = mem.shape
    depth = g_all.shape[0]
    tm = _tile(m, 512)
    out_spec = pl.BlockSpec((None, tm, d), lambda l, i, j: (l, i, 0))
    return pl.pallas_call(
        _mem_kv_kernel,
        grid=(depth, m // tm, 2),
        in_specs=[
            pl.BlockSpec((tm, d), lambda l, i, j: (i, 0)),
            pl.BlockSpec((None, 1, d), lambda l, i, j: (l, 0, 0)),
            pl.BlockSpec((None, d, d), lambda l, i, j: (l, 0, j)),
        ],
        out_specs=[out_spec, out_spec],
        out_shape=[jax.ShapeDtypeStruct((depth, m, d), F32)] * 2,
        scratch_shapes=[pltpu.VMEM((tm, d), BF16)],
        compiler_params=_params("parallel", "parallel", "arbitrary"),
        name="mem_kv",
    )(mem, g_all.reshape(depth, 1, d), w_all)


def _matmul_res_kernel(h_ref, w_ref, r_ref, o_ref):
    o_ref[...] = r_ref[...] + jnp.dot(h_ref[...].astype(BF16), w_ref[...], preferred_element_type=F32)


def matmul_res(h, w, res, *, name="matmul_res"):
    m, k = h.shape
    n = w.shape[1]
    tm = _tile(m, 512)
    return pl.pallas_call(
        _matmul_res_kernel,
        grid=(m // tm,),
        in_specs=[
            pl.BlockSpec((tm, k), lambda i: (i, 0)),
            pl.BlockSpec((k, n), lambda i: (0, 0)),
            pl.BlockSpec((tm, n), lambda i: (i, 0)),
        ],
        out_specs=pl.BlockSpec((tm, n), lambda i: (i, 0)),
        out_shape=jax.ShapeDtypeStruct((m, n), F32),
        compiler_params=_params("parallel"),
        name=name,
    )(h, w, res)


def _mlp_kernel(*refs, final_norm):
    if final_norm:
        x_ref, g_ref, wu_ref, wd_ref, gf_ref, o_ref, y_ref, xn_ref, acc_ref = refs
    else:
        x_ref, g_ref, wu_ref, wd_ref, o_ref, xn_ref, acc_ref = refs
    j = pl.program_id(1)

    @pl.when(j == 0)
    def _():
        xn_ref[...] = _rmsnorm(x_ref[...], g_ref[...]).astype(BF16)
        acc_ref[...] = jnp.zeros_like(acc_ref)

    h = jnp.dot(xn_ref[...], wu_ref[...], preferred_element_type=F32)
    h = jnp.square(jnp.maximum(h, 0.0)).astype(BF16)
    acc_ref[...] += jnp.dot(h, wd_ref[...], preferred_element_type=F32)

    @pl.when(j == pl.num_programs(1) - 1)
    def _():
        out = x_ref[...] + acc_ref[...]
        o_ref[...] = out
        if final_norm:
            y_ref[...] = _rmsnorm(out, gf_ref[...])


def mlp(x, g, w_up, w_down, final_g=None):
    m, d = x.shape
    f = w_up.shape[1]
    tm = _tile(m, 1024)
    tf = _tile(f, 1024)
    row = lambda i, j: (i, 0)
    vec = pl.BlockSpec((1, d), lambda i, j: (0, 0))
    in_specs = [
        pl.BlockSpec((tm, d), row),
        vec,
        pl.BlockSpec((d, tf), lambda i, j: (0, j)),
        pl.BlockSpec((tf, d), lambda i, j: (j, 0)),
    ]
    args = [x, g.reshape(1, d), w_up, w_down]
    out_specs = [pl.BlockSpec((tm, d), row)]
    out_shape = [jax.ShapeDtypeStruct((m, d), F32)]
    if final_g is not None:
        in_specs.append(vec)
        args.append(final_g.reshape(1, d))
        out_specs.append(pl.BlockSpec((tm, d), row))
        out_shape.append(jax.ShapeDtypeStruct((m, d), F32))
    outs = pl.pallas_call(
        functools.partial(_mlp_kernel, final_norm=final_g is not None),
        grid=(m // tm, f // tf),
        in_specs=in_specs,
        out_specs=out_specs,
        out_shape=out_shape,
        scratch_shapes=[pltpu.VMEM((tm, d), BF16), pltpu.VMEM((tm, d), F32)],
        compiler_params=_params("parallel", "arbitrary"),
        name="mlp",
    )(*args)
    return outs if final_g is not None else (outs[0], None)


MEM_ATTN_SEQS = 4


def _mem_attn_kernel(q_ref, k_ref, v_ref, o_ref):
    nb, rows, d = q_ref.shape
    nh = XA_HEADS
    dh = d // nh
    n_rows = k_ref.shape[1] * nh
    head_of_col = lax.broadcasted_iota(jnp.int32, (rows, n_rows), 1) % nh
    pairs = [(n, h) for n in range(nb) for h in range(nh)]
    kf = [k_ref[n].reshape(n_rows, dh).astype(BF16) for n in range(nb)]
    vf = [v_ref[n].reshape(n_rows, dh).astype(BF16) for n in range(nb)]
    scores = [lax.dot_general(q_ref[n, :, h * dh:(h + 1) * dh], kf[n], _NT, preferred_element_type=F32)
              for n, h in pairs]
    probs = []
    for (n, h), s in zip(pairs, scores):
        s = jnp.where(head_of_col == h, s, NEG_INF)
        e = jnp.exp(s - jnp.max(s, axis=1, keepdims=True))
        probs.append((e * (1.0 / jnp.sum(e, axis=1, keepdims=True))).astype(BF16))
    for (n, h), p in zip(pairs, probs):
        o = jnp.dot(p, vf[n], preferred_element_type=F32)
        o_ref[n, :, h * dh:(h + 1) * dh] = o.astype(o_ref.dtype)


def mem_attn(q, k_all, v_all, layer):
    bsz, t, d = q.shape
    nb = MEM_ATTN_SEQS if bsz % MEM_ATTN_SEQS == 0 else 1
    kv_spec = pl.BlockSpec((None, nb) + k_all.shape[2:], lambda b: (layer, b, 0, 0, 0))
    q_spec = pl.BlockSpec((nb, t, d), lambda b: (b, 0, 0))
    return pl.pallas_call(
        _mem_attn_kernel,
        grid=(bsz // nb,),
        in_specs=[q_spec, kv_spec, kv_spec],
        out_specs=q_spec,
        out_shape=jax.ShapeDtypeStruct((bsz, t, d), BF16),
        compiler_params=_params("parallel"),
        name="mem_attn",
    )(q, k_all, v_all)


def _xattn_kernel(x_ref, g_ref, wq_ref, k_ref, v_ref, wo_ref, o_ref, q_scr, oh_scr, *, scale):
    d = x_ref.shape[1]
    dh = d // XA_HEADS
    x = x_ref[...]
    xn = _rmsnorm(x, g_ref[...]).astype(BF16)
    q_scr[...] = (jnp.dot(xn, wq_ref[...], preferred_element_type=F32) * scale).astype(BF16)
    heads = [slice(h * dh, (h + 1) * dh) for h in range(XA_HEADS)]
    scores = [lax.dot_general(q_scr[:, sl], k_ref[:, sl].astype(BF16), _NT, preferred_element_type=F32)
              for sl in heads]
    probs = []
    for s in scores:
        e = jnp.exp(s - jnp.max(s, axis=1, keepdims=True))
        probs.append((e * (1.0 / jnp.sum(e, axis=1, keepdims=True))).astype(BF16))
    for sl, p in zip(heads, probs):
        oh_scr[:, sl] = jnp.dot(p, v_ref[:, sl].astype(BF16), preferred_element_type=F32).astype(BF16)
    o_ref[...] = x + jnp.dot(oh_scr[...], wo_ref[...], preferred_element_type=F32)


def xattn(x, g, w_q, k_all, v_all, w_o, layer):
    bsz, t, d = x.shape
    n_mem = k_all.shape[2]
    tm = _tile(t, 512)
    kv_spec = pl.BlockSpec((None, None, n_mem, d), lambda b, i: (layer, b, 0, 0))
    w_spec = pl.BlockSpec((d, d), lambda b, i: (0, 0))
    x_spec = pl.BlockSpec((None, tm, d), lambda b, i: (b, i, 0))
    return pl.pallas_call(
        functools.partial(_xattn_kernel, scale=(d // XA_HEADS) ** -0.5),
        grid=(bsz, t // tm),
        in_specs=[x_spec, pl.BlockSpec((1, d), lambda b, i: (0, 0)), w_spec, kv_spec, kv_spec, w_spec],
        out_specs=x_spec,
        out_shape=jax.ShapeDtypeStruct((bsz, t, d), F32),
        scratch_shapes=[pltpu.VMEM((tm, d), BF16), pltpu.VMEM((tm, d), BF16)],
        compiler_params=_params("parallel", "parallel"),
        name="xattn",
    )(x, g.reshape(1, d), w_q, k_all, v_all, w_o)


def _lru_gates(uc, wg_ref, brg, big, lam):
    half = uc.shape[1] // 2
    ucb = uc.astype(BF16)
    g0 = jnp.dot(ucb[:, :half], wg_ref[0], preferred_element_type=F32)
    g1 = jnp.dot(ucb[:, half:], wg_ref[1], preferred_element_type=F32)
    rg = jnp.concatenate([g0[:, :half], g1[:, :half]], axis=1) + brg
    ig = jnp.concatenate([g0[:, half:], g1[:, half:]], axis=1) + big
    r = _sigmoid(rg)
    i = _sigmoid(ig)
    log_a = (-LRU_C * r) * _softplus(-lam)
    a = jnp.exp(log_a)
    mult = jnp.sqrt(-jnp.tanh(log_a) * (a * a + 1.0))
    return a, mult, i


def _scan8(a8, x8, row8):
    for s in (1, 2, 4):
        keep = row8 >= s
        xs = jnp.where(keep, pltpu.roll(x8, s, 0), 0.0)
        a_s = jnp.where(keep, pltpu.roll(a8, s, 0), 1.0)
        x8 = x8 + a8 * xs
        a8 = a8 * a_s
    return a8, x8


def _lru_prompt_kernel(x_ref, g_ref, win_ref, buf_ref, h0_ref, cw_ref, cb_ref, wg_ref, brg_ref, big_ref, lam_ref,
                       wout_ref, o_ref, hl_ref, cbuf_ref, ufull, a_scr, x_scr, gate_scr, hc, *, first_at_zero):
    i = pl.program_id(1)
    tt = x_ref.shape[0]
    c = wout_ref.shape[0]
    pad = SUBLANES

    @pl.when(i == 0)
    def _():
        ufull[0:pad, :] = buf_ref[...]
        hc[...] = h0_ref[...]

    @pl.when(i > 0)
    def _():
        ufull[0:pad, :] = ufull[tt:tt + pad, :]

    xn = _rmsnorm(x_ref[...], g_ref[...]).astype(BF16)
    gu = jnp.dot(xn, win_ref[...], preferred_element_type=F32)
    gate_scr[...] = _gelu_tanh(gu[:, :c])
    u = gu[:, c:]
    ufull[pad:pad + tt, :] = u
    uc = cb_ref[...]
    for k in range(CONV_WIDTH - 1):
        off = pad - (CONV_WIDTH - 1) + k
        uc = uc + ufull[off:off + tt, :] * cw_ref[k:k + 1, :]
    uc = uc + u * cw_ref[CONV_WIDTH - 1:CONV_WIDTH, :]

    a, mult, ig = _lru_gates(uc, wg_ref, brg_ref[...], big_ref[...], lam_ref[...])
    if first_at_zero:
        row = lax.broadcasted_iota(jnp.int32, (tt, 1), 0)
        mult = jnp.where(jnp.logical_and(row == 0, i == 0), 1.0, mult)
    xin = (mult * ig) * uc

    row8 = lax.broadcasted_iota(jnp.int32, (SUBLANES, c), 0)
    for grp in range(tt // SUBLANES):
        rows = slice(grp * SUBLANES, (grp + 1) * SUBLANES)
        a8, x8 = _scan8(a[rows, :], xin[rows, :], row8)
        a_scr[rows, :] = a8
        x_scr[rows, :] = x8

    def body(r, h):
        off = pl.multiple_of(r * SUBLANES, SUBLANES)
        h8 = x_scr[pl.ds(off, SUBLANES), :] + a_scr[pl.ds(off, SUBLANES), :] * h
        x_scr[pl.ds(off, SUBLANES), :] = h8
        return h8[SUBLANES - 1:SUBLANES, :]

    h = lax.fori_loop(0, tt // SUBLANES, body, hc[...])
    hc[...] = h
    y = (gate_scr[...] * x_scr[...]).astype(BF16)
    o_ref[...] = x_ref[...] + jnp.dot(y, wout_ref[...], preferred_element_type=F32)

    @pl.when(i == pl.num_programs(1) - 1)
    def _():
        hl_ref[...] = h
        cbuf_ref[...] = ufull[tt:tt + pad, :]


def lru_prompt(x, g, w_in, conv_buf, h0, conv_w, conv_b, wg, b_rg, b_ig, lam, w_out, *, pos0):
    bsz, t, d = x.shape
    c = w_out.shape[0]
    tt = _tile(t, 256)
    pad = SUBLANES
    buf8 = jnp.pad(conv_buf, ((0, 0), (pad - (CONV_WIDTH - 1), 0), (0, 0)))
    vec = pl.BlockSpec((1, c), lambda b, i: (0, 0))
    out, h_last, cbuf = pl.pallas_call(
        functools.partial(_lru_prompt_kernel, first_at_zero=(pos0 == 0)),
        grid=(bsz, t // tt),
        in_specs=[
            pl.BlockSpec((None, tt, d), lambda b, i: (b, i, 0)),
            pl.BlockSpec((1, d), lambda b, i: (0, 0)),
            pl.BlockSpec((d, 2 * c), lambda b, i: (0, 0)),
            pl.BlockSpec((None, pad, c), lambda b, i: (b, 0, 0)),
            pl.BlockSpec((None, 1, c), lambda b, i: (b, 0, 0)),
            pl.BlockSpec((CONV_WIDTH, c), lambda b, i: (0, 0)),
            vec,
            pl.BlockSpec((2, c // 2, c), lambda b, i: (0, 0, 0)),
            vec, vec, vec,
            pl.BlockSpec((c, d), lambda b, i: (0, 0)),
        ],
        out_specs=[
            pl.BlockSpec((None, tt, d), lambda b, i: (b, i, 0)),
            pl.BlockSpec((None, 1, c), lambda b, i: (b, 0, 0)),
            pl.BlockSpec((None, pad, c), lambda b, i: (b, 0, 0)),
        ],
        out_shape=[
            jax.ShapeDtypeStruct((bsz, t, d), F32),
            jax.ShapeDtypeStruct((bsz, 1, c), F32),
            jax.ShapeDtypeStruct((bsz, pad, c), F32),
        ],
        scratch_shapes=[
            pltpu.VMEM((tt + pad, c), F32),
            pltpu.VMEM((tt, c), F32),
            pltpu.VMEM((tt, c), F32),
            pltpu.VMEM((tt, c), F32),
            pltpu.VMEM((1, c), F32),
        ],
        compiler_params=_params("parallel", "arbitrary"),
        name="lru_prompt",
    )(x, g.reshape(1, d), w_in, buf8, h0.reshape(bsz, 1, c), conv_w, conv_b.reshape(1, c), wg,
      b_rg.reshape(1, c), b_ig.reshape(1, c), lam.reshape(1, c), w_out)
    return out, h_last.reshape(bsz, c), cbuf[:, pad - (CONV_WIDTH - 1):, :]


def _lru_step_kernel(gate_ref, u_ref, buf_ref, h0_ref, cw_ref, cb_ref, wg_ref, brg_ref, big_ref, lam_ref,
                     y_ref, h_ref, *, first_at_zero):
    u = u_ref[...]
    uc = cb_ref[...]
    for k in range(CONV_WIDTH - 1):
        uc = uc + buf_ref[k] * cw_ref[k:k + 1, :]
    uc = uc + u * cw_ref[CONV_WIDTH - 1:CONV_WIDTH, :]
    a, mult, ig = _lru_gates(uc, wg_ref, brg_ref[...], big_ref[...], lam_ref[...])
    if first_at_zero:
        mult = jnp.ones_like(mult)
    h = a * h0_ref[...] + (mult * ig) * uc
    h_ref[...] = h
    y_ref[...] = (_gelu_tanh(gate_ref[...]) * h).astype(y_ref.dtype)


def lru_step(gu, conv_buf, h0, conv_w, conv_b, wg, b_rg, b_ig, lam, *, pos0):
    bsz, c2 = gu.shape
    c = c2 // 2
    buf_t = jnp.swapaxes(conv_buf, 0, 1)
    full = lambda *shape: pl.BlockSpec(shape, lambda i: (0,) * len(shape))
    y, h = pl.pallas_call(
        functools.partial(_lru_step_kernel, first_at_zero=(pos0 == 0)),
        grid=(1,),
        in_specs=[
            pl.BlockSpec((bsz, c), lambda i: (0, 0)),
            pl.BlockSpec((bsz, c), lambda i: (0, 1)),
            full(CONV_WIDTH - 1, bsz, c),
            full(bsz, c),
            full(CONV_WIDTH, c),
            full(1, c),
            full(2, c // 2, c),
            full(1, c), full(1, c), full(1, c),
        ],
        out_specs=[full(bsz, c), full(bsz, c)],
        out_shape=[jax.ShapeDtypeStruct((bsz, c), BF16), jax.ShapeDtypeStruct((bsz, c), F32)],
        compiler_params=_params("arbitrary"),
        name="lru_step",
    )(gu, gu, buf_t, h0, conv_w, conv_b.reshape(1, c), wg,
      b_rg.reshape(1, c), b_ig.reshape(1, c), lam.reshape(1, c))
    new_buf = jnp.concatenate([conv_buf[:, 1:], gu[:, None, c:]], axis=1)
    return y, h, new_buf


def _pool_groups(xn, shifted, cnt, w_ref, b, scale):
    d = xn.shape[1]
    gw = d // len(POOL_WINDOWS)
    ys = []
    for gi, win in enumerate(POOL_WINDOWS):
        ch = slice(gi * gw, (gi + 1) * gw)
        s = xn[:, ch]
        for k in range(1, win):
            s = s + shifted(k, ch)
        dd = (s / cnt(win) - xn[:, ch]).astype(BF16)
        ys.append(jnp.dot(dd, w_ref[gi], preferred_element_type=F32))
    return (jnp.concatenate(ys, axis=1) + b) * scale


def _pool_prompt_kernel(x_ref, buf_ref, g_ref, w_ref, b_ref, sc_ref, o_ref, nb_ref, full, *, pos0):
    i = pl.program_id(1)
    tm, d = x_ref.shape

    @pl.when(i == 0)
    def _():
        full[0:HALO, :] = buf_ref[...]

    @pl.when(i > 0)
    def _():
        full[0:HALO, :] = full[tm:tm + HALO, :]

    x = x_ref[...]
    xn = _rmsnorm(x, g_ref[...])
    full[HALO:HALO + tm, :] = xn
    pos = pos0 + i * tm + lax.broadcasted_iota(jnp.int32, (tm, 1), 0)
    y = _pool_groups(
        xn,
        lambda k, ch: full[HALO - k:HALO - k + tm, ch],
        lambda win: jnp.minimum(pos + 1, win).astype(F32),
        w_ref, b_ref[...], sc_ref[...])
    o_ref[...] = x + y

    @pl.when(i == pl.num_programs(1) - 1)
    def _():
        nb_ref[...] = full[tm:tm + HALO, :]


def pool_prompt(x, buf, g, w, b, scale, *, pos0):
    bsz, t, d = x.shape
    nbuf = buf.shape[1]
    tm = _tile(t, 512)
    buf16 = jnp.pad(buf, ((0, 0), (HALO - nbuf, 0), (0, 0)))
    vec = pl.BlockSpec((1, d), lambda b_, i: (0, 0))
    ng = len(POOL_WINDOWS)
    out, nb = pl.pallas_call(
        functools.partial(_pool_prompt_kernel, pos0=pos0),
        grid=(bsz, t // tm),
        in_specs=[
            pl.BlockSpec((None, tm, d), lambda b_, i: (b_, i, 0)),
            pl.BlockSpec((None, HALO, d), lambda b_, i: (b_, 0, 0)),
            vec,
            pl.BlockSpec((ng, d // ng, d // ng), lambda b_, i: (0, 0, 0)),
            vec, vec,
        ],
        out_specs=[
            pl.BlockSpec((None, tm, d), lambda b_, i: (b_, i, 0)),
            pl.BlockSpec((None, HALO, d), lambda b_, i: (b_, 0, 0)),
        ],
        out_shape=[jax.ShapeDtypeStruct((bsz, t, d), F32), jax.ShapeDtypeStruct((bsz, HALO, d), F32)],
        scratch_shapes=[pltpu.VMEM((tm + HALO, d), F32)],
        compiler_params=_params("parallel", "arbitrary"),
        name="pool_prompt",
    )(x, buf16, g.reshape(1, d), w, b.reshape(1, d), scale.reshape(1, d))
    return out, nb[:, HALO - nbuf:, :]


def _pool_step_kernel(x_ref, buf_ref, g_ref, w_ref, b_ref, sc_ref, o_ref, xn_ref, *, pos0):
    x = x_ref[...]
    xn = _rmsnorm(x, g_ref[...])
    nbuf = buf_ref.shape[0]
    y = _pool_groups(
        xn,
        lambda k, ch: buf_ref[nbuf - k, :, ch],
        lambda win: float(min(pos0 + 1, win)),
        w_ref, b_ref[...], sc_ref[...])
    o_ref[...] = x + y
    xn_ref[...] = xn


def pool_step(x, buf, g, w, b, scale, *, pos0):
    bsz, d = x.shape
    nbuf = buf.shape[1]
    ng = len(POOL_WINDOWS)
    buf_t = jnp.swapaxes(buf, 0, 1)
    full = lambda *shape: pl.BlockSpec(shape, lambda i: (0,) * len(shape))
    out, xn = pl.pallas_call(
        functools.partial(_pool_step_kernel, pos0=pos0),
        grid=(1,),
        in_specs=[full(bsz, d), full(nbuf, bsz, d), full(1, d), full(ng, d // ng, d // ng), full(1, d), full(1, d)],
        out_specs=[full(bsz, d), full(bsz, d)],
        out_shape=[jax.ShapeDtypeStruct((bsz, d), F32)] * 2,
        compiler_params=_params("arbitrary"),
        name="pool_step",
    )(x, buf_t, g.reshape(1, d), w, b.reshape(1, d), scale.reshape(1, d))
    return out, jnp.concatenate([buf[:, 1:], xn[:, None, :]], axis=1)


def _lf_cumsum_kernel(fl_ref, b_ref, lf_ref, cum_ref, cumt_ref):
    t, w = fl_ref.shape
    lf_ref[...] = -_softplus(-(fl_ref[...] + b_ref[...]))
    row8 = lax.broadcasted_iota(jnp.int32, (SUBLANES, w), 0)

    def body(r, carry):
        off = pl.multiple_of(r * SUBLANES, SUBLANES)
        x8 = lf_ref[pl.ds(off, SUBLANES), :]
        for s in (1, 2, 4):
            x8 = x8 + jnp.where(row8 >= s, pltpu.roll(x8, s, 0), 0.0)
        c8 = x8 + carry
        cum_ref[pl.ds(off, SUBLANES), :] = c8
        return c8[SUBLANES - 1:SUBLANES, :]

    lax.fori_loop(0, t // SUBLANES, body, jnp.zeros((1, w), F32))
    cumt_ref[...] = cum_ref[...].T


def lf_cumsum(fl, b_pad):
    bsz, t, w = fl.shape
    blk = pl.BlockSpec((None, t, w), lambda b: (b, 0, 0))
    return pl.pallas_call(
        _lf_cumsum_kernel,
        grid=(bsz,),
        in_specs=[blk, pl.BlockSpec((1, w), lambda b: (0, 0))],
        out_specs=[blk, blk, pl.BlockSpec((None, w, t), lambda b: (b, 0, 0))],
        out_shape=[jax.ShapeDtypeStruct((bsz, t, w), F32)] * 2 + [jax.ShapeDtypeStruct((bsz, w, t), F32)],
        compiler_params=_params("parallel"),
        name="fox_lf_cumsum",
    )(fl, b_pad)


def _lf_kernel(fl_ref, b_ref, lf_ref):
    lf_ref[...] = -_softplus(-(fl_ref[...] + b_ref[...]))


def lf_only(fl, b_pad):
    m, w = fl.shape
    return pl.pallas_call(
        _lf_kernel,
        grid=(1,),
        in_specs=[pl.BlockSpec((m, w), lambda i: (0, 0)), pl.BlockSpec((1, w), lambda i: (0, 0))],
        out_specs=pl.BlockSpec((m, w), lambda i: (0, 0)),
        out_shape=jax.ShapeDtypeStruct((m, w), F32),
        name="fox_lf",
    )(fl, b_pad)


FLASH_HEADS = 4


def _fox_flash_kernel(qi_tab, ki_tab, q_ref, k_ref, v_ref, fq_ref, ck_ref, o_ref,
                      q2_ref, kb_ref, vt_ref, m_ref, l_ref, acc_ref, *, tq, hb):
    hg = pl.program_id(1)
    pair = pl.program_id(2)
    qi = qi_tab[pair]
    ki = ki_tab[pair]
    w = q_ref.shape[1]
    dh = w // hb

    @pl.when(ki == 0)
    def _():
        m_ref[...] = jnp.full_like(m_ref, -jnp.inf)
        l_ref[...] = jnp.zeros_like(l_ref)
        acc_ref[...] = jnp.zeros_like(acc_ref)
        q = q_ref[...].astype(F32)
        head_of_lane = lax.broadcasted_iota(jnp.int32, (tq, w), 1) // dh
        for hh in range(hb):
            q2_ref[hh] = jnp.where(head_of_lane == hh, q, 0.0).astype(BF16)

    kb_ref[...] = k_ref[...].astype(BF16)
    vt_ref[...] = v_ref[...].T.astype(BF16)
    ck = pltpu.roll(ck_ref[...], (LANES - hg * hb) % LANES, 1)
    fk_cols = [ck[:, hh:hh + 1] for hh in range(hb)]

    def pair_update(masked):
        scores = [lax.dot_general(kb_ref[...], q2_ref[hh], _NT, preferred_element_type=F32) for hh in range(hb)]
        if masked:
            keep = (lax.broadcasted_iota(jnp.int32, (tq, tq), 1) >= lax.broadcasted_iota(jnp.int32, (tq, tq), 0))
        for hh in range(hb):
            t = scores[hh] - fk_cols[hh]
            if masked:
                t = jnp.where(keep, t, NEG_INF)
            fq = fq_ref[hh:hh + 1, :]
            m_prev = m_ref[hh:hh + 1, :]
            m_new = jnp.maximum(m_prev, jnp.max(t, axis=0, keepdims=True) + fq)
            alpha = jnp.exp(m_prev - m_new)
            p = jnp.exp(t + (fq - m_new))
            l_ref[hh:hh + 1, :] = alpha * l_ref[hh:hh + 1, :] + jnp.sum(p, axis=0, keepdims=True)
            m_ref[hh:hh + 1, :] = m_new
            ch = slice(hh * dh, (hh + 1) * dh)
            pv = jnp.dot(vt_ref[ch, :], p.astype(BF16), preferred_element_type=F32)
            acc_ref[ch, :] = alpha * acc_ref[ch, :] + pv

    @pl.when(ki < qi)
    def _():
        pair_update(False)

    @pl.when(ki == qi)
    def _():
        pair_update(True)
        for hh in range(hb):
            ch = slice(hh * dh, (hh + 1) * dh)
            acc_ref[ch, :] = acc_ref[ch, :] * (1.0 / l_ref[hh:hh + 1, :])
        o_ref[...] = acc_ref[...].T.astype(o_ref.dtype)


def fox_flash(q, k, v, cum, cum_t):
    bsz, t, d = q.shape
    tq = _tile(t, 512)
    hb = FLASH_HEADS
    w = hb * (d // FOX_HEADS)
    nq = t // tq
    pairs = [(qi, ki) for qi in range(nq) for ki in range(qi + 1)]
    qi_tab = jnp.asarray([pq for pq, _ in pairs], jnp.int32)
    ki_tab = jnp.asarray([pk for _, pk in pairs], jnp.int32)
    cum_rows = cum_t[:, :FOX_HEADS, :].reshape(bsz, FOX_HEADS // hb, hb, t)
    q_spec = pl.BlockSpec((None, tq, w), lambda b, hg, p, qt, kt: (b, qt[p], hg))
    kv_spec = pl.BlockSpec((None, tq, w), lambda b, hg, p, qt, kt: (b, kt[p], hg))
    grid_spec = pltpu.PrefetchScalarGridSpec(
        num_scalar_prefetch=2,
        grid=(bsz, d // w, len(pairs)),
        in_specs=[
            q_spec, kv_spec, kv_spec,
            pl.BlockSpec((None, None, hb, tq), lambda b, hg, p, qt, kt: (b, hg, 0, qt[p])),
            pl.BlockSpec((None, tq, LANES), lambda b, hg, p, qt, kt: (b, kt[p], 0)),
        ],
        out_specs=q_spec,
        scratch_shapes=[
            pltpu.VMEM((hb, tq, w), BF16),
            pltpu.VMEM((tq, w), BF16),
            pltpu.VMEM((w, tq), BF16),
            pltpu.VMEM((hb, tq), F32),
            pltpu.VMEM((hb, tq), F32),
            pltpu.VMEM((w, tq), F32),
        ],
    )
    return pl.pallas_call(
        functools.partial(_fox_flash_kernel, tq=tq, hb=hb),
        grid_spec=grid_spec,
        out_shape=jax.ShapeDtypeStruct((bsz, t, d), BF16),
        compiler_params=_params("parallel", "parallel", "arbitrary"),
        name="fox_flash",
    )(qi_tab, ki_tab, q, k, v, cum_rows, cum)


DECODE_PAGES = 8


def _fox_decode_kernel(pt_ref, q_ref, kn_ref, vn_ref, lfn_ref, *refs):
    g = DECODE_PAGES
    k_refs, v_refs, lft_refs = refs[0:g], refs[g:2 * g], refs[2 * g:3 * g]
    o_ref, qm_ref, m_ref, l_ref, acc_ref, carry_ref = refs[3 * g:]
    j = pl.program_id(1)
    d = q_ref.shape[1]
    nh = FOX_HEADS
    dh = d // nh
    page = k_refs[0].shape[1]
    head = lax.broadcasted_iota(jnp.int32, (nh, d), 0)
    own = head == lax.broadcasted_iota(jnp.int32, (nh, d), 1) // dh

    @pl.when(j == 0)
    def _():
        qm_ref[...] = jnp.where(own, jnp.broadcast_to(q_ref[...], (nh, d)), 0.0).astype(BF16)
        m_ref[...] = jnp.full_like(m_ref, -jnp.inf)
        l_ref[...] = jnp.zeros_like(l_ref)
        acc_ref[...] = jnp.zeros_like(acc_ref)
        carry_ref[...] = jnp.zeros_like(carry_ref)

    hl = lax.broadcasted_iota(jnp.int32, (nh, LANES), 0) == lax.broadcasted_iota(jnp.int32, (nh, LANES), 1)
    lf_new = jnp.sum(jnp.where(hl, jnp.broadcast_to(lfn_ref[...], (nh, LANES)), 0.0), axis=1, keepdims=True)
    later = (lax.broadcasted_iota(jnp.int32, (page, page), 0)
             > lax.broadcasted_iota(jnp.int32, (page, page), 1)).astype(F32)

    qm = qm_ref[...]
    scores = [jnp.dot(qm, k_ref[...].astype(BF16), preferred_element_type=F32) for k_ref in k_refs]
    lfts = [lft_ref[...] for lft_ref in lft_refs]
    within = jnp.dot(jnp.concatenate(lfts, axis=0), later, preferred_element_type=F32,
                     precision=lax.Precision.HIGHEST)
    carry = carry_ref[...]
    logits = []
    for i in range(g):
        logits.append(scores[i] + lf_new + (within[i * nh:(i + 1) * nh, :] + carry))
        carry = carry + jnp.sum(lfts[i], axis=1, keepdims=True)
    carry_ref[...] = carry
    s = jnp.concatenate(logits, axis=1)
    m_prev = m_ref[...]
    m_new = jnp.maximum(m_prev, jnp.max(s, axis=1, keepdims=True))
    alpha = jnp.exp(m_prev - m_new)
    p = jnp.exp(s - m_new)
    l_ref[...] = alpha * l_ref[...] + jnp.sum(p, axis=1, keepdims=True)
    m_ref[...] = m_new
    pb = p.astype(BF16)
    pv = lax.dot_general(pb[:, 0:page], v_refs[0][...].astype(BF16), _NT, preferred_element_type=F32)
    for i in range(1, g):
        pv = pv + lax.dot_general(pb[:, i * page:(i + 1) * page], v_refs[i][...].astype(BF16), _NT,
                                  preferred_element_type=F32)
    acc_ref[...] = alpha * acc_ref[...] + pv

    @pl.when(j == pl.num_programs(1) - 1)
    def _():
        kn = jnp.broadcast_to(kn_ref[...].astype(BF16).astype(F32), (nh, d))
        vn = jnp.broadcast_to(vn_ref[...].astype(BF16).astype(F32), (nh, d))
        s_new = jnp.sum(qm_ref[...].astype(F32) * kn, axis=1, keepdims=True)
        m_prev = m_ref[...]
        m_fin = jnp.maximum(m_prev, s_new)
        alpha = jnp.exp(m_prev - m_fin)
        p_new = jnp.exp(s_new - m_fin)
        l_fin = alpha * l_ref[...] + p_new
        acc = alpha * acc_ref[...] + p_new.astype(BF16).astype(F32) * vn
        o_ref[...] = jnp.sum(jnp.where(own, acc * (1.0 / l_fin), 0.0), axis=0, keepdims=True)


def fox_decode(q, k_new, v_new, lf_new, kt_pool, vt_pool, lft_pool, page_table, layer):
    bsz, d = q.shape
    n_pages = page_table.shape[1]
    page = kt_pool.shape[3]
    nh = FOX_HEADS
    g = DECODE_PAGES
    assert n_pages % g == 0
    row = pl.BlockSpec((None, 1, d), lambda b, j, pt: (b, 0, 0))

    def page_idx(i):
        return lambda b, j, pt: (layer, pt[b * n_pages + (n_pages - 1 - (j * g + i))], 0, 0)

    grid_spec = pltpu.PrefetchScalarGridSpec(
        num_scalar_prefetch=1,
        grid=(bsz, n_pages // g),
        in_specs=[row, row, row, pl.BlockSpec((None, 1, LANES), lambda b, j, pt: (b, 0, 0))]
        + [pl.BlockSpec((None, None, d, page), page_idx(i)) for i in range(g)]
        + [pl.BlockSpec((None, None, d, page), page_idx(i)) for i in range(g)]
        + [pl.BlockSpec((None, None, nh, page), page_idx(i)) for i in range(g)],
        out_specs=row,
        scratch_shapes=[
            pltpu.VMEM((nh, d), BF16),
            pltpu.VMEM((nh, 1), F32),
            pltpu.VMEM((nh, 1), F32),
            pltpu.VMEM((nh, d), F32),
            pltpu.VMEM((nh, 1), F32),
        ],
    )
    out = pl.pallas_call(
        _fox_decode_kernel,
        grid_spec=grid_spec,
        out_shape=jax.ShapeDtypeStruct((bsz, 1, d), F32),
        compiler_params=_params("parallel", "arbitrary"),
        name="fox_decode",
    )(page_table.reshape(-1), q.reshape(bsz, 1, d), k_new.reshape(bsz, 1, d), v_new.reshape(bsz, 1, d),
      lf_new.reshape(bsz, 1, LANES), *([kt_pool] * g + [vt_pool] * g + [lft_pool] * g))
    return out.reshape(bsz, d)


def _block_diag_gate_weights(w_rg, w_ig):
    nb, bw, _ = w_rg.shape
    half = nb // 2
    assert (half * bw) % LANES == 0
    eye = jnp.eye(half, dtype=w_rg.dtype)

    def dense(w):
        return (w[:, :, None, :] * eye[:, None, :, None]).reshape(half * bw, half * bw)

    return jnp.stack([
        jnp.concatenate([dense(w_rg[c * half:(c + 1) * half]), dense(w_ig[c * half:(c + 1) * half])], axis=1)
        for c in range(2)]).astype(BF16)


def _trunk(x, bsz, t, pos0, mem_k, mem_v, lru_h, lru_conv, pool_buf, fox_paged, p):
    d = x.shape[1]
    depth = p["norm_mix_g"].shape[0]
    hs, convs, pools, ks, vs, lfs = [], [], [], [], [], []
    y = None
    for layer in range(depth):
        kind, j = layer % N_MIXERS, layer // N_MIXERS
        g_mix = p["norm_mix_g"][layer]
        if kind == 0:
            c = p["w_lru_out"].shape[1]
            args = (p["lru_conv_w"][j], p["lru_conv_b"][j], p["lru_wg"][j], p["lru_b_rg"][j], p["lru_b_ig"][j],
                    p["lru_lambda"][j])
            if t > 1:
                x3, hl, cb = lru_prompt(x.reshape(bsz, t, d), g_mix, p["w_lru_in"][j], lru_conv[j], lru_h[j],
                                        *args, p["w_lru_out"][j], pos0=pos0)
                x = x3.reshape(bsz * t, d)
            else:
                gu = norm_matmul(x, g_mix, p["w_lru_in"][j], tn=c, name="lru_in")
                yl, hl, cb = lru_step(gu, lru_conv[j], lru_h[j], *args, pos0=pos0)
                x = matmul_res(yl, p["w_lru_out"][j], x, name="lru_out")
            hs.append(hl)
            convs.append(cb)
        elif kind == 1:
            dh = d // FOX_HEADS
            if fox_paged is None:
                q, k, v, fl, k3, v3 = fox_proj(x, g_mix, p["w_fox_qkv"][j], p["w_fox_f"][j], q_dtype=BF16,
                                               q_scale=dh ** -0.5, split_heads=True)
                lf, cum, cum_t = lf_cumsum(fl.reshape(bsz, t, LANES), p["b_fox_f"][j])
                o = fox_flash(q.reshape(bsz, t, d), k.reshape(bsz, t, d), v.reshape(bsz, t, d), cum, cum_t)
                o = o.reshape(bsz * t, d)
                lf = lf[:, :, :FOX_HEADS]
            else:
                q, k3, v3, fl = fox_proj(x, g_mix, p["w_fox_qkv"][j], p["w_fox_f"][j], q_dtype=F32,
                                         q_scale=dh ** -0.5, split_heads=False)
                k_pool, v_pool, lft_pool, page_table = fox_paged
                lf = lf_only(fl, p["b_fox_f"][j])
                o = fox_decode(q, k3, v3, lf, k_pool, v_pool, lft_pool, page_table, j)
                lf = lf[:, :FOX_HEADS].reshape(bsz, t, FOX_HEADS)
            ks.append(k3.reshape(bsz, t, FOX_HEADS, dh))
            vs.append(v3.reshape(bsz, t, FOX_HEADS, dh))
            lfs.append(lf)
            x = matmul_res(o, p["w_fox_o"][j], x, name="fox_out")
        else:
            args = (g_mix, p["w_pool"][j], p["b_pool"][j], p["pool_scale"][j])
            if t > 1:
                x3, pb = pool_prompt(x.reshape(bsz, t, d), pool_buf[j], *args, pos0=pos0)
                x = x3.reshape(bsz * t, d)
            else:
                x, pb = pool_step(x, pool_buf[j], *args, pos0=pos0)
            pools.append(pb)

        if t > 1:
            x = xattn(x.reshape(bsz, t, d), p["norm_x_g"][layer], p["w_xq"][layer], mem_k, mem_v,
                      p["w_xo"][layer], layer).reshape(bsz * t, d)
        else:
            q = norm_matmul(x, p["norm_x_g"][layer], p["w_xq"][layer], tn=d, out_dtype=BF16,
                            scale=(d // XA_HEADS) ** -0.5, name="xattn_q")
            rows = 2 * SUBLANES
            o = mem_attn(jnp.broadcast_to(q[:, None, :], (bsz, rows, d)), mem_k, mem_v, layer)[:, 0, :]
            x = matmul_res(o, p["w_xo"][layer], x, name="xattn_out")

        final_g = p["final_norm_g"] if layer == depth - 1 else None
        x, y = mlp(x, p["norm_mlp_g"][layer], p["w_up"][layer], p["w_down"][layer], final_g)
    return y, hs, convs, pools, ks, vs, lfs


def kernel(x_prompt, x_sample, mem_prompt, cache_fox_k, cache_fox_v, cache_fox_lf, cache_mem_k, cache_mem_v, state_lru_h, state_lru_conv, state_pool, page_table, norm_mix_g, norm_mem_g, norm_x_g, norm_mlp_g, final_norm_g, w_lru_in, lru_conv_w, lru_conv_b, lru_w_rg, lru_b_rg, lru_w_ig, lru_b_ig, lru_lambda, w_lru_out, w_fox_qkvf, b_fox_f, w_fox_o, w_pool, b_pool, pool_scale, w_xq, w_xkv, w_xo, w_up, w_down):
    bsz, seq, d = x_prompt.shape
    dec, dec_seq, _ = x_sample.shape
    assert dec_seq == 1
    depth = norm_mix_g.shape[0]
    n_mem = mem_prompt.shape[1]
    n_fox = w_fox_qkvf.shape[0]
    n_lru = w_lru_in.shape[0]
    n_pool_layers = w_pool.shape[0]
    c = w_lru_out.shape[1]
    dt = x_prompt.dtype

    bias_pad = jnp.pad(b_fox_f, ((0, 0), (0, LANES - FOX_HEADS))).reshape(n_fox, 1, LANES)
    p = dict(
        norm_mix_g=norm_mix_g, norm_x_g=norm_x_g, norm_mlp_g=norm_mlp_g, final_norm_g=final_norm_g,
        w_lru_in=w_lru_in.astype(BF16), lru_conv_w=lru_conv_w, lru_conv_b=lru_conv_b,
        lru_wg=jnp.stack([_block_diag_gate_weights(lru_w_rg[l], lru_w_ig[l]) for l in range(n_lru)]),
        lru_b_rg=lru_b_rg, lru_b_ig=lru_b_ig, lru_lambda=lru_lambda, w_lru_out=w_lru_out.astype(BF16),
        w_fox_qkv=w_fox_qkvf[:, :, :3 * d].astype(BF16),
        w_fox_f=jnp.pad(w_fox_qkvf[:, :, 3 * d:], ((0, 0), (0, 0), (0, LANES - FOX_HEADS))).astype(BF16),
        b_fox_f=bias_pad, w_fox_o=w_fox_o.astype(BF16),
        w_pool=w_pool.astype(BF16), b_pool=b_pool, pool_scale=pool_scale,
        w_xq=w_xq.astype(BF16), w_xo=w_xo.astype(BF16), w_up=w_up.astype(BF16), w_down=w_down.astype(BF16),
    )

    mem_k_p, mem_v_p = mem_kv(mem_prompt.reshape(bsz * n_mem, d), norm_mem_g, w_xkv.astype(BF16))
    mem_k_p = mem_k_p.reshape(depth, bsz, n_mem, d)
    mem_v_p = mem_v_p.reshape(depth, bsz, n_mem, d)
    h0 = jnp.zeros((n_lru, bsz, c), dt)
    c0 = jnp.zeros((n_lru, bsz, CONV_WIDTH - 1, c), dt)
    pb0 = jnp.zeros((n_pool_layers, bsz, max(POOL_WINDOWS) - 1, d), dt)
    y_p, hs_p, convs_p, pools_p, ks_p, vs_p, lfs_p = _trunk(
        x_prompt.reshape(bsz * seq, d), bsz, seq, 0, mem_k_p, mem_v_p, h0, c0, pb0, None, p)

    n_pool_pages, page = cache_fox_k.shape[1], cache_fox_k.shape[2]
    pos_s = page_table.shape[1] * page
    channel_major = lambda c: jnp.transpose(c, (0, 1, 3, 4, 2)).reshape(n_fox, n_pool_pages, d, page)
    fox_paged = (channel_major(cache_fox_k), channel_major(cache_fox_v), jnp.swapaxes(cache_fox_lf, 2, 3), page_table)
    y_s, hs_s, convs_s, pools_s, ks_s, vs_s, lfs_s = _trunk(
        x_sample.reshape(dec, d), dec, 1, pos_s, cache_mem_k, cache_mem_v,
        state_lru_h, state_lru_conv, state_pool, fox_paged, p)

    xa = (depth, bsz, n_mem, XA_HEADS, d // XA_HEADS)
    return (y_p.reshape(bsz, seq, d), y_s.reshape(dec, 1, d),
            jnp.stack(hs_p), jnp.stack(convs_p), jnp.stack(pools_p), jnp.stack(ks_p), jnp.stack(vs_p),
            jnp.stack(lfs_p), mem_k_p.reshape(xa), mem_v_p.reshape(xa),
            jnp.stack(hs_s), jnp.stack(convs_s), jnp.stack(pools_s), jnp.stack(ks_s), jnp.stack(vs_s),
            jnp.stack(lfs_s))
```

```python
import functools

import jax
import jax.numpy as jnp
from jax import lax
from jax.experimental import pallas as pl
from jax.experimental.pallas import tpu as pltpu

F32 = jnp.float32
BF16 = jnp.bfloat16

RMS_EPS = 1e-6
NEG_INF = -1e30
LRU_C = 8.0
CONV_WIDTH = 4
FOX_HEADS = 16
XA_HEADS = 4
POOL_WINDOWS = (2, 4, 8, 16)
N_MIXERS = 3

LANES = 128
SUBLANES = 8
HALO = 16

_NT = (((1,), (1,)), ((), ()))


def _tile(n, target):
    t = 1
    while t * 2 <= min(n, target):
        t *= 2
    while t > 1 and n % t:
        t //= 2
    return t if (n % t == 0 and t >= SUBLANES) else n


def _params(*sem):
    return pltpu.CompilerParams(dimension_semantics=sem)


def _rmsnorm(x, g):
    x = x.astype(F32)
    x = x * lax.rsqrt(jnp.mean(x * x, axis=-1, keepdims=True) + RMS_EPS)
    return x * g


def _softplus(z):
    return jnp.maximum(z, 0.0) + jnp.log1p(jnp.exp(-jnp.abs(z)))


def _sigmoid(x):
    return 0.5 * jnp.tanh(0.5 * x) + 0.5


def _gelu_tanh(x):
    c = 0.7978845608028654
    return x * (0.5 * (1.0 + jnp.tanh(c * (x + 0.044715 * (x * x * x)))))


def _norm_matmul_kernel(x_ref, g_ref, w_ref, o_ref, xn_ref, *, scale):
    @pl.when(pl.program_id(1) == 0)
    def _():
        xn_ref[...] = _rmsnorm(x_ref[...], g_ref[...]).astype(BF16)

    acc = jnp.dot(xn_ref[...], w_ref[...], preferred_element_type=F32)
    if scale != 1.0:
        acc = acc * scale
    o_ref[...] = acc.astype(o_ref.dtype)


def norm_matmul(x, g, w, *, tn, out_dtype=F32, scale=1.0, name="norm_matmul"):
    m, d = x.shape
    n = w.shape[1]
    tm = _tile(m, 1024)
    return pl.pallas_call(
        functools.partial(_norm_matmul_kernel, scale=scale),
        grid=(m // tm, n // tn),
        in_specs=[
            pl.BlockSpec((tm, d), lambda i, j: (i, 0)),
            pl.BlockSpec((1, d), lambda i, j: (0, 0)),
            pl.BlockSpec((d, tn), lambda i, j: (0, j)),
        ],
        out_specs=pl.BlockSpec((tm, tn), lambda i, j: (i, j)),
        out_shape=jax.ShapeDtypeStruct((m, n), out_dtype),
        scratch_shapes=[pltpu.VMEM((tm, d), BF16)],
        compiler_params=_params("parallel", "arbitrary"),
        name=name,
    )(x, g.reshape(1, d), w)


def _fox_proj_kernel(x_ref, g_ref, w_ref, wf_ref, q_ref, k_ref, v_ref, f_ref, *rest, q_scale, split_heads):
    d = x_ref.shape[1]
    xn = _rmsnorm(x_ref[...], g_ref[...]).astype(BF16)
    q = jnp.dot(xn, w_ref[:, 0:d], preferred_element_type=F32)
    q_ref[...] = (q * q_scale).astype(q_ref.dtype)
    k = jnp.dot(xn, w_ref[:, d:2 * d], preferred_element_type=F32)
    k_ref[...] = k
    v = jnp.dot(xn, w_ref[:, 2 * d:3 * d], preferred_element_type=F32)
    v_ref[...] = v
    f_ref[...] = jnp.dot(xn, wf_ref[...], preferred_element_type=F32)
    if split_heads:
        k3_ref, v3_ref = rest
        k3_ref[...] = k.reshape(k3_ref.shape)
        v3_ref[...] = v.reshape(v3_ref.shape)


def fox_proj(x, g, w_qkv, w_f, *, q_dtype, q_scale, split_heads):
    m, d = x.shape
    tm = _tile(m, 512)
    dh = d // FOX_HEADS
    row = lambda i: (i, 0)
    out_specs = [pl.BlockSpec((tm, d), row), pl.BlockSpec((tm, d), row), pl.BlockSpec((tm, d), row),
                 pl.BlockSpec((tm, LANES), row)]
    out_shape = [jax.ShapeDtypeStruct((m, d), q_dtype), jax.ShapeDtypeStruct((m, d), F32),
                 jax.ShapeDtypeStruct((m, d), F32), jax.ShapeDtypeStruct((m, LANES), F32)]
    if split_heads:
        out_specs += [pl.BlockSpec((tm, FOX_HEADS, dh), lambda i: (i, 0, 0))] * 2
        out_shape += [jax.ShapeDtypeStruct((m, FOX_HEADS, dh), F32)] * 2
    return pl.pallas_call(
        functools.partial(_fox_proj_kernel, q_scale=q_scale, split_heads=split_heads),
        grid=(m // tm,),
        in_specs=[
            pl.BlockSpec((tm, d), row),
            pl.BlockSpec((1, d), lambda i: (0, 0)),
            pl.BlockSpec((d, 3 * d), lambda i: (0, 0)),
            pl.BlockSpec((d, LANES), lambda i: (0, 0)),
        ],
        out_specs=out_specs,
        out_shape=out_shape,
        compiler_params=_params("parallel"),
        name="fox_proj",
    )(x, g.reshape(1, d), w_qkv, w_f)


def _mem_kv_kernel(x_ref, g_ref, w_ref, k_ref, v_ref, xn_ref):
    j = pl.program_id(2)

    @pl.when(j == 0)
    def _():
        xn_ref[...] = _rmsnorm(x_ref[...], g_ref[...]).astype(BF16)
        k_ref[...] = jnp.dot(xn_ref[...], w_ref[...], preferred_element_type=F32)

    @pl.when(j == 1)
    def _():
        v_ref[...] = jnp.dot(xn_ref[...], w_ref[...], preferred_element_type=F32)


def mem_kv(mem, g_all, w_all):
    m, d = mem.shape
    depth = g_all.shape[0]
    tm = _tile(m, 512)
    out_spec = pl.BlockSpec((None, tm, d), lambda l, i, j: (l, i, 0))
    return pl.pallas_call(
        _mem_kv_kernel,
        grid=(depth, m // tm, 2),
        in_specs=[
            pl.BlockSpec((tm, d), lambda l, i, j: (i, 0)),
            pl.BlockSpec((None, 1, d), lambda l, i, j: (l, 0, 0)),
            pl.BlockSpec((None, d, d), lambda l, i, j: (l, 0, j)),
        ],
        out_specs=[out_spec, out_spec],
        out_shape=[jax.ShapeDtypeStruct((depth, m, d), F32)] * 2,
        scratch_shapes=[pltpu.VMEM((tm, d), BF16)],
        compiler_params=_params("parallel", "parallel", "arbitrary"),
        name="mem_kv",
    )(mem, g_all.reshape(depth, 1, d), w_all)


def _matmul_res_kernel(h_ref, w_ref, r_ref, o_ref):
    o_ref[...] = r_ref[...] + jnp.dot(h_ref[...].astype(BF16), w_ref[...], preferred_element_type=F32)


def matmul_res(h, w, res, *, name="matmul_res"):
    m, k = h.shape
    n = w.shape[1]
    tm = _tile(m, 512)
    return pl.pallas_call(
        _matmul_res_kernel,
        grid=(m // tm,),
        in_specs=[
            pl.BlockSpec((tm, k), lambda i: (i, 0)),
            pl.BlockSpec((k, n), lambda i: (0, 0)),
            pl.BlockSpec((tm, n), lambda i: (i, 0)),
        ],
        out_specs=pl.BlockSpec((tm, n), lambda i: (i, 0)),
        out_shape=jax.ShapeDtypeStruct((m, n), F32),
        compiler_params=_params("parallel"),
        name=name,
    )(h, w, res)


def _mlp_kernel(*refs, final_norm):
    if final_norm:
        x_ref, g_ref, wu_ref, wd_ref, gf_ref, o_ref, y_ref, xn_ref, acc_ref = refs
    else:
        x_ref, g_ref, wu_ref, wd_ref, o_ref, xn_ref, acc_ref = refs
    j = pl.program_id(1)

    @pl.when(j == 0)
    def _():
        xn_ref[...] = _rmsnorm(x_ref[...], g_ref[...]).astype(BF16)
        acc_ref[...] = jnp.zeros_like(acc_ref)

    h = jnp.dot(xn_ref[...], wu_ref[...], preferred_element_type=F32)
    h = jnp.square(jnp.maximum(h, 0.0)).astype(BF16)
    acc_ref[...] += jnp.dot(h, wd_ref[...], preferred_element_type=F32)

    @pl.when(j == pl.num_programs(1) - 1)
    def _():
        out = x_ref[...] + acc_ref[...]
        o_ref[...] = out
        if final_norm:
            y_ref[...] = _rmsnorm(out, gf_ref[...])


def mlp(x, g, w_up, w_down, final_g=None):
    m, d = x.shape
    f = w_up.shape[1]
    tm = _tile(m, 1024)
    tf = _tile(f, 1024)
    row = lambda i, j: (i, 0)
    vec = pl.BlockSpec((1, d), lambda i, j: (0, 0))
    in_specs = [
        pl.BlockSpec((tm, d), row),
        vec,
        pl.BlockSpec((d, tf), lambda i, j: (0, j)),
        pl.BlockSpec((tf, d), lambda i, j: (j, 0)),
    ]
    args = [x, g.reshape(1, d), w_up, w_down]
    out_specs = [pl.BlockSpec((tm, d), row)]
    out_shape = [jax.ShapeDtypeStruct((m, d), F32)]
    if final_g is not None:
        in_specs.append(vec)
        args.append(final_g.reshape(1, d))
        out_specs.append(pl.BlockSpec((tm, d), row))
        out_shape.append(jax.ShapeDtypeStruct((m, d), F32))
    outs = pl.pallas_call(
        functools.partial(_mlp_kernel, final_norm=final_g is not None),
        grid=(m // tm, f // tf),
        in_specs=in_specs,
        out_specs=out_specs,
        out_shape=out_shape,
        scratch_shapes=[pltpu.VMEM((tm, d), BF16), pltpu.VMEM((tm, d), F32)],
        compiler_params=_params("parallel", "arbitrary"),
        name="mlp",
    )(*args)
    return outs if final_g is not None else (outs[0], None)


MEM_ATTN_SEQS = 4


def _mem_attn_kernel(q_ref, k_ref, v_ref, o_ref):
    nb, rows, d = q_ref.shape
    nh = XA_HEADS
    dh = d // nh
    n_rows = k_ref.shape[1] * nh
    head_of_col = lax.broadcasted_iota(jnp.int32, (rows, n_rows), 1) % nh
    pairs = [(n, h) for n in range(nb) for h in range(nh)]
    kf = [k_ref[n].reshape(n_rows, dh).astype(BF16) for n in range(nb)]
    vf = [v_ref[n].reshape(n_rows, dh).astype(BF16) for n in range(nb)]
    scores = [lax.dot_general(q_ref[n, :, h * dh:(h + 1) * dh], kf[n], _NT, preferred_element_type=F32)
              for n, h in pairs]
    probs = []
    for (n, h), s in zip(pairs, scores):
        s = jnp.where(head_of_col == h, s, NEG_INF)
        e = jnp.exp(s - jnp.max(s, axis=1, keepdims=True))
        probs.append((e * (1.0 / jnp.sum(e, axis=1, keepdims=True))).astype(BF16))
    for (n, h), p in zip(pairs, probs):
        o = jnp.dot(p, vf[n], preferred_element_type=F32)
        o_ref[n, :, h * dh:(h + 1) * dh] = o.astype(o_ref.dtype)


def mem_attn(q, k_all, v_all, layer):
    bsz, t, d = q.shape
    nb = MEM_ATTN_SEQS if bsz % MEM_ATTN_SEQS == 0 else 1
    kv_spec = pl.BlockSpec((None, nb) + k_all.shape[2:], lambda b: (layer, b, 0, 0, 0))
    q_spec = pl.BlockSpec((nb, t, d), lambda b: (b, 0, 0))
    return pl.pallas_call(
        _mem_attn_kernel,
        grid=(bsz // nb,),
        in_specs=[q_spec, kv_spec, kv_spec],
        out_specs=q_spec,
        out_shape=jax.ShapeDtypeStruct((bsz, t, d), BF16),
        compiler_params=_params("parallel"),
        name="mem_attn",
    )(q, k_all, v_all)


def _xattn_kernel(x_ref, g_ref, wq_ref, k_ref, v_ref, wo_ref, o_ref, q_scr, oh_scr, *, scale):
    d = x_ref.shape[1]
    dh = d // XA_HEADS
    x = x_ref[...]
    xn = _rmsnorm(x, g_ref[...]).astype(BF16)
    q_scr[...] = (jnp.dot(xn, wq_ref[...], preferred_element_type=F32) * scale).astype(BF16)
    heads = [slice(h * dh, (h + 1) * dh) for h in range(XA_HEADS)]
    scores = [lax.dot_general(q_scr[:, sl], k_ref[:, sl].astype(BF16), _NT, preferred_element_type=F32)
              for sl in heads]
    probs = []
    for s in scores:
        e = jnp.exp(s - jnp.max(s, axis=1, keepdims=True))
        probs.append((e * (1.0 / jnp.sum(e, axis=1, keepdims=True))).astype(BF16))
    for sl, p in zip(heads, probs):
        oh_scr[:, sl] = jnp.dot(p, v_ref[:, sl].astype(BF16), preferred_element_type=F32).astype(BF16)
    o_ref[...] = x + jnp.dot(oh_scr[...], wo_ref[...], preferred_element_type=F32)


def xattn(x, g, w_q, k_all, v_all, w_o, layer):
    bsz, t, d = x.shape
    n_mem = k_all.shape[2]
    tm = _tile(t, 512)
    kv_spec = pl.BlockSpec((None, None, n_mem, d), lambda b, i: (layer, b, 0, 0))
    w_spec = pl.BlockSpec((d, d), lambda b, i: (0, 0))
    x_spec = pl.BlockSpec((None, tm, d), lambda b, i: (b, i, 0))
    return pl.pallas_call(
        functools.partial(_xattn_kernel, scale=(d // XA_HEADS) ** -0.5),
        grid=(bsz, t // tm),
        in_specs=[x_spec, pl.BlockSpec((1, d), lambda b, i: (0, 0)), w_spec, kv_spec, kv_spec, w_spec],
        out_specs=x_spec,
        out_shape=jax.ShapeDtypeStruct((bsz, t, d), F32),
        scratch_shapes=[pltpu.VMEM((tm, d), BF16), pltpu.VMEM((tm, d), BF16)],
        compiler_params=_params("parallel", "parallel"),
        name="xattn",
    )(x, g.reshape(1, d), w_q, k_all, v_all, w_o)


def _lru_gate_logits(uc, wg_ref):
    half = uc.shape[1] // 2
    ucb = uc.astype(BF16)
    g0 = jnp.dot(ucb[:, :half], wg_ref[0], preferred_element_type=F32)
    g1 = jnp.dot(ucb[:, half:], wg_ref[1], preferred_element_type=F32)
    rg = jnp.concatenate([g0[:, :half], g1[:, :half]], axis=1)
    ig = jnp.concatenate([g0[:, half:], g1[:, half:]], axis=1)
    return rg, ig


def _lru_coeffs(rg, ig, lam):
    r = _sigmoid(rg)
    i = _sigmoid(ig)
    log_a = (-LRU_C * r) * _softplus(-lam)
    a = jnp.exp(log_a)
    mult = jnp.sqrt(-jnp.tanh(log_a) * (a * a + 1.0))
    return a, mult, i


def _scan8(a8, x8, row8):
    for s in (1, 2, 4):
        keep = row8 >= s
        xs = jnp.where(keep, pltpu.roll(x8, s, 0), 0.0)
        a_s = jnp.where(keep, pltpu.roll(a8, s, 0), 1.0)
        x8 = x8 + a8 * xs
        a8 = a8 * a_s
    return a8, x8


LRU_PARTS = 2


def _lru_prompt_kernel(x_ref, g_ref, win_ref, buf_ref, h0_ref, cw_ref, cb_ref, wg_ref, brg_ref, big_ref, lam_ref,
                       wout_ref, o_ref, hl_ref, cbuf_ref, ufull, x_scr, gate_scr, hc, *, first_at_zero):
    i = pl.program_id(1)
    tt = x_ref.shape[0]
    c = wout_ref.shape[0]
    pad = SUBLANES

    @pl.when(i == 0)
    def _():
        ufull[0:pad, :] = buf_ref[...]
        hc[...] = h0_ref[...]

    @pl.when(i > 0)
    def _():
        ufull[0:pad, :] = ufull[tt:tt + pad, :]

    hr = tt // LRU_PARTS
    parts = [slice(k * hr, (k + 1) * hr) for k in range(LRU_PARTS)]
    xns = [_rmsnorm(x_ref[rows, :], g_ref[...]).astype(BF16) for rows in parts]
    gus = [jnp.dot(xn, win_ref[...], preferred_element_type=F32) for xn in xns]
    ucs = []
    for rows, gu in zip(parts, gus):
        gate_scr[rows, :] = _gelu_tanh(gu[:, :c])
        u = gu[:, c:]
        ufull[pad + rows.start:pad + rows.stop, :] = u
        uc = cb_ref[...]
        for k in range(CONV_WIDTH - 1):
            off = pad - (CONV_WIDTH - 1) + k
            uc = uc + ufull[off + rows.start:off + rows.stop, :] * cw_ref[k:k + 1, :]
        ucs.append(uc + u * cw_ref[CONV_WIDTH - 1:CONV_WIDTH, :])
    logits = [_lru_gate_logits(uc, wg_ref) for uc in ucs]

    row8 = lax.broadcasted_iota(jnp.int32, (SUBLANES, c), 0)
    h = hc[...]
    for rows, uc, (rg, ig) in zip(parts, ucs, logits):
        a, mult, ig = _lru_coeffs(rg + brg_ref[...], ig + big_ref[...], lam_ref[...])
        if first_at_zero and rows.start == 0:
            row = lax.broadcasted_iota(jnp.int32, (hr, 1), 0)
            mult = jnp.where(jnp.logical_and(row == 0, i == 0), 1.0, mult)
        xin = (mult * ig) * uc
        for grp in range(hr // SUBLANES):
            sub = slice(grp * SUBLANES, (grp + 1) * SUBLANES)
            a8, x8 = _scan8(a[sub, :], xin[sub, :], row8)
            h8 = x8 + a8 * h
            x_scr[rows.start + sub.start:rows.start + sub.stop, :] = h8
            h = h8[SUBLANES - 1:SUBLANES, :]
        y = (gate_scr[rows, :] * x_scr[rows, :]).astype(BF16)
        o_ref[rows, :] = x_ref[rows, :] + jnp.dot(y, wout_ref[...], preferred_element_type=F32)
    hc[...] = h

    @pl.when(i == pl.num_programs(1) - 1)
    def _():
        hl_ref[...] = h
        cbuf_ref[...] = ufull[tt:tt + pad, :]


def lru_prompt(x, g, w_in, conv_buf, h0, conv_w, conv_b, wg, b_rg, b_ig, lam, w_out, *, pos0):
    bsz, t, d = x.shape
    c = w_out.shape[0]
    tt = _tile(t, 256)
    pad = SUBLANES
    buf8 = jnp.pad(conv_buf, ((0, 0), (pad - (CONV_WIDTH - 1), 0), (0, 0)))
    vec = pl.BlockSpec((1, c), lambda b, i: (0, 0))
    out, h_last, cbuf = pl.pallas_call(
        functools.partial(_lru_prompt_kernel, first_at_zero=(pos0 == 0)),
        grid=(bsz, t // tt),
        in_specs=[
            pl.BlockSpec((None, tt, d), lambda b, i: (b, i, 0)),
            pl.BlockSpec((1, d), lambda b, i: (0, 0)),
            pl.BlockSpec((d, 2 * c), lambda b, i: (0, 0)),
            pl.BlockSpec((None, pad, c), lambda b, i: (b, 0, 0)),
            pl.BlockSpec((None, 1, c), lambda b, i: (b, 0, 0)),
            pl.BlockSpec((CONV_WIDTH, c), lambda b, i: (0, 0)),
            vec,
            pl.BlockSpec((2, c // 2, c), lambda b, i: (0, 0, 0)),
            vec, vec, vec,
            pl.BlockSpec((c, d), lambda b, i: (0, 0)),
        ],
        out_specs=[
            pl.BlockSpec((None, tt, d), lambda b, i: (b, i, 0)),
            pl.BlockSpec((None, 1, c), lambda b, i: (b, 0, 0)),
            pl.BlockSpec((None, pad, c), lambda b, i: (b, 0, 0)),
        ],
        out_shape=[
            jax.ShapeDtypeStruct((bsz, t, d), F32),
            jax.ShapeDtypeStruct((bsz, 1, c), F32),
            jax.ShapeDtypeStruct((bsz, pad, c), F32),
        ],
        scratch_shapes=[
            pltpu.VMEM((tt + pad, c), F32),
            pltpu.VMEM((tt, c), F32),
            pltpu.VMEM((tt, c), F32),
            pltpu.VMEM((1, c), F32),
        ],
        compiler_params=_params("parallel", "arbitrary"),
        name="lru_prompt",
    )(x, g.reshape(1, d), w_in, buf8, h0.reshape(bsz, 1, c), conv_w, conv_b.reshape(1, c), wg,
      b_rg.reshape(1, c), b_ig.reshape(1, c), lam.reshape(1, c), w_out)
    return out, h_last.reshape(bsz, c), cbuf[:, pad - (CONV_WIDTH - 1):, :]


def _lru_step_kernel(gate_ref, u_ref, buf_ref, h0_ref, cw_ref, cb_ref, wg_ref, brg_ref, big_ref, lam_ref,
                     y_ref, h_ref, *, first_at_zero):
    u = u_ref[...]
    uc = cb_ref[...]
    for k in range(CONV_WIDTH - 1):
        uc = uc + buf_ref[k] * cw_ref[k:k + 1, :]
    uc = uc + u * cw_ref[CONV_WIDTH - 1:CONV_WIDTH, :]
    rg, ig = _lru_gate_logits(uc, wg_ref)
    a, mult, ig = _lru_coeffs(rg + brg_ref[...], ig + big_ref[...], lam_ref[...])
    if first_at_zero:
        mult = jnp.ones_like(mult)
    h = a * h0_ref[...] + (mult * ig) * uc
    h_ref[...] = h
    y_ref[...] = (_gelu_tanh(gate_ref[...]) * h).astype(y_ref.dtype)


def lru_step(gu, conv_buf, h0, conv_w, conv_b, wg, b_rg, b_ig, lam, *, pos0):
    bsz, c2 = gu.shape
    c = c2 // 2
    buf_t = jnp.swapaxes(conv_buf, 0, 1)
    full = lambda *shape: pl.BlockSpec(shape, lambda i: (0,) * len(shape))
    y, h = pl.pallas_call(
        functools.partial(_lru_step_kernel, first_at_zero=(pos0 == 0)),
        grid=(1,),
        in_specs=[
            pl.BlockSpec((bsz, c), lambda i: (0, 0)),
            pl.BlockSpec((bsz, c), lambda i: (0, 1)),
            full(CONV_WIDTH - 1, bsz, c),
            full(bsz, c),
            full(CONV_WIDTH, c),
            full(1, c),
            full(2, c // 2, c),
            full(1, c), full(1, c), full(1, c),
        ],
        out_specs=[full(bsz, c), full(bsz, c)],
        out_shape=[jax.ShapeDtypeStruct((bsz, c), BF16), jax.ShapeDtypeStruct((bsz, c), F32)],
        compiler_params=_params("arbitrary"),
        name="lru_step",
    )(gu, gu, buf_t, h0, conv_w, conv_b.reshape(1, c), wg,
      b_rg.reshape(1, c), b_ig.reshape(1, c), lam.reshape(1, c))
    new_buf = jnp.concatenate([conv_buf[:, 1:], gu[:, None, c:]], axis=1)
    return y, h, new_buf


def _pool_groups(xn, shifted, cnt, w_ref, b, scale):
    d = xn.shape[1]
    gw = d // len(POOL_WINDOWS)
    ys = []
    for gi, win in enumerate(POOL_WINDOWS):
        ch = slice(gi * gw, (gi + 1) * gw)
        s = xn[:, ch]
        for k in range(1, win):
            s = s + shifted(k, ch)
        dd = (s / cnt(win) - xn[:, ch]).astype(BF16)
        ys.append(jnp.dot(dd, w_ref[gi], preferred_element_type=F32))
    return (jnp.concatenate(ys, axis=1) + b) * scale


def _pool_prompt_kernel(x_ref, buf_ref, g_ref, w_ref, b_ref, sc_ref, o_ref, nb_ref, full, *, pos0):
    i = pl.program_id(1)
    tm, d = x_ref.shape

    @pl.when(i == 0)
    def _():
        full[0:HALO, :] = buf_ref[...]

    @pl.when(i > 0)
    def _():
        full[0:HALO, :] = full[tm:tm + HALO, :]

    x = x_ref[...]
    xn = _rmsnorm(x, g_ref[...])
    full[HALO:HALO + tm, :] = xn
    pos = pos0 + i * tm + lax.broadcasted_iota(jnp.int32, (tm, 1), 0)
    y = _pool_groups(
        xn,
        lambda k, ch: full[HALO - k:HALO - k + tm, ch],
        lambda win: jnp.minimum(pos + 1, win).astype(F32),
        w_ref, b_ref[...], sc_ref[...])
    o_ref[...] = x + y

    @pl.when(i == pl.num_programs(1) - 1)
    def _():
        nb_ref[...] = full[tm:tm + HALO, :]


def pool_prompt(x, buf, g, w, b, scale, *, pos0):
    bsz, t, d = x.shape
    nbuf = buf.shape[1]
    tm = _tile(t, 512)
    buf16 = jnp.pad(buf, ((0, 0), (HALO - nbuf, 0), (0, 0)))
    vec = pl.BlockSpec((1, d), lambda b_, i: (0, 0))
    ng = len(POOL_WINDOWS)
    out, nb = pl.pallas_call(
        functools.partial(_pool_prompt_kernel, pos0=pos0),
        grid=(bsz, t // tm),
        in_specs=[
            pl.BlockSpec((None, tm, d), lambda b_, i: (b_, i, 0)),
            pl.BlockSpec((None, HALO, d), lambda b_, i: (b_, 0, 0)),
            vec,
            pl.BlockSpec((ng, d // ng, d // ng), lambda b_, i: (0, 0, 0)),
            vec, vec,
        ],
        out_specs=[
            pl.BlockSpec((None, tm, d), lambda b_, i: (b_, i, 0)),
            pl.BlockSpec((None, HALO, d), lambda b_, i: (b_, 0, 0)),
        ],
        out_shape=[jax.ShapeDtypeStruct((bsz, t, d), F32), jax.ShapeDtypeStruct((bsz, HALO, d), F32)],
        scratch_shapes=[pltpu.VMEM((tm + HALO, d), F32)],
        compiler_params=_params("parallel", "arbitrary"),
        name="pool_prompt",
    )(x, buf16, g.reshape(1, d), w, b.reshape(1, d), scale.reshape(1, d))
    return out, nb[:, HALO - nbuf:, :]


def _pool_step_kernel(x_ref, buf_ref, g_ref, w_ref, b_ref, sc_ref, o_ref, xn_ref, *, pos0):
    x = x_ref[...]
    xn = _rmsnorm(x, g_ref[...])
    nbuf = buf_ref.shape[0]
    y = _pool_groups(
        xn,
        lambda k, ch: buf_ref[nbuf - k, :, ch],
        lambda win: float(min(pos0 + 1, win)),
        w_ref, b_ref[...], sc_ref[...])
    o_ref[...] = x + y
    xn_ref[...] = xn


def pool_step(x, buf, g, w, b, scale, *, pos0):
    bsz, d = x.shape
    nbuf = buf.shape[1]
    ng = len(POOL_WINDOWS)
    buf_t = jnp.swapaxes(buf, 0, 1)
    full = lambda *shape: pl.BlockSpec(shape, lambda i: (0,) * len(shape))
    out, xn = pl.pallas_call(
        functools.partial(_pool_step_kernel, pos0=pos0),
        grid=(1,),
        in_specs=[full(bsz, d), full(nbuf, bsz, d), full(1, d), full(ng, d // ng, d // ng), full(1, d), full(1, d)],
        out_specs=[full(bsz, d), full(bsz, d)],
        out_shape=[jax.ShapeDtypeStruct((bsz, d), F32)] * 2,
        compiler_params=_params("arbitrary"),
        name="pool_step",
    )(x, buf_t, g.reshape(1, d), w, b.reshape(1, d), scale.reshape(1, d))
    return out, jnp.concatenate([buf[:, 1:], xn[:, None, :]], axis=1)


def _lf_cumsum_kernel(fl_ref, b_ref, lf_ref, cum_ref, cumt_ref):
    t, w = fl_ref.shape
    lf_ref[...] = -_softplus(-(fl_ref[...] + b_ref[...]))
    row8 = lax.broadcasted_iota(jnp.int32, (SUBLANES, w), 0)

    def body(r, carry):
        off = pl.multiple_of(r * SUBLANES, SUBLANES)
        x8 = lf_ref[pl.ds(off, SUBLANES), :]
        for s in (1, 2, 4):
            x8 = x8 + jnp.where(row8 >= s, pltpu.roll(x8, s, 0), 0.0)
        c8 = x8 + carry
        cum_ref[pl.ds(off, SUBLANES), :] = c8
        return c8[SUBLANES - 1:SUBLANES, :]

    lax.fori_loop(0, t // SUBLANES, body, jnp.zeros((1, w), F32))
    cumt_ref[...] = cum_ref[...].T


def lf_cumsum(fl, b_pad):
    bsz, t, w = fl.shape
    blk = pl.BlockSpec((None, t, w), lambda b: (b, 0, 0))
    return pl.pallas_call(
        _lf_cumsum_kernel,
        grid=(bsz,),
        in_specs=[blk, pl.BlockSpec((1, w), lambda b: (0, 0))],
        out_specs=[blk, blk, pl.BlockSpec((None, w, t), lambda b: (b, 0, 0))],
        out_shape=[jax.ShapeDtypeStruct((bsz, t, w), F32)] * 2 + [jax.ShapeDtypeStruct((bsz, w, t), F32)],
        compiler_params=_params("parallel"),
        name="fox_lf_cumsum",
    )(fl, b_pad)


def _lf_kernel(fl_ref, b_ref, lf_ref):
    lf_ref[...] = -_softplus(-(fl_ref[...] + b_ref[...]))


def lf_only(fl, b_pad):
    m, w = fl.shape
    return pl.pallas_call(
        _lf_kernel,
        grid=(1,),
        in_specs=[pl.BlockSpec((m, w), lambda i: (0, 0)), pl.BlockSpec((1, w), lambda i: (0, 0))],
        out_specs=pl.BlockSpec((m, w), lambda i: (0, 0)),
        out_shape=jax.ShapeDtypeStruct((m, w), F32),
        name="fox_lf",
    )(fl, b_pad)


FLASH_HEADS = 4


def _fox_flash_kernel(qi_tab, ki_tab, q_ref, k_ref, v_ref, fq_ref, ck_ref, o_ref,
                      q2_ref, kb_ref, vt_ref, m_ref, l_ref, acc_ref, *, tq, hb):
    hg = pl.program_id(1)
    pair = pl.program_id(2)
    qi = qi_tab[pair]
    ki = ki_tab[pair]
    w = q_ref.shape[1]
    dh = w // hb

    @pl.when(ki == 0)
    def _():
        m_ref[...] = jnp.full_like(m_ref, -jnp.inf)
        l_ref[...] = jnp.zeros_like(l_ref)
        acc_ref[...] = jnp.zeros_like(acc_ref)
        q = q_ref[...].astype(F32)
        head_of_lane = lax.broadcasted_iota(jnp.int32, (tq, w), 1) // dh
        for hh in range(hb):
            q2_ref[hh] = jnp.where(head_of_lane == hh, q, 0.0).astype(BF16)

    kb_ref[...] = k_ref[...].astype(BF16)
    vt_ref[...] = v_ref[...].T.astype(BF16)
    ck = pltpu.roll(ck_ref[...], (LANES - hg * hb) % LANES, 1)
    fk_cols = [ck[:, hh:hh + 1] for hh in range(hb)]

    def pair_update(masked):
        scores = [lax.dot_general(kb_ref[...], q2_ref[hh], _NT, preferred_element_type=F32) for hh in range(hb)]
        if masked:
            keep = (lax.broadcasted_iota(jnp.int32, (tq, tq), 1) >= lax.broadcasted_iota(jnp.int32, (tq, tq), 0))
        for hh in range(hb):
            t = scores[hh] - fk_cols[hh]
            if masked:
                t = jnp.where(keep, t, NEG_INF)
            fq = fq_ref[hh:hh + 1, :]
            m_prev = m_ref[hh:hh + 1, :]
            m_new = jnp.maximum(m_prev, jnp.max(t, axis=0, keepdims=True) + fq)
            alpha = jnp.exp(m_prev - m_new)
            p = jnp.exp(t + (fq - m_new))
            l_ref[hh:hh + 1, :] = alpha * l_ref[hh:hh + 1, :] + jnp.sum(p, axis=0, keepdims=True)
            m_ref[hh:hh + 1, :] = m_new
            ch = slice(hh * dh, (hh + 1) * dh)
            pv = jnp.dot(vt_ref[ch, :], p.astype(BF16), preferred_element_type=F32)
            acc_ref[ch, :] = alpha * acc_ref[ch, :] + pv

    @pl.when(ki < qi)
    def _():
        pair_update(False)

    @pl.when(ki == qi)
    def _():
        pair_update(True)
        for hh in range(hb):
            ch = slice(hh * dh, (hh + 1) * dh)
            acc_ref[ch, :] = acc_ref[ch, :] * (1.0 / l_ref[hh:hh + 1, :])
        o_ref[...] = acc_ref[...].T.astype(o_ref.dtype)


def fox_flash(q, k, v, cum, cum_t):
    bsz, t, d = q.shape
    tq = _tile(t, 512)
    hb = FLASH_HEADS
    w = hb * (d // FOX_HEADS)
    nq = t // tq
    pairs = [(qi, ki) for qi in range(nq) for ki in range(qi + 1)]
    qi_tab = jnp.asarray([pq for pq, _ in pairs], jnp.int32)
    ki_tab = jnp.asarray([pk for _, pk in pairs], jnp.int32)
    cum_rows = cum_t[:, :FOX_HEADS, :].reshape(bsz, FOX_HEADS // hb, hb, t)
    q_spec = pl.BlockSpec((None, tq, w), lambda b, hg, p, qt, kt: (b, qt[p], hg))
    kv_spec = pl.BlockSpec((None, tq, w), lambda b, hg, p, qt, kt: (b, kt[p], hg))
    grid_spec = pltpu.PrefetchScalarGridSpec(
        num_scalar_prefetch=2,
        grid=(bsz, d // w, len(pairs)),
        in_specs=[
            q_spec, kv_spec, kv_spec,
            pl.BlockSpec((None, None, hb, tq), lambda b, hg, p, qt, kt: (b, hg, 0, qt[p])),
            pl.BlockSpec((None, tq, LANES), lambda b, hg, p, qt, kt: (b, kt[p], 0)),
        ],
        out_specs=q_spec,
        scratch_shapes=[
            pltpu.VMEM((hb, tq, w), BF16),
            pltpu.VMEM((tq, w), BF16),
            pltpu.VMEM((w, tq), BF16),
            pltpu.VMEM((hb, tq), F32),
            pltpu.VMEM((hb, tq), F32),
            pltpu.VMEM((w, tq), F32),
        ],
    )
    return pl.pallas_call(
        functools.partial(_fox_flash_kernel, tq=tq, hb=hb),
        grid_spec=grid_spec,
        out_shape=jax.ShapeDtypeStruct((bsz, t, d), BF16),
        compiler_params=_params("parallel", "parallel", "arbitrary"),
        name="fox_flash",
    )(qi_tab, ki_tab, q, k, v, cum_rows, cum)


DECODE_PAGES = 8


def _fox_decode_kernel(pt_ref, q_ref, kn_ref, vn_ref, lfn_ref, *refs):
    g = DECODE_PAGES
    k_refs, v_refs, lft_refs = refs[0:g], refs[g:2 * g], refs[2 * g:3 * g]
    o_ref, qm_ref, m_ref, l_ref, acc_ref, carry_ref = refs[3 * g:]
    j = pl.program_id(1)
    d = q_ref.shape[1]
    nh = FOX_HEADS
    dh = d // nh
    page = k_refs[0].shape[1]
    head = lax.broadcasted_iota(jnp.int32, (nh, d), 0)
    own = head == lax.broadcasted_iota(jnp.int32, (nh, d), 1) // dh

    @pl.when(j == 0)
    def _():
        qm_ref[...] = jnp.where(own, jnp.broadcast_to(q_ref[...], (nh, d)), 0.0).astype(BF16)
        m_ref[...] = jnp.full_like(m_ref, -jnp.inf)
        l_ref[...] = jnp.zeros_like(l_ref)
        acc_ref[...] = jnp.zeros_like(acc_ref)
        carry_ref[...] = jnp.zeros_like(carry_ref)

    hl = lax.broadcasted_iota(jnp.int32, (nh, LANES), 0) == lax.broadcasted_iota(jnp.int32, (nh, LANES), 1)
    lf_new = jnp.sum(jnp.where(hl, jnp.broadcast_to(lfn_ref[...], (nh, LANES)), 0.0), axis=1, keepdims=True)
    later = (lax.broadcasted_iota(jnp.int32, (page, page), 0)
             > lax.broadcasted_iota(jnp.int32, (page, page), 1)).astype(F32)

    qm = qm_ref[...]
    scores = [jnp.dot(qm, k_ref[...].astype(BF16), preferred_element_type=F32) for k_ref in k_refs]
    lfts = [lft_ref[...] for lft_ref in lft_refs]
    within = jnp.dot(jnp.concatenate(lfts, axis=0), later, preferred_element_type=F32,
                     precision=lax.Precision.HIGHEST)
    carry = carry_ref[...]
    logits = []
    for i in range(g):
        logits.append(scores[i] + lf_new + (within[i * nh:(i + 1) * nh, :] + carry))
        carry = carry + jnp.sum(lfts[i], axis=1, keepdims=True)
    carry_ref[...] = carry
    s = jnp.concatenate(logits, axis=1)
    m_prev = m_ref[...]
    m_new = jnp.maximum(m_prev, jnp.max(s, axis=1, keepdims=True))
    alpha = jnp.exp(m_prev - m_new)
    p = jnp.exp(s - m_new)
    l_ref[...] = alpha * l_ref[...] + jnp.sum(p, axis=1, keepdims=True)
    m_ref[...] = m_new
    pb = p.astype(BF16)
    pv = lax.dot_general(pb[:, 0:page], v_refs[0][...].astype(BF16), _NT, preferred_element_type=F32)
    for i in range(1, g):
        pv = pv + lax.dot_general(pb[:, i * page:(i + 1) * page], v_refs[i][...].astype(BF16), _NT,
                                  preferred_element_type=F32)
    acc_ref[...] = alpha * acc_ref[...] + pv

    @pl.when(j == pl.num_programs(1) - 1)
    def _():
        kn = jnp.broadcast_to(kn_ref[...].astype(BF16).astype(F32), (nh, d))
        vn = jnp.broadcast_to(vn_ref[...].astype(BF16).astype(F32), (nh, d))
        s_new = jnp.sum(qm_ref[...].astype(F32) * kn, axis=1, keepdims=True)
        m_prev = m_ref[...]
        m_fin = jnp.maximum(m_prev, s_new)
        alpha = jnp.exp(m_prev - m_fin)
        p_new = jnp.exp(s_new - m_fin)
        l_fin = alpha * l_ref[...] + p_new
        acc = alpha * acc_ref[...] + p_new.astype(BF16).astype(F32) * vn
        o_ref[...] = jnp.sum(jnp.where(own, acc * (1.0 / l_fin), 0.0), axis=0, keepdims=True)


def fox_decode(q, k_new, v_new, lf_new, kt_pool, vt_pool, lft_pool, page_table, layer):
    bsz, d = q.shape
    n_pages = page_table.shape[1]
    page = kt_pool.shape[3]
    nh = FOX_HEADS
    g = DECODE_PAGES
    assert n_pages % g == 0
    row = pl.BlockSpec((None, 1, d), lambda b, j, pt: (b, 0, 0))

    def page_idx(i):
        return lambda b, j, pt: (layer, pt[b * n_pages + (n_pages - 1 - (j * g + i))], 0, 0)

    grid_spec = pltpu.PrefetchScalarGridSpec(
        num_scalar_prefetch=1,
        grid=(bsz, n_pages // g),
        in_specs=[row, row, row, pl.BlockSpec((None, 1, LANES), lambda b, j, pt: (b, 0, 0))]
        + [pl.BlockSpec((None, None, d, page), page_idx(i)) for i in range(g)]
        + [pl.BlockSpec((None, None, d, page), page_idx(i)) for i in range(g)]
        + [pl.BlockSpec((None, None, nh, page), page_idx(i)) for i in range(g)],
        out_specs=row,
        scratch_shapes=[
            pltpu.VMEM((nh, d), BF16),
            pltpu.VMEM((nh, 1), F32),
            pltpu.VMEM((nh, 1), F32),
            pltpu.VMEM((nh, d), F32),
            pltpu.VMEM((nh, 1), F32),
        ],
    )
    out = pl.pallas_call(
        _fox_decode_kernel,
        grid_spec=grid_spec,
        out_shape=jax.ShapeDtypeStruct((bsz, 1, d), F32),
        compiler_params=_params("parallel", "arbitrary"),
        name="fox_decode",
    )(page_table.reshape(-1), q.reshape(bsz, 1, d), k_new.reshape(bsz, 1, d), v_new.reshape(bsz, 1, d),
      lf_new.reshape(bsz, 1, LANES), *([kt_pool] * g + [vt_pool] * g + [lft_pool] * g))
    return out.reshape(bsz, d)


def _block_diag_gate_weights(w_rg, w_ig):
    nb, bw, _ = w_rg.shape
    half = nb // 2
    assert (half * bw) % LANES == 0
    eye = jnp.eye(half, dtype=w_rg.dtype)

    def dense(w):
        return (w[:, :, None, :] * eye[:, None, :, None]).reshape(half * bw, half * bw)

    return jnp.stack([
        jnp.concatenate([dense(w_rg[c * half:(c + 1) * half]), dense(w_ig[c * half:(c + 1) * half])], axis=1)
        for c in range(2)]).astype(BF16)


def _trunk(x, bsz, t, pos0, mem_k, mem_v, lru_h, lru_conv, pool_buf, fox_paged, p):
    d = x.shape[1]
    depth = p["norm_mix_g"].shape[0]
    hs, convs, pools, ks, vs, lfs = [], [], [], [], [], []
    y = None
    for layer in range(depth):
        kind, j = layer % N_MIXERS, layer // N_MIXERS
        g_mix = p["norm_mix_g"][layer]
        if kind == 0:
            c = p["w_lru_out"].shape[1]
            args = (p["lru_conv_w"][j], p["lru_conv_b"][j], p["lru_wg"][j], p["lru_b_rg"][j], p["lru_b_ig"][j],
                    p["lru_lambda"][j])
            if t > 1:
                x3, hl, cb = lru_prompt(x.reshape(bsz, t, d), g_mix, p["w_lru_in"][j], lru_conv[j], lru_h[j],
                                        *args, p["w_lru_out"][j], pos0=pos0)
                x = x3.reshape(bsz * t, d)
            else:
                gu = norm_matmul(x, g_mix, p["w_lru_in"][j], tn=c, name="lru_in")
                yl, hl, cb = lru_step(gu, lru_conv[j], lru_h[j], *args, pos0=pos0)
                x = matmul_res(yl, p["w_lru_out"][j], x, name="lru_out")
            hs.append(hl)
            convs.append(cb)
        elif kind == 1:
            dh = d // FOX_HEADS
            if fox_paged is None:
                q, k, v, fl, k3, v3 = fox_proj(x, g_mix, p["w_fox_qkv"][j], p["w_fox_f"][j], q_dtype=BF16,
                                               q_scale=dh ** -0.5, split_heads=True)
                lf, cum, cum_t = lf_cumsum(fl.reshape(bsz, t, LANES), p["b_fox_f"][j])
                o = fox_flash(q.reshape(bsz, t, d), k.reshape(bsz, t, d), v.reshape(bsz, t, d), cum, cum_t)
                o = o.reshape(bsz * t, d)
                lf = lf[:, :, :FOX_HEADS]
            else:
                q, k3, v3, fl = fox_proj(x, g_mix, p["w_fox_qkv"][j], p["w_fox_f"][j], q_dtype=F32,
                                         q_scale=dh ** -0.5, split_heads=False)
                k_pool, v_pool, lft_pool, page_table = fox_paged
                lf = lf_only(fl, p["b_fox_f"][j])
                o = fox_decode(q, k3, v3, lf, k_pool, v_pool, lft_pool, page_table, j)
                lf = lf[:, :FOX_HEADS].reshape(bsz, t, FOX_HEADS)
            ks.append(k3.reshape(bsz, t, FOX_HEADS, dh))
            vs.append(v3.reshape(bsz, t, FOX_HEADS, dh))
            lfs.append(lf)
            x = matmul_res(o, p["w_fox_o"][j], x, name="fox_out")
        else:
            args = (g_mix, p["w_pool"][j], p["b_pool"][j], p["pool_scale"][j])
            if t > 1:
                x3, pb = pool_prompt(x.reshape(bsz, t, d), pool_buf[j], *args, pos0=pos0)
                x = x3.reshape(bsz * t, d)
            else:
                x, pb = pool_step(x, pool_buf[j], *args, pos0=pos0)
            pools.append(pb)

        if t > 1:
            x = xattn(x.reshape(bsz, t, d), p["norm_x_g"][layer], p["w_xq"][layer], mem_k, mem_v,
                      p["w_xo"][layer], layer).reshape(bsz * t, d)
        else:
            q = norm_matmul(x, p["norm_x_g"][layer], p["w_xq"][layer], tn=d, out_dtype=BF16,
                            scale=(d // XA_HEADS) ** -0.5, name="xattn_q")
            rows = 2 * SUBLANES
            o = mem_attn(jnp.broadcast_to(q[:, None, :], (bsz, rows, d)), mem_k, mem_v, layer)[:, 0, :]
            x = matmul_res(o, p["w_xo"][layer], x, name="xattn_out")

        final_g = p["final_norm_g"] if layer == depth - 1 else None
        x, y = mlp(x, p["norm_mlp_g"][layer], p["w_up"][layer], p["w_down"][layer], final_g)
    return y, hs, convs, pools, ks, vs, lfs


def kernel(x_prompt, x_sample, mem_prompt, cache_fox_k, cache_fox_v, cache_fox_lf, cache_mem_k, cache_mem_v, state_lru_h, state_lru_conv, state_pool, page_table, norm_mix_g, norm_mem_g, norm_x_g, norm_mlp_g, final_norm_g, w_lru_in, lru_conv_w, lru_conv_b, lru_w_rg, lru_b_rg, lru_w_ig, lru_b_ig, lru_lambda, w_lru_out, w_fox_qkvf, b_fox_f, w_fox_o, w_pool, b_pool, pool_scale, w_xq, w_xkv, w_xo, w_up, w_down):
    bsz, seq, d = x_prompt.shape
    dec, dec_seq, _ = x_sample.shape
    assert dec_seq == 1
    depth = norm_mix_g.shape[0]
    n_mem = mem_prompt.shape[1]
    n_fox = w_fox_qkvf.shape[0]
    n_lru = w_lru_in.shape[0]
    n_pool_layers = w_pool.shape[0]
    c = w_lru_out.shape[1]
    dt = x_prompt.dtype

    bias_pad = jnp.pad(b_fox_f, ((0, 0), (0, LANES - FOX_HEADS))).reshape(n_fox, 1, LANES)
    p = dict(
        norm_mix_g=norm_mix_g, norm_x_g=norm_x_g, norm_mlp_g=norm_mlp_g, final_norm_g=final_norm_g,
        w_lru_in=w_lru_in.astype(BF16), lru_conv_w=lru_conv_w, lru_conv_b=lru_conv_b,
        lru_wg=jnp.stack([_block_diag_gate_weights(lru_w_rg[l], lru_w_ig[l]) for l in range(n_lru)]),
        lru_b_rg=lru_b_rg, lru_b_ig=lru_b_ig, lru_lambda=lru_lambda, w_lru_out=w_lru_out.astype(BF16),
        w_fox_qkv=w_fox_qkvf[:, :, :3 * d].astype(BF16),
        w_fox_f=jnp.pad(w_fox_qkvf[:, :, 3 * d:], ((0, 0), (0, 0), (0, LANES - FOX_HEADS))).astype(BF16),
        b_fox_f=bias_pad, w_fox_o=w_fox_o.astype(BF16),
        w_pool=w_pool.astype(BF16), b_pool=b_pool, pool_scale=pool_scale,
        w_xq=w_xq.astype(BF16), w_xo=w_xo.astype(BF16), w_up=w_up.astype(BF16), w_down=w_down.astype(BF16),
    )

    mem_k_p, mem_v_p = mem_kv(mem_prompt.reshape(bsz * n_mem, d), norm_mem_g, w_xkv.astype(BF16))
    mem_k_p = mem_k_p.reshape(depth, bsz, n_mem, d)
    mem_v_p = mem_v_p.reshape(depth, bsz, n_mem, d)
    h0 = jnp.zeros((n_lru, bsz, c), dt)
    c0 = jnp.zeros((n_lru, bsz, CONV_WIDTH - 1, c), dt)
    pb0 = jnp.zeros((n_pool_layers, bsz, max(POOL_WINDOWS) - 1, d), dt)
    y_p, hs_p, convs_p, pools_p, ks_p, vs_p, lfs_p = _trunk(
        x_prompt.reshape(bsz * seq, d), bsz, seq, 0, mem_k_p, mem_v_p, h0, c0, pb0, None, p)

    n_pool_pages, page = cache_fox_k.shape[1], cache_fox_k.shape[2]
    pos_s = page_table.shape[1] * page
    channel_major = lambda c: jnp.transpose(c, (0, 1, 3, 4, 2)).reshape(n_fox, n_pool_pages, d, page)
    fox_paged = (channel_major(cache_fox_k), channel_major(cache_fox_v), jnp.swapaxes(cache_fox_lf, 2, 3), page_table)
    y_s, hs_s, convs_s, pools_s, ks_s, vs_s, lfs_s = _trunk(
        x_sample.reshape(dec, d), dec, 1, pos_s, cache_mem_k, cache_mem_v,
        state_lru_h, state_lru_conv, state_pool, fox_paged, p)

    xa = (depth, bsz, n_mem, XA_HEADS, d // XA_HEADS)
    return (y_p.reshape(bsz, seq, d), y_s.reshape(dec, 1, d),
            jnp.stack(hs_p), jnp.stack(convs_p), jnp.stack(pools_p), jnp.stack(ks_p), jnp.stack(vs_p),
            jnp.stack(lfs_p), mem_k_p.reshape(xa), mem_v_p.reshape(xa),
            jnp.stack(hs_s), jnp.stack(convs_s), jnp.stack(pools_s), jnp.stack(ks_s), jnp.stack(vs_s),
            jnp.stack(lfs_s))
```

```python
import functools

import jax
import jax.numpy as jnp
from jax import lax
from jax.experimental import pallas as pl
from jax.experimental.pallas import tpu as pltpu

F32 = jnp.float32
BF16 = jnp.bfloat16

RMS_EPS = 1e-6
NEG_INF = -1e30
LRU_C = 8.0
CONV_WIDTH = 4
FOX_HEADS = 16
XA_HEADS = 4
POOL_WINDOWS = (2, 4, 8, 16)
N_MIXERS = 3

LANES = 128
SUBLANES = 8
HALO = 16

_NT = (((1,), (1,)), ((), ()))


def _tile(n, target):
    t = 1
    while t * 2 <= min(n, target):
        t *= 2
    while t > 1 and n % t:
        t //= 2
    return t if (n % t == 0 and t >= SUBLANES) else n


def _params(*sem):
    return pltpu.CompilerParams(dimension_semantics=sem)


def _rmsnorm(x, g):
    x = x.astype(F32)
    x = x * lax.rsqrt(jnp.mean(x * x, axis=-1, keepdims=True) + RMS_EPS)
    return x * g


def _softplus(z):
    return jnp.maximum(z, 0.0) + jnp.log1p(jnp.exp(-jnp.abs(z)))


def _sigmoid(x):
    return 0.5 * jnp.tanh(0.5 * x) + 0.5


def _gelu_tanh(x):
    c = 0.7978845608028654
    return x * (0.5 * (1.0 + jnp.tanh(c * (x + 0.044715 * (x * x * x)))))


def _norm_matmul_kernel(x_ref, g_ref, w_ref, o_ref, xn_ref, *, scale):
    @pl.when(pl.program_id(1) == 0)
    def _():
        xn_ref[...] = _rmsnorm(x_ref[...], g_ref[...]).astype(BF16)

    acc = jnp.dot(xn_ref[...], w_ref[...], preferred_element_type=F32)
    if scale != 1.0:
        acc = acc * scale
    o_ref[...] = acc.astype(o_ref.dtype)


def norm_matmul(x, g, w, *, tn, out_dtype=F32, scale=1.0, name="norm_matmul"):
    m, d = x.shape
    n = w.shape[1]
    tm = _tile(m, 1024)
    return pl.pallas_call(
        functools.partial(_norm_matmul_kernel, scale=scale),
        grid=(m // tm, n // tn),
        in_specs=[
            pl.BlockSpec((tm, d), lambda i, j: (i, 0)),
            pl.BlockSpec((1, d), lambda i, j: (0, 0)),
            pl.BlockSpec((d, tn), lambda i, j: (0, j)),
        ],
        out_specs=pl.BlockSpec((tm, tn), lambda i, j: (i, j)),
        out_shape=jax.ShapeDtypeStruct((m, n), out_dtype),
        scratch_shapes=[pltpu.VMEM((tm, d), BF16)],
        compiler_params=_params("parallel", "arbitrary"),
        name=name,
    )(x, g.reshape(1, d), w)


def _fox_proj_kernel(x_ref, g_ref, w_ref, wf_ref, q_ref, k_ref, v_ref, f_ref, *rest, q_scale, split_heads):
    d = x_ref.shape[1]
    xn = _rmsnorm(x_ref[...], g_ref[...]).astype(BF16)
    q = jnp.dot(xn, w_ref[:, 0:d], preferred_element_type=F32)
    q_ref[...] = (q * q_scale).astype(q_ref.dtype)
    k = jnp.dot(xn, w_ref[:, d:2 * d], preferred_element_type=F32)
    k_ref[...] = k.astype(k_ref.dtype)
    v = jnp.dot(xn, w_ref[:, 2 * d:3 * d], preferred_element_type=F32)
    v_ref[...] = v
    f_ref[...] = jnp.dot(xn, wf_ref[...], preferred_element_type=F32)
    if split_heads:
        k3_ref, v3_ref = rest
        k3_ref[...] = k.reshape(k3_ref.shape)
        v3_ref[...] = v.reshape(v3_ref.shape)


def fox_proj(x, g, w_qkv, w_f, *, q_dtype, q_scale, split_heads):
    m, d = x.shape
    tm = _tile(m, 512)
    dh = d // FOX_HEADS
    row = lambda i: (i, 0)
    out_specs = [pl.BlockSpec((tm, d), row), pl.BlockSpec((tm, d), row), pl.BlockSpec((tm, d), row),
                 pl.BlockSpec((tm, LANES), row)]
    out_shape = [jax.ShapeDtypeStruct((m, d), q_dtype), jax.ShapeDtypeStruct((m, d), BF16 if split_heads else F32),
                 jax.ShapeDtypeStruct((m, d), F32), jax.ShapeDtypeStruct((m, LANES), F32)]
    if split_heads:
        out_specs += [pl.BlockSpec((tm, FOX_HEADS, dh), lambda i: (i, 0, 0))] * 2
        out_shape += [jax.ShapeDtypeStruct((m, FOX_HEADS, dh), F32)] * 2
    return pl.pallas_call(
        functools.partial(_fox_proj_kernel, q_scale=q_scale, split_heads=split_heads),
        grid=(m // tm,),
        in_specs=[
            pl.BlockSpec((tm, d), row),
            pl.BlockSpec((1, d), lambda i: (0, 0)),
            pl.BlockSpec((d, 3 * d), lambda i: (0, 0)),
            pl.BlockSpec((d, LANES), lambda i: (0, 0)),
        ],
        out_specs=out_specs,
        out_shape=out_shape,
        compiler_params=_params("parallel"),
        name="fox_proj",
    )(x, g.reshape(1, d), w_qkv, w_f)


def _mem_kv_kernel(x_ref, g_ref, w_ref, k_ref, v_ref, xn_ref):
    j = pl.program_id(2)

    @pl.when(j == 0)
    def _():
        xn_ref[...] = _rmsnorm(x_ref[...], g_ref[...]).astype(BF16)
        k_ref[...] = jnp.dot(xn_ref[...], w_ref[...], preferred_element_type=F32)

    @pl.when(j == 1)
    def _():
        v_ref[...] = jnp.dot(xn_ref[...], w_ref[...], preferred_element_type=F32)


def mem_kv(mem, g_all, w_all):
    m, d = mem.shape
    depth = g_all.shape[0]
    tm = _tile(m, 512)
    out_spec = pl.BlockSpec((None, tm, d), lambda l, i, j: (l, i, 0))
    return pl.pallas_call(
        _mem_kv_kernel,
        grid=(depth, m // tm, 2),
        in_specs=[
            pl.BlockSpec((tm, d), lambda l, i, j: (i, 0)),
            pl.BlockSpec((None, 1, d), lambda l, i, j: (l, 0, 0)),
            pl.BlockSpec((None, d, d), lambda l, i, j: (l, 0, j)),
        ],
        out_specs=[out_spec, out_spec],
        out_shape=[jax.ShapeDtypeStruct((depth, m, d), F32)] * 2,
        scratch_shapes=[pltpu.VMEM((tm, d), BF16)],
        compiler_params=_params("parallel", "parallel", "arbitrary"),
        name="mem_kv",
    )(mem, g_all.reshape(depth, 1, d), w_all)


def _matmul_res_kernel(h_ref, w_ref, r_ref, o_ref):
    o_ref[...] = r_ref[...] + jnp.dot(h_ref[...].astype(BF16), w_ref[...], preferred_element_type=F32)


def matmul_res(h, w, res, *, name="matmul_res"):
    m, k = h.shape
    n = w.shape[1]
    tm = _tile(m, 512)
    return pl.pallas_call(
        _matmul_res_kernel,
        grid=(m // tm,),
        in_specs=[
            pl.BlockSpec((tm, k), lambda i: (i, 0)),
            pl.BlockSpec((k, n), lambda i: (0, 0)),
            pl.BlockSpec((tm, n), lambda i: (i, 0)),
        ],
        out_specs=pl.BlockSpec((tm, n), lambda i: (i, 0)),
        out_shape=jax.ShapeDtypeStruct((m, n), F32),
        compiler_params=_params("parallel"),
        name=name,
    )(h, w, res)


def _mlp_kernel(*refs, final_norm):
    if final_norm:
        x_ref, g_ref, wu_ref, wd_ref, gf_ref, o_ref, y_ref, xn_ref, acc_ref = refs
    else:
        x_ref, g_ref, wu_ref, wd_ref, o_ref, xn_ref, acc_ref = refs
    j = pl.program_id(1)

    @pl.when(j == 0)
    def _():
        xn_ref[...] = _rmsnorm(x_ref[...], g_ref[...]).astype(BF16)
        acc_ref[...] = jnp.zeros_like(acc_ref)

    h = jnp.dot(xn_ref[...], wu_ref[...], preferred_element_type=F32)
    h = jnp.square(jnp.maximum(h, 0.0)).astype(BF16)
    acc_ref[...] += jnp.dot(h, wd_ref[...], preferred_element_type=F32)

    @pl.when(j == pl.num_programs(1) - 1)
    def _():
        out = x_ref[...] + acc_ref[...]
        o_ref[...] = out
        if final_norm:
            y_ref[...] = _rmsnorm(out, gf_ref[...])


def mlp(x, g, w_up, w_down, final_g=None):
    m, d = x.shape
    f = w_up.shape[1]
    tm = _tile(m, 1024)
    tf = _tile(f, 1024)
    row = lambda i, j: (i, 0)
    vec = pl.BlockSpec((1, d), lambda i, j: (0, 0))
    in_specs = [
        pl.BlockSpec((tm, d), row),
        vec,
        pl.BlockSpec((d, tf), lambda i, j: (0, j)),
        pl.BlockSpec((tf, d), lambda i, j: (j, 0)),
    ]
    args = [x, g.reshape(1, d), w_up, w_down]
    out_specs = [pl.BlockSpec((tm, d), row)]
    out_shape = [jax.ShapeDtypeStruct((m, d), F32)]
    if final_g is not None:
        in_specs.append(vec)
        args.append(final_g.reshape(1, d))
        out_specs.append(pl.BlockSpec((tm, d), row))
        out_shape.append(jax.ShapeDtypeStruct((m, d), F32))
    outs = pl.pallas_call(
        functools.partial(_mlp_kernel, final_norm=final_g is not None),
        grid=(m // tm, f // tf),
        in_specs=in_specs,
        out_specs=out_specs,
        out_shape=out_shape,
        scratch_shapes=[pltpu.VMEM((tm, d), BF16), pltpu.VMEM((tm, d), F32)],
        compiler_params=_params("parallel", "arbitrary"),
        name="mlp",
    )(*args)
    return outs if final_g is not None else (outs[0], None)


MEM_ATTN_SEQS = 4


def _mem_attn_kernel(q_ref, k_ref, v_ref, o_ref):
    nb, rows, d = q_ref.shape
    nh = XA_HEADS
    dh = d // nh
    n_rows = k_ref.shape[1] * nh
    own = (lax.broadcasted_iota(jnp.int32, (nh * rows, n_rows), 0) // rows
           == lax.broadcasted_iota(jnp.int32, (nh * rows, n_rows), 1) % nh)
    qs = [jnp.concatenate([q_ref[n, :, h * dh:(h + 1) * dh] for h in range(nh)], axis=0) for n in range(nb)]
    kf = [k_ref[n].reshape(n_rows, dh).astype(BF16) for n in range(nb)]
    vf = [v_ref[n].reshape(n_rows, dh).astype(BF16) for n in range(nb)]
    scores = [lax.dot_general(qs[n], kf[n], _NT, preferred_element_type=F32) for n in range(nb)]
    probs = []
    for s in scores:
        s = jnp.where(own, s, NEG_INF)
        e = jnp.exp(s - jnp.max(s, axis=1, keepdims=True))
        probs.append((e * (1.0 / jnp.sum(e, axis=1, keepdims=True))).astype(BF16))
    for n in range(nb):
        o = jnp.dot(probs[n], vf[n], preferred_element_type=F32)
        for h in range(nh):
            o_ref[n, :, h * dh:(h + 1) * dh] = o[h * rows:(h + 1) * rows, :].astype(o_ref.dtype)


def mem_attn(q, k_all, v_all, layer):
    bsz, t, d = q.shape
    nb = MEM_ATTN_SEQS if bsz % MEM_ATTN_SEQS == 0 else 1
    kv_spec = pl.BlockSpec((None, nb) + k_all.shape[2:], lambda b: (layer, b, 0, 0, 0))
    q_spec = pl.BlockSpec((nb, t, d), lambda b: (b, 0, 0))
    return pl.pallas_call(
        _mem_attn_kernel,
        grid=(bsz // nb,),
        in_specs=[q_spec, kv_spec, kv_spec],
        out_specs=q_spec,
        out_shape=jax.ShapeDtypeStruct((bsz, t, d), BF16),
        compiler_params=_params("parallel"),
        name="mem_attn",
    )(q, k_all, v_all)


def _xattn_kernel(x_ref, g_ref, wq_ref, k_ref, v_ref, wo_ref, o_ref, q_scr, oh_scr, *, scale):
    d = x_ref.shape[1]
    dh = d // XA_HEADS
    x = x_ref[...]
    xn = _rmsnorm(x, g_ref[...]).astype(BF16)
    q_scr[...] = (jnp.dot(xn, wq_ref[...], preferred_element_type=F32) * scale).astype(BF16)
    heads = [slice(h * dh, (h + 1) * dh) for h in range(XA_HEADS)]
    scores = [lax.dot_general(q_scr[:, sl], k_ref[:, sl].astype(BF16), _NT, preferred_element_type=F32)
              for sl in heads]
    probs = []
    for s in scores:
        e = jnp.exp(s - jnp.max(s, axis=1, keepdims=True))
        probs.append((e * (1.0 / jnp.sum(e, axis=1, keepdims=True))).astype(BF16))
    for sl, p in zip(heads, probs):
        oh_scr[:, sl] = jnp.dot(p, v_ref[:, sl].astype(BF16), preferred_element_type=F32).astype(BF16)
    o_ref[...] = x + jnp.dot(oh_scr[...], wo_ref[...], preferred_element_type=F32)


def xattn(x, g, w_q, k_all, v_all, w_o, layer):
    bsz, t, d = x.shape
    n_mem = k_all.shape[2]
    tm = _tile(t, 512)
    kv_spec = pl.BlockSpec((None, None, n_mem, d), lambda b, i: (layer, b, 0, 0))
    w_spec = pl.BlockSpec((d, d), lambda b, i: (0, 0))
    x_spec = pl.BlockSpec((None, tm, d), lambda b, i: (b, i, 0))
    vec = pl.BlockSpec((1, d), lambda b, i: (0, 0))
    return pl.pallas_call(
        functools.partial(_xattn_kernel, scale=(d // XA_HEADS) ** -0.5),
        grid=(bsz, t // tm),
        in_specs=[x_spec, vec, w_spec, kv_spec, kv_spec, w_spec],
        out_specs=x_spec,
        out_shape=jax.ShapeDtypeStruct((bsz, t, d), F32),
        scratch_shapes=[pltpu.VMEM((tm, d), BF16), pltpu.VMEM((tm, d), BF16)],
        compiler_params=_params("parallel", "parallel"),
        name="xattn",
    )(x, g.reshape(1, d), w_q, k_all, v_all, w_o)


def _lru_gate_logits(uc, wg_ref):
    half = uc.shape[1] // 2
    ucb = uc.astype(BF16)
    g0 = jnp.dot(ucb[:, :half], wg_ref[0], preferred_element_type=F32)
    g1 = jnp.dot(ucb[:, half:], wg_ref[1], preferred_element_type=F32)
    rg = jnp.concatenate([g0[:, :half], g1[:, :half]], axis=1)
    ig = jnp.concatenate([g0[:, half:], g1[:, half:]], axis=1)
    return rg, ig


def _lru_coeffs(rg, ig, lam):
    r = _sigmoid(rg)
    i = _sigmoid(ig)
    log_a = (-LRU_C * r) * _softplus(-lam)
    a = jnp.exp(log_a)
    mult = jnp.sqrt(-jnp.tanh(log_a) * (a * a + 1.0))
    return a, mult, i


def _scan8(a8, x8, row8):
    for s in (1, 2, 4):
        keep = row8 >= s
        xs = jnp.where(keep, pltpu.roll(x8, s, 0), 0.0)
        a_s = jnp.where(keep, pltpu.roll(a8, s, 0), 1.0)
        x8 = x8 + a8 * xs
        a8 = a8 * a_s
    return a8, x8


LRU_PARTS = 2


def _lru_prompt_kernel(x_ref, g_ref, win_ref, buf_ref, h0_ref, cw_ref, cb_ref, wg_ref, brg_ref, big_ref, lam_ref,
                       wout_ref, o_ref, hl_ref, cbuf_ref, ufull, x_scr, gate_scr, hc, *, first_at_zero):
    i = pl.program_id(1)
    tt = x_ref.shape[0]
    c = wout_ref.shape[0]
    pad = SUBLANES

    @pl.when(i == 0)
    def _():
        ufull[0:pad, :] = buf_ref[...]
        hc[...] = h0_ref[...]

    @pl.when(i > 0)
    def _():
        ufull[0:pad, :] = ufull[tt:tt + pad, :]

    hr = tt // LRU_PARTS
    parts = [slice(k * hr, (k + 1) * hr) for k in range(LRU_PARTS)]
    xns = [_rmsnorm(x_ref[rows, :], g_ref[...]).astype(BF16) for rows in parts]
    gus = [jnp.dot(xn, win_ref[...], preferred_element_type=F32) for xn in xns]
    ucs = []
    for rows, gu in zip(parts, gus):
        gate_scr[rows, :] = _gelu_tanh(gu[:, :c])
        u = gu[:, c:]
        ufull[pad + rows.start:pad + rows.stop, :] = u
        uc = cb_ref[...]
        for k in range(CONV_WIDTH - 1):
            off = pad - (CONV_WIDTH - 1) + k
            uc = uc + ufull[off + rows.start:off + rows.stop, :] * cw_ref[k:k + 1, :]
        ucs.append(uc + u * cw_ref[CONV_WIDTH - 1:CONV_WIDTH, :])
    logits = [_lru_gate_logits(uc, wg_ref) for uc in ucs]

    row8 = lax.broadcasted_iota(jnp.int32, (SUBLANES, c), 0)
    h = hc[...]
    for rows, uc, (rg, ig) in zip(parts, ucs, logits):
        a, mult, ig = _lru_coeffs(rg + brg_ref[...], ig + big_ref[...], lam_ref[...])
        if first_at_zero and rows.start == 0:
            row = lax.broadcasted_iota(jnp.int32, (hr, 1), 0)
            mult = jnp.where(jnp.logical_and(row == 0, i == 0), 1.0, mult)
        xin = (mult * ig) * uc
        for grp in range(hr // SUBLANES):
            sub = slice(grp * SUBLANES, (grp + 1) * SUBLANES)
            a8, x8 = _scan8(a[sub, :], xin[sub, :], row8)
            h8 = x8 + a8 * h
            x_scr[rows.start + sub.start:rows.start + sub.stop, :] = h8
            h = h8[SUBLANES - 1:SUBLANES, :]
        y = (gate_scr[rows, :] * x_scr[rows, :]).astype(BF16)
        o_ref[rows, :] = x_ref[rows, :] + jnp.dot(y, wout_ref[...], preferred_element_type=F32)
    hc[...] = h

    @pl.when(i == pl.num_programs(1) - 1)
    def _():
        hl_ref[...] = h
        cbuf_ref[...] = ufull[tt:tt + pad, :]


def lru_prompt(x, g, w_in, conv_buf, h0, conv_w, conv_b, wg, b_rg, b_ig, lam, w_out, *, pos0):
    bsz, t, d = x.shape
    c = w_out.shape[0]
    tt = _tile(t, 256)
    pad = SUBLANES
    buf8 = jnp.pad(conv_buf, ((0, 0), (pad - (CONV_WIDTH - 1), 0), (0, 0)))
    vec = pl.BlockSpec((1, c), lambda b, i: (0, 0))
    out, h_last, cbuf = pl.pallas_call(
        functools.partial(_lru_prompt_kernel, first_at_zero=(pos0 == 0)),
        grid=(bsz, t // tt),
        in_specs=[
            pl.BlockSpec((None, tt, d), lambda b, i: (b, i, 0)),
            pl.BlockSpec((1, d), lambda b, i: (0, 0)),
            pl.BlockSpec((d, 2 * c), lambda b, i: (0, 0)),
            pl.BlockSpec((None, pad, c), lambda b, i: (b, 0, 0)),
            pl.BlockSpec((None, 1, c), lambda b, i: (b, 0, 0)),
            pl.BlockSpec((CONV_WIDTH, c), lambda b, i: (0, 0)),
            vec,
            pl.BlockSpec((2, c // 2, c), lambda b, i: (0, 0, 0)),
            vec, vec, vec,
            pl.BlockSpec((c, d), lambda b, i: (0, 0)),
        ],
        out_specs=[
            pl.BlockSpec((None, tt, d), lambda b, i: (b, i, 0)),
            pl.BlockSpec((None, 1, c), lambda b, i: (b, 0, 0)),
            pl.BlockSpec((None, pad, c), lambda b, i: (b, 0, 0)),
        ],
        out_shape=[
            jax.ShapeDtypeStruct((bsz, t, d), F32),
            jax.ShapeDtypeStruct((bsz, 1, c), F32),
            jax.ShapeDtypeStruct((bsz, pad, c), F32),
        ],
        scratch_shapes=[
            pltpu.VMEM((tt + pad, c), F32),
            pltpu.VMEM((tt, c), F32),
            pltpu.VMEM((tt, c), F32),
            pltpu.VMEM((1, c), F32),
        ],
        compiler_params=_params("parallel", "arbitrary"),
        name="lru_prompt",
    )(x, g.reshape(1, d), w_in, buf8, h0.reshape(bsz, 1, c), conv_w, conv_b.reshape(1, c), wg,
      b_rg.reshape(1, c), b_ig.reshape(1, c), lam.reshape(1, c), w_out)
    return out, h_last.reshape(bsz, c), cbuf[:, pad - (CONV_WIDTH - 1):, :]


def _lru_step_kernel(gate_ref, u_ref, buf_ref, h0_ref, cw_ref, cb_ref, wg_ref, brg_ref, big_ref, lam_ref,
                     y_ref, h_ref, *, first_at_zero):
    u = u_ref[...]
    uc = cb_ref[...]
    for k in range(CONV_WIDTH - 1):
        uc = uc + buf_ref[k] * cw_ref[k:k + 1, :]
    uc = uc + u * cw_ref[CONV_WIDTH - 1:CONV_WIDTH, :]
    rg, ig = _lru_gate_logits(uc, wg_ref)
    a, mult, ig = _lru_coeffs(rg + brg_ref[...], ig + big_ref[...], lam_ref[...])
    if first_at_zero:
        mult = jnp.ones_like(mult)
    h = a * h0_ref[...] + (mult * ig) * uc
    h_ref[...] = h
    y_ref[...] = (_gelu_tanh(gate_ref[...]) * h).astype(y_ref.dtype)


def lru_step(gu, conv_buf, h0, conv_w, conv_b, wg, b_rg, b_ig, lam, *, pos0):
    bsz, c2 = gu.shape
    c = c2 // 2
    buf_t = jnp.swapaxes(conv_buf, 0, 1)
    full = lambda *shape: pl.BlockSpec(shape, lambda i: (0,) * len(shape))
    y, h = pl.pallas_call(
        functools.partial(_lru_step_kernel, first_at_zero=(pos0 == 0)),
        grid=(1,),
        in_specs=[
            pl.BlockSpec((bsz, c), lambda i: (0, 0)),
            pl.BlockSpec((bsz, c), lambda i: (0, 1)),
            full(CONV_WIDTH - 1, bsz, c),
            full(bsz, c),
            full(CONV_WIDTH, c),
            full(1, c),
            full(2, c // 2, c),
            full(1, c), full(1, c), full(1, c),
        ],
        out_specs=[full(bsz, c), full(bsz, c)],
        out_shape=[jax.ShapeDtypeStruct((bsz, c), BF16), jax.ShapeDtypeStruct((bsz, c), F32)],
        compiler_params=_params("arbitrary"),
        name="lru_step",
    )(gu, gu, buf_t, h0, conv_w, conv_b.reshape(1, c), wg,
      b_rg.reshape(1, c), b_ig.reshape(1, c), lam.reshape(1, c))
    new_buf = jnp.concatenate([conv_buf[:, 1:], gu[:, None, c:]], axis=1)
    return y, h, new_buf


def _pool_groups(xn, shifted, cnt, w_ref, b, scale):
    d = xn.shape[1]
    gw = d // len(POOL_WINDOWS)
    ys = []
    for gi, win in enumerate(POOL_WINDOWS):
        ch = slice(gi * gw, (gi + 1) * gw)
        s = xn[:, ch]
        for k in range(1, win):
            s = s + shifted(k, ch)
        dd = (s / cnt(win) - xn[:, ch]).astype(BF16)
        ys.append(jnp.dot(dd, w_ref[gi], preferred_element_type=F32))
    return (jnp.concatenate(ys, axis=1) + b) * scale


def _pool_prompt_kernel(x_ref, buf_ref, g_ref, w_ref, b_ref, sc_ref, o_ref, nb_ref, full, *, pos0):
    i = pl.program_id(1)
    tm, d = x_ref.shape

    @pl.when(i == 0)
    def _():
        full[0:HALO, :] = buf_ref[...]

    @pl.when(i > 0)
    def _():
        full[0:HALO, :] = full[tm:tm + HALO, :]

    x = x_ref[...]
    xn = _rmsnorm(x, g_ref[...])
    full[HALO:HALO + tm, :] = xn
    pos = pos0 + i * tm + lax.broadcasted_iota(jnp.int32, (tm, 1), 0)
    y = _pool_groups(
        xn,
        lambda k, ch: full[HALO - k:HALO - k + tm, ch],
        lambda win: jnp.minimum(pos + 1, win).astype(F32),
        w_ref, b_ref[...], sc_ref[...])
    o_ref[...] = x + y

    @pl.when(i == pl.num_programs(1) - 1)
    def _():
        nb_ref[...] = full[tm:tm + HALO, :]


def pool_prompt(x, buf, g, w, b, scale, *, pos0):
    bsz, t, d = x.shape
    nbuf = buf.shape[1]
    tm = _tile(t, 512)
    buf16 = jnp.pad(buf, ((0, 0), (HALO - nbuf, 0), (0, 0)))
    vec = pl.BlockSpec((1, d), lambda b_, i: (0, 0))
    ng = len(POOL_WINDOWS)
    out, nb = pl.pallas_call(
        functools.partial(_pool_prompt_kernel, pos0=pos0),
        grid=(bsz, t // tm),
        in_specs=[
            pl.BlockSpec((None, tm, d), lambda b_, i: (b_, i, 0)),
            pl.BlockSpec((None, HALO, d), lambda b_, i: (b_, 0, 0)),
            vec,
            pl.BlockSpec((ng, d // ng, d // ng), lambda b_, i: (0, 0, 0)),
            vec, vec,
        ],
        out_specs=[
            pl.BlockSpec((None, tm, d), lambda b_, i: (b_, i, 0)),
            pl.BlockSpec((None, HALO, d), lambda b_, i: (b_, 0, 0)),
        ],
        out_shape=[jax.ShapeDtypeStruct((bsz, t, d), F32), jax.ShapeDtypeStruct((bsz, HALO, d), F32)],
        scratch_shapes=[pltpu.VMEM((tm + HALO, d), F32)],
        compiler_params=_params("parallel", "arbitrary"),
        name="pool_prompt",
    )(x, buf16, g.reshape(1, d), w, b.reshape(1, d), scale.reshape(1, d))
    return out, nb[:, HALO - nbuf:, :]


def _pool_step_kernel(x_ref, buf_ref, g_ref, w_ref, b_ref, sc_ref, o_ref, xn_ref, *, pos0):
    x = x_ref[...]
    xn = _rmsnorm(x, g_ref[...])
    nbuf = buf_ref.shape[0]
    y = _pool_groups(
        xn,
        lambda k, ch: buf_ref[nbuf - k, :, ch],
        lambda win: float(min(pos0 + 1, win)),
        w_ref, b_ref[...], sc_ref[...])
    o_ref[...] = x + y
    xn_ref[...] = xn


def pool_step(x, buf, g, w, b, scale, *, pos0):
    bsz, d = x.shape
    nbuf = buf.shape[1]
    ng = len(POOL_WINDOWS)
    buf_t = jnp.swapaxes(buf, 0, 1)
    full = lambda *shape: pl.BlockSpec(shape, lambda i: (0,) * len(shape))
    out, xn = pl.pallas_call(
        functools.partial(_pool_step_kernel, pos0=pos0),
        grid=(1,),
        in_specs=[full(bsz, d), full(nbuf, bsz, d), full(1, d), full(ng, d // ng, d // ng), full(1, d), full(1, d)],
        out_specs=[full(bsz, d), full(bsz, d)],
        out_shape=[jax.ShapeDtypeStruct((bsz, d), F32)] * 2,
        compiler_params=_params("arbitrary"),
        name="pool_step",
    )(x, buf_t, g.reshape(1, d), w, b.reshape(1, d), scale.reshape(1, d))
    return out, jnp.concatenate([buf[:, 1:], xn[:, None, :]], axis=1)


def _lf_cumsum_kernel(fl_ref, b_ref, lf_ref, cum_ref, cumt_ref):
    t, w = fl_ref.shape
    lf_ref[...] = -_softplus(-(fl_ref[...] + b_ref[...]))
    row8 = lax.broadcasted_iota(jnp.int32, (SUBLANES, w), 0)

    def body(r, carry):
        off = pl.multiple_of(r * SUBLANES, SUBLANES)
        x8 = lf_ref[pl.ds(off, SUBLANES), :]
        for s in (1, 2, 4):
            x8 = x8 + jnp.where(row8 >= s, pltpu.roll(x8, s, 0), 0.0)
        c8 = x8 + carry
        cum_ref[pl.ds(off, SUBLANES), :] = c8
        return c8[SUBLANES - 1:SUBLANES, :]

    lax.fori_loop(0, t // SUBLANES, body, jnp.zeros((1, w), F32))
    cumt_ref[...] = cum_ref[...].T


def lf_cumsum(fl, b_pad):
    bsz, t, w = fl.shape
    blk = pl.BlockSpec((None, t, w), lambda b: (b, 0, 0))
    return pl.pallas_call(
        _lf_cumsum_kernel,
        grid=(bsz,),
        in_specs=[blk, pl.BlockSpec((1, w), lambda b: (0, 0))],
        out_specs=[blk, blk, pl.BlockSpec((None, w, t), lambda b: (b, 0, 0))],
        out_shape=[jax.ShapeDtypeStruct((bsz, t, w), F32)] * 2 + [jax.ShapeDtypeStruct((bsz, w, t), F32)],
        compiler_params=_params("parallel"),
        name="fox_lf_cumsum",
    )(fl, b_pad)


def _lf_kernel(fl_ref, b_ref, lf_ref):
    lf_ref[...] = -_softplus(-(fl_ref[...] + b_ref[...]))


def lf_only(fl, b_pad):
    m, w = fl.shape
    return pl.pallas_call(
        _lf_kernel,
        grid=(1,),
        in_specs=[pl.BlockSpec((m, w), lambda i: (0, 0)), pl.BlockSpec((1, w), lambda i: (0, 0))],
        out_specs=pl.BlockSpec((m, w), lambda i: (0, 0)),
        out_shape=jax.ShapeDtypeStruct((m, w), F32),
        name="fox_lf",
    )(fl, b_pad)


FLASH_HEADS = 8


def _fox_flash_kernel(qi_tab, ki_tab, q_ref, k_ref, v_ref, fq_ref, ck_ref, o_ref,
                      q2_ref, vt_ref, m_ref, l_ref, acc_ref, *, tq, hb):
    hg = pl.program_id(1)
    pair = pl.program_id(2)
    qi = qi_tab[pair]
    ki = ki_tab[pair]
    w = q_ref.shape[1]
    dh = w // hb

    @pl.when(ki == 0)
    def _():
        m_ref[...] = jnp.full_like(m_ref, -jnp.inf)
        l_ref[...] = jnp.zeros_like(l_ref)
        acc_ref[...] = jnp.zeros_like(acc_ref)
        q = q_ref[...].astype(F32)
        head_of_lane = lax.broadcasted_iota(jnp.int32, (tq, w), 1) // dh
        for hh in range(hb):
            q2_ref[hh] = jnp.where(head_of_lane == hh, q, 0.0).astype(BF16)

    vt_ref[...] = v_ref[...].T.astype(BF16)
    ck = pltpu.roll(ck_ref[...], (LANES - hg * hb) % LANES, 1)
    fk_cols = [ck[:, hh:hh + 1] for hh in range(hb)]

    def pair_update(masked):
        scores = [lax.dot_general(k_ref[...], q2_ref[hh], _NT, preferred_element_type=F32) for hh in range(hb)]
        if masked:
            keep = (lax.broadcasted_iota(jnp.int32, (tq, tq), 1) >= lax.broadcasted_iota(jnp.int32, (tq, tq), 0))
        for hh in range(hb):
            t = scores[hh] - fk_cols[hh]
            if masked:
                t = jnp.where(keep, t, NEG_INF)
            fq = fq_ref[hh:hh + 1, :]
            m_prev = m_ref[hh:hh + 1, :]
            m_new = jnp.maximum(m_prev, jnp.max(t, axis=0, keepdims=True) + fq)
            alpha = jnp.exp(m_prev - m_new)
            p = jnp.exp(t + (fq - m_new))
            l_ref[hh:hh + 1, :] = alpha * l_ref[hh:hh + 1, :] + jnp.sum(p, axis=0, keepdims=True)
            m_ref[hh:hh + 1, :] = m_new
            ch = slice(hh * dh, (hh + 1) * dh)
            pv = jnp.dot(vt_ref[ch, :], p.astype(BF16), preferred_element_type=F32)
            acc_ref[ch, :] = alpha * acc_ref[ch, :] + pv

    @pl.when(ki < qi)
    def _():
        pair_update(False)

    @pl.when(ki == qi)
    def _():
        pair_update(True)
        for hh in range(hb):
            ch = slice(hh * dh, (hh + 1) * dh)
            acc_ref[ch, :] = acc_ref[ch, :] * (1.0 / l_ref[hh:hh + 1, :])
        o_ref[...] = acc_ref[...].T.astype(o_ref.dtype)


def fox_flash(q, k, v, cum, cum_t):
    bsz, t, d = q.shape
    tq = _tile(t, 512)
    hb = FLASH_HEADS
    w = hb * (d // FOX_HEADS)
    nq = t // tq
    pairs = [(qi, ki) for qi in range(nq) for ki in range(qi + 1)]
    qi_tab = jnp.asarray([pq for pq, _ in pairs], jnp.int32)
    ki_tab = jnp.asarray([pk for _, pk in pairs], jnp.int32)
    cum_rows = cum_t[:, :FOX_HEADS, :].reshape(bsz, FOX_HEADS // hb, hb, t)
    q_spec = pl.BlockSpec((None, tq, w), lambda b, hg, p, qt, kt: (b, qt[p], hg))
    kv_spec = pl.BlockSpec((None, tq, w), lambda b, hg, p, qt, kt: (b, kt[p], hg))
    grid_spec = pltpu.PrefetchScalarGridSpec(
        num_scalar_prefetch=2,
        grid=(bsz, d // w, len(pairs)),
        in_specs=[
            q_spec, kv_spec, kv_spec,
            pl.BlockSpec((None, None, hb, tq), lambda b, hg, p, qt, kt: (b, hg, 0, qt[p])),
            pl.BlockSpec((None, tq, LANES), lambda b, hg, p, qt, kt: (b, kt[p], 0)),
        ],
        out_specs=q_spec,
        scratch_shapes=[
            pltpu.VMEM((hb, tq, w), BF16),
            pltpu.VMEM((w, tq), BF16),
            pltpu.VMEM((hb, tq), F32),
            pltpu.VMEM((hb, tq), F32),
            pltpu.VMEM((w, tq), F32),
        ],
    )
    return pl.pallas_call(
        functools.partial(_fox_flash_kernel, tq=tq, hb=hb),
        grid_spec=grid_spec,
        out_shape=jax.ShapeDtypeStruct((bsz, t, d), BF16),
        compiler_params=_params("parallel", "parallel", "arbitrary"),
        name="fox_flash",
    )(qi_tab, ki_tab, q, k, v, cum_rows, cum)


DECODE_PAGES = 8


def _fox_decode_kernel(pt_ref, q_ref, kn_ref, vn_ref, lfn_ref, *refs):
    g = DECODE_PAGES
    k_refs, v_refs, lft_refs = refs[0:g], refs[g:2 * g], refs[2 * g:3 * g]
    o_ref, qm_ref, m_ref, l_ref, acc_ref, carry_ref = refs[3 * g:]
    j = pl.program_id(1)
    d = q_ref.shape[1]
    nh = FOX_HEADS
    dh = d // nh
    page = k_refs[0].shape[1]
    head = lax.broadcasted_iota(jnp.int32, (nh, d), 0)
    own = head == lax.broadcasted_iota(jnp.int32, (nh, d), 1) // dh

    @pl.when(j == 0)
    def _():
        qm_ref[...] = jnp.where(own, jnp.broadcast_to(q_ref[...], (nh, d)), 0.0).astype(BF16)
        m_ref[...] = jnp.full_like(m_ref, -jnp.inf)
        l_ref[...] = jnp.zeros_like(l_ref)
        acc_ref[...] = jnp.zeros_like(acc_ref)
        carry_ref[...] = jnp.zeros_like(carry_ref)

    hl = lax.broadcasted_iota(jnp.int32, (nh, LANES), 0) == lax.broadcasted_iota(jnp.int32, (nh, LANES), 1)
    lf_new = jnp.sum(jnp.where(hl, jnp.broadcast_to(lfn_ref[...], (nh, LANES)), 0.0), axis=1, keepdims=True)
    later = (lax.broadcasted_iota(jnp.int32, (page, page), 0)
             > lax.broadcasted_iota(jnp.int32, (page, page), 1)).astype(F32)

    qm = qm_ref[...]
    scores = [jnp.dot(qm, k_ref[...].astype(BF16), preferred_element_type=F32) for k_ref in k_refs]
    lfts = [lft_ref[...] for lft_ref in lft_refs]
    within = jnp.dot(jnp.concatenate(lfts, axis=0), later, preferred_element_type=F32,
                     precision=lax.Precision.HIGHEST)
    carry = carry_ref[...]
    logits = []
    for i in range(g):
        logits.append(scores[i] + lf_new + (within[i * nh:(i + 1) * nh, :] + carry))
        carry = carry + jnp.sum(lfts[i], axis=1, keepdims=True)
    carry_ref[...] = carry
    s = jnp.concatenate(logits, axis=1)
    m_prev = m_ref[...]
    m_new = jnp.maximum(m_prev, jnp.max(s, axis=1, keepdims=True))
    alpha = jnp.exp(m_prev - m_new)
    p = jnp.exp(s - m_new)
    l_ref[...] = alpha * l_ref[...] + jnp.sum(p, axis=1, keepdims=True)
    m_ref[...] = m_new
    pb = p.astype(BF16)
    pv = lax.dot_general(pb[:, 0:page], v_refs[0][...].astype(BF16), _NT, preferred_element_type=F32)
    for i in range(1, g):
        pv = pv + lax.dot_general(pb[:, i * page:(i + 1) * page], v_refs[i][...].astype(BF16), _NT,
                                  preferred_element_type=F32)
    acc_ref[...] = alpha * acc_ref[...] + pv

    @pl.when(j == pl.num_programs(1) - 1)
    def _():
        kn = jnp.broadcast_to(kn_ref[...].astype(BF16).astype(F32), (nh, d))
        vn = jnp.broadcast_to(vn_ref[...].astype(BF16).astype(F32), (nh, d))
        s_new = jnp.sum(qm_ref[...].astype(F32) * kn, axis=1, keepdims=True)
        m_prev = m_ref[...]
        m_fin = jnp.maximum(m_prev, s_new)
        alpha = jnp.exp(m_prev - m_fin)
        p_new = jnp.exp(s_new - m_fin)
        l_fin = alpha * l_ref[...] + p_new
        acc = alpha * acc_ref[...] + p_new.astype(BF16).astype(F32) * vn
        o_ref[...] = jnp.sum(jnp.where(own, acc * (1.0 / l_fin), 0.0), axis=0, keepdims=True)


def fox_decode(q, k_new, v_new, lf_new, kt_pool, vt_pool, lft_pool, page_table, layer):
    bsz, d = q.shape
    n_pages = page_table.shape[1]
    page = kt_pool.shape[3]
    nh = FOX_HEADS
    g = DECODE_PAGES
    assert n_pages % g == 0
    row = pl.BlockSpec((None, 1, d), lambda b, j, pt: (b, 0, 0))

    def page_idx(i):
        return lambda b, j, pt: (layer, pt[b * n_pages + (n_pages - 1 - (j * g + i))], 0, 0)

    grid_spec = pltpu.PrefetchScalarGridSpec(
        num_scalar_prefetch=1,
        grid=(bsz, n_pages // g),
        in_specs=[row, row, row, pl.BlockSpec((None, 1, LANES), lambda b, j, pt: (b, 0, 0))]
        + [pl.BlockSpec((None, None, d, page), page_idx(i)) for i in range(g)]
        + [pl.BlockSpec((None, None, d, page), page_idx(i)) for i in range(g)]
        + [pl.BlockSpec((None, None, nh, page), page_idx(i)) for i in range(g)],
        out_specs=row,
        scratch_shapes=[
            pltpu.VMEM((nh, d), BF16),
            pltpu.VMEM((nh, 1), F32),
            pltpu.VMEM((nh, 1), F32),
            pltpu.VMEM((nh, d), F32),
            pltpu.VMEM((nh, 1), F32),
        ],
    )
    out = pl.pallas_call(
        _fox_decode_kernel,
        grid_spec=grid_spec,
        out_shape=jax.ShapeDtypeStruct((bsz, 1, d), F32),
        compiler_params=_params("parallel", "arbitrary"),
        name="fox_decode",
    )(page_table.reshape(-1), q.reshape(bsz, 1, d), k_new.reshape(bsz, 1, d), v_new.reshape(bsz, 1, d),
      lf_new.reshape(bsz, 1, LANES), *([kt_pool] * g + [vt_pool] * g + [lft_pool] * g))
    return out.reshape(bsz, d)


def _block_diag_gate_weights(w_rg, w_ig):
    nb, bw, _ = w_rg.shape
    half = nb // 2
    assert (half * bw) % LANES == 0
    eye = jnp.eye(half, dtype=w_rg.dtype)

    def dense(w):
        return (w[:, :, None, :] * eye[:, None, :, None]).reshape(half * bw, half * bw)

    return jnp.stack([
        jnp.concatenate([dense(w_rg[c * half:(c + 1) * half]), dense(w_ig[c * half:(c + 1) * half])], axis=1)
        for c in range(2)]).astype(BF16)


def _trunk(x, bsz, t, pos0, mem_k, mem_v, lru_h, lru_conv, pool_buf, fox_paged, p):
    d = x.shape[1]
    depth = p["norm_mix_g"].shape[0]
    hs, convs, pools, ks, vs, lfs = [], [], [], [], [], []
    y = None
    for layer in range(depth):
        kind, j = layer % N_MIXERS, layer // N_MIXERS
        g_mix = p["norm_mix_g"][layer]
        if kind == 0:
            c = p["w_lru_out"].shape[1]
            args = (p["lru_conv_w"][j], p["lru_conv_b"][j], p["lru_wg"][j], p["lru_b_rg"][j], p["lru_b_ig"][j],
                    p["lru_lambda"][j])
            if t > 1:
                x3, hl, cb = lru_prompt(x.reshape(bsz, t, d), g_mix, p["w_lru_in"][j], lru_conv[j], lru_h[j],
                                        *args, p["w_lru_out"][j], pos0=pos0)
                x = x3.reshape(bsz * t, d)
            else:
                gu = norm_matmul(x, g_mix, p["w_lru_in"][j], tn=c, name="lru_in")
                yl, hl, cb = lru_step(gu, lru_conv[j], lru_h[j], *args, pos0=pos0)
                x = matmul_res(yl, p["w_lru_out"][j], x, name="lru_out")
            hs.append(hl)
            convs.append(cb)
        elif kind == 1:
            dh = d // FOX_HEADS
            if fox_paged is None:
                q, k, v, fl, k3, v3 = fox_proj(x, g_mix, p["w_fox_qkv"][j], p["w_fox_f"][j], q_dtype=BF16,
                                               q_scale=dh ** -0.5, split_heads=True)
                lf, cum, cum_t = lf_cumsum(fl.reshape(bsz, t, LANES), p["b_fox_f"][j])
                o = fox_flash(q.reshape(bsz, t, d), k.reshape(bsz, t, d), v.reshape(bsz, t, d), cum, cum_t)
                o = o.reshape(bsz * t, d)
                lf = lf[:, :, :FOX_HEADS]
            else:
                q, k3, v3, fl = fox_proj(x, g_mix, p["w_fox_qkv"][j], p["w_fox_f"][j], q_dtype=F32,
                                         q_scale=dh ** -0.5, split_heads=False)
                k_pool, v_pool, lft_pool, page_table = fox_paged
                lf = lf_only(fl, p["b_fox_f"][j])
                o = fox_decode(q, k3, v3, lf, k_pool, v_pool, lft_pool, page_table, j)
                lf = lf[:, :FOX_HEADS].reshape(bsz, t, FOX_HEADS)
            ks.append(k3.reshape(bsz, t, FOX_HEADS, dh))
            vs.append(v3.reshape(bsz, t, FOX_HEADS, dh))
            lfs.append(lf)
            x = matmul_res(o, p["w_fox_o"][j], x, name="fox_out")
        else:
            args = (g_mix, p["w_pool"][j], p["b_pool"][j], p["pool_scale"][j])
            if t > 1:
                x3, pb = pool_prompt(x.reshape(bsz, t, d), pool_buf[j], *args, pos0=pos0)
                x = x3.reshape(bsz * t, d)
            else:
                x, pb = pool_step(x, pool_buf[j], *args, pos0=pos0)
            pools.append(pb)

        if t > 1:
            x = xattn(x.reshape(bsz, t, d), p["norm_x_g"][layer], p["w_xq"][layer], mem_k, mem_v,
                      p["w_xo"][layer], layer).reshape(bsz * t, d)
        else:
            q = norm_matmul(x, p["norm_x_g"][layer], p["w_xq"][layer], tn=d, out_dtype=BF16,
                            scale=(d // XA_HEADS) ** -0.5, name="xattn_q")
            rows = 2 * SUBLANES
            o = mem_attn(jnp.broadcast_to(q[:, None, :], (bsz, rows, d)), mem_k, mem_v, layer)[:, 0, :]
            x = matmul_res(o, p["w_xo"][layer], x, name="xattn_out")

        final_g = p["final_norm_g"] if layer == depth - 1 else None
        x, y = mlp(x, p["norm_mlp_g"][layer], p["w_up"][layer], p["w_down"][layer], final_g)
    return y, hs, convs, pools, ks, vs, lfs


def kernel(x_prompt, x_sample, mem_prompt, cache_fox_k, cache_fox_v, cache_fox_lf, cache_mem_k, cache_mem_v, state_lru_h, state_lru_conv, state_pool, page_table, norm_mix_g, norm_mem_g, norm_x_g, norm_mlp_g, final_norm_g, w_lru_in, lru_conv_w, lru_conv_b, lru_w_rg, lru_b_rg, lru_w_ig, lru_b_ig, lru_lambda, w_lru_out, w_fox_qkvf, b_fox_f, w_fox_o, w_pool, b_pool, pool_scale, w_xq, w_xkv, w_xo, w_up, w_down):
    bsz, seq, d = x_prompt.shape
    dec, dec_seq, _ = x_sample.shape
    assert dec_seq == 1
    depth = norm_mix_g.shape[0]
    n_mem = mem_prompt.shape[1]
    n_fox = w_fox_qkvf.shape[0]
    n_lru = w_lru_in.shape[0]
    n_pool_layers = w_pool.shape[0]
    c = w_lru_out.shape[1]
    dt = x_prompt.dtype

    bias_pad = jnp.pad(b_fox_f, ((0, 0), (0, LANES - FOX_HEADS))).reshape(n_fox, 1, LANES)
    p = dict(
        norm_mix_g=norm_mix_g, norm_x_g=norm_x_g, norm_mlp_g=norm_mlp_g, final_norm_g=final_norm_g,
        w_lru_in=w_lru_in.astype(BF16), lru_conv_w=lru_conv_w, lru_conv_b=lru_conv_b,
        lru_wg=jnp.stack([_block_diag_gate_weights(lru_w_rg[l], lru_w_ig[l]) for l in range(n_lru)]),
        lru_b_rg=lru_b_rg, lru_b_ig=lru_b_ig, lru_lambda=lru_lambda, w_lru_out=w_lru_out.astype(BF16),
        w_fox_qkv=w_fox_qkvf[:, :, :3 * d].astype(BF16),
        w_fox_f=jnp.pad(w_fox_qkvf[:, :, 3 * d:], ((0, 0), (0, 0), (0, LANES - FOX_HEADS))).astype(BF16),
        b_fox_f=bias_pad, w_fox_o=w_fox_o.astype(BF16),
        w_pool=w_pool.astype(BF16), b_pool=b_pool, pool_scale=pool_scale,
        w_xq=w_xq.astype(BF16), w_xo=w_xo.astype(BF16), w_up=w_up.astype(BF16), w_down=w_down.astype(BF16),
    )

    mem_k_p, mem_v_p = mem_kv(mem_prompt.reshape(bsz * n_mem, d), norm_mem_g, w_xkv.astype(BF16))
    mem_k_p = mem_k_p.reshape(depth, bsz, n_mem, d)
    mem_v_p = mem_v_p.reshape(depth, bsz, n_mem, d)
    h0 = jnp.zeros((n_lru, bsz, c), dt)
    c0 = jnp.zeros((n_lru, bsz, CONV_WIDTH - 1, c), dt)
    pb0 = jnp.zeros((n_pool_layers, bsz, max(POOL_WINDOWS) - 1, d), dt)
    y_p, hs_p, convs_p, pools_p, ks_p, vs_p, lfs_p = _trunk(
        x_prompt.reshape(bsz * seq, d), bsz, seq, 0, mem_k_p, mem_v_p, h0, c0, pb0, None, p)

    n_pool_pages, page = cache_fox_k.shape[1], cache_fox_k.shape[2]
    pos_s = page_table.shape[1] * page
    channel_major = lambda c: jnp.transpose(c, (0, 1, 3, 4, 2)).reshape(n_fox, n_pool_pages, d, page)
    fox_paged = (channel_major(cache_fox_k), channel_major(cache_fox_v), jnp.swapaxes(cache_fox_lf, 2, 3), page_table)
    y_s, hs_s, convs_s, pools_s, ks_s, vs_s, lfs_s = _trunk(
        x_sample.reshape(dec, d), dec, 1, pos_s, cache_mem_k, cache_mem_v,
        state_lru_h, state_lru_conv, state_pool, fox_paged, p)

    xa = (depth, bsz, n_mem, XA_HEADS, d // XA_HEADS)
    return (y_p.reshape(bsz, seq, d), y_s.reshape(dec, 1, d),
            jnp.stack(hs_p), jnp.stack(convs_p), jnp.stack(pools_p), jnp.stack(ks_p), jnp.stack(vs_p),
            jnp.stack(lfs_p), mem_k_p.reshape(xa), mem_v_p.reshape(xa),
            jnp.stack(hs_s), jnp.stack(convs_s), jnp.stack(pools_s), jnp.stack(ks_s), jnp.stack(vs_s),
            jnp.stack(lfs_s))
```
